```python
import jax, jax.numpy as jnp
from jax import lax
import numpy as np

D_MODEL = 1024
BATCH = 8
SEQ = 4096
DEPTH = 2

CHUNK = 64
Q_BLOCK = 128
D_FF = 2816
EPS = 1e-6
ROPE_BASE = 10000.0
ROPE_DIM = 32

MLA_HEADS = 8
MLA_Q_RANK = 256
MLA_KV_RANK = 128
MLA_NOPE = 64
MLA_ROPE = ROPE_DIM
MLA_V = 64
GLA_HEADS = 4
GLA_DK = 32
GLA_DV = 64
GLA_GATE_RANK = 16
GLA_TAU = 16.0
RET_HEADS = 4
RET_DK = ROPE_DIM
RET_DV = 64

MLA_WIDTH = MLA_HEADS * MLA_V
GLA_WIDTH = GLA_HEADS * GLA_DV
RET_WIDTH = RET_HEADS * RET_DV
MIX_WIDTH = MLA_WIDTH + GLA_WIDTH + RET_WIDTH

IN_SPLITS = (
    MLA_Q_RANK, MLA_KV_RANK, MLA_ROPE,
    GLA_HEADS * GLA_DK, GLA_HEADS * GLA_DK, GLA_WIDTH,
    GLA_GATE_RANK, GLA_WIDTH,
    RET_HEADS * RET_DK, RET_HEADS * RET_DK, RET_WIDTH,
    RET_WIDTH,
)
D_IN = sum(IN_SPLITS)

kernel_name = "hybrid_mla_gla_retention_macaron"


def rms_norm(x, g):
    xf = x.astype(jnp.float32)
    y = xf * lax.rsqrt(jnp.mean(xf * xf, axis=-1, keepdims=True) + EPS)
    return (y * g.astype(jnp.float32)).astype(x.dtype)


def swiglu(x, w_gate, w_up, w_down):
    return (jax.nn.silu(x @ w_gate) * (x @ w_up)) @ w_down


def rope_tables(positions):
    inv = ROPE_BASE ** (-jnp.arange(0, ROPE_DIM, 2, dtype=jnp.float32) / ROPE_DIM)
    ang = positions.astype(jnp.float32)[..., None] * inv
    return jnp.cos(ang), jnp.sin(ang)


def apply_rope(x, cos, sin):
    xf = x.astype(jnp.float32)
    x1, x2 = jnp.split(xf, 2, axis=-1)
    c, s = cos[:, :, None, :], sin[:, :, None, :]
    return jnp.concatenate([x1 * c - x2 * s, x1 * s + x2 * c], axis=-1).astype(x.dtype)


def block_causal_attention(q, k, v):
    B, S, H, Dq = q.shape
    Dv = v.shape[-1]
    nb = S // Q_BLOCK
    scale = Dq ** -0.5
    qb = jnp.moveaxis(q.reshape(B, nb, Q_BLOCK, H, Dq), 1, 0)
    q_chunk = (jnp.arange(S) // CHUNK).reshape(nb, Q_BLOCK)
    k_chunk = jnp.arange(S) // CHUNK

    def one_block(args):
        q_blk, qc = args
        s = jnp.einsum('bqhd,bkhd->bhqk', q_blk, k,
                       preferred_element_type=jnp.float32) * scale
        mask = k_chunk[None, :] <= qc[:, None]
        s = jnp.where(mask, s, jnp.finfo(jnp.float32).min)
        p = jax.nn.softmax(s, axis=-1).astype(v.dtype)
        return jnp.einsum('bhqk,bkhe->bqhe', p, v)

    out = lax.map(one_block, (qb, q_chunk))
    return jnp.moveaxis(out, 0, 1).reshape(B, S, H, Dv)


def chunk_state_scan(d_state, decay):
    def step(s, inp):
        ds, dec = inp
        return dec * s + ds, s
    s0 = jnp.zeros_like(d_state[:, 0])
    _, s_before = lax.scan(step, s0, (jnp.moveaxis(d_state, 1, 0), jnp.moveaxis(decay, 1, 0)))
    return jnp.moveaxis(s_before, 0, 1)


def gla_chunked(q, k, v, log_a):
    B, S, H, DK = q.shape
    DV = v.shape[-1]
    N, L = S // CHUNK, CHUNK
    r = lambda t: t.reshape(B, N, L, H, t.shape[-1]).astype(jnp.float32)
    q, k, v, g = r(q), r(k), r(v), r(log_a)
    b = jnp.cumsum(g, axis=2)
    b_last = b[:, :, -1]
    eb, enb = jnp.exp(b), jnp.exp(-b)
    q_fwd, k_fwd = q * eb, k * enb
    a_low = jnp.einsum('bnihd,bnjhd->bnhij', q_fwd, k_fwd)
    a_up = jnp.einsum('bnihd,bnjhd->bnhij', q * enb, k * eb)
    lower = jnp.tril(jnp.ones((L, L), dtype=bool))
    attn = jnp.where(lower, a_low, a_up)
    o_intra = jnp.einsum('bnhij,bnjhe->bnihe', attn, v)
    k_state = k * jnp.exp(b_last[:, :, None] - b)
    d_state = jnp.einsum('bnjhd,bnjhe->bnhde', k_state, v)
    s_before = chunk_state_scan(d_state, jnp.exp(b_last)[..., None])
    o_inter = jnp.einsum('bnihd,bnhde->bnihe', q_fwd, s_before)
    return (o_intra + o_inter).reshape(B, S, H, DV)


def retention_chunked(q, k, v):
    B, S, H, DK = q.shape
    DV = v.shape[-1]
    N, L = S // CHUNK, CHUNK
    r = lambda t: t.reshape(B, N, L, H, t.shape[-1]).astype(jnp.float32)
    q, k, v = r(q), r(k), r(v)
    log_gamma = jnp.log1p(-jnp.exp2(-5.0 - jnp.arange(H, dtype=jnp.float32)))
    idx = jnp.arange(L, dtype=jnp.float32)
    dist = jnp.abs(idx[:, None] - idx[None, :])
    decay_mat = jnp.exp(log_gamma[:, None, None] * dist)
    scores = jnp.einsum('bnihd,bnjhd->bnhij', q, k) * decay_mat
    o_intra = jnp.einsum('bnhij,bnjhe->bnihe', scores, v)
    q_dec = jnp.exp(log_gamma[None, :] * (idx[:, None] + 1.0))
    k_dec = jnp.exp(log_gamma[None, :] * (L - 1.0 - idx[:, None]))
    d_state = jnp.einsum('bnjhd,bnjhe,jh->bnhde', k, v, k_dec)
    chunk_dec = jnp.broadcast_to(jnp.exp(log_gamma * L)[None, None, :, None, None], (1, N, H, 1, 1))
    s_before = chunk_state_scan(d_state, chunk_dec)
    o_inter = jnp.einsum('bnihd,bnhde,ih->bnihe', q, s_before, q_dec)
    return (o_intra + o_inter).reshape(B, S, H, DV)


def hybrid_mixer(h, cos, sin, w_in, mla_q_norm_g, mla_w_q_up, mla_kv_norm_g, mla_w_kv_up,
                 gla_w_gate_up, gla_b_gate, gla_norm_g, ret_norm_g, w_out):
    B, S, _ = h.shape
    offsets = np.cumsum(IN_SPLITS)[:-1].tolist()
    (mq, mkv, mkr, gq, gk, gv, gg, gr, rq, rk, rv, rg) = jnp.split(h @ w_in, offsets, axis=-1)

    qh = (rms_norm(mq, mla_q_norm_g) @ mla_w_q_up).reshape(B, S, MLA_HEADS, MLA_NOPE + MLA_ROPE)
    q_pe = apply_rope(qh[..., MLA_NOPE:], cos, sin)
    kvh = (rms_norm(mkv, mla_kv_norm_g) @ mla_w_kv_up).reshape(B, S, MLA_HEADS, MLA_NOPE + MLA_V)
    k_pe = apply_rope(mkr[:, :, None, :], cos, sin)
    q_mla = jnp.concatenate([qh[..., :MLA_NOPE], q_pe], axis=-1)
    k_mla = jnp.concatenate([kvh[..., :MLA_NOPE],
                             jnp.broadcast_to(k_pe, (B, S, MLA_HEADS, MLA_ROPE))], axis=-1)
    o_mla = block_causal_attention(q_mla, k_mla, kvh[..., MLA_NOPE:]).reshape(B, S, MLA_WIDTH)

    g_q = gq.reshape(B, S, GLA_HEADS, GLA_DK) * (GLA_DK ** -0.5)
    g_k = gk.reshape(B, S, GLA_HEADS, GLA_DK)
    g_v = gv.reshape(B, S, GLA_HEADS, GLA_DV)
    log_a = jax.nn.log_sigmoid((gg @ gla_w_gate_up + gla_b_gate).astype(jnp.float32)) / GLA_TAU
    o_g = gla_chunked(g_q, g_k, g_v, log_a.reshape(B, S, GLA_HEADS, GLA_DK)).astype(h.dtype)
    o_gla = (rms_norm(o_g, gla_norm_g) * jax.nn.silu(gr.reshape(B, S, GLA_HEADS, GLA_DV))
             ).reshape(B, S, GLA_WIDTH)

    r_q = apply_rope(rq.reshape(B, S, RET_HEADS, RET_DK), cos, sin)
    r_k = apply_rope(rk.reshape(B, S, RET_HEADS, RET_DK), cos, sin) * (RET_DK ** -0.5)
    o_r = retention_chunked(r_q, r_k, rv.reshape(B, S, RET_HEADS, RET_DV)).astype(h.dtype)
    o_ret = (rms_norm(o_r, ret_norm_g) * jax.nn.silu(rg.reshape(B, S, RET_HEADS, RET_DV))
             ).reshape(B, S, RET_WIDTH)

    return jnp.concatenate([o_mla, o_gla, o_ret], axis=-1) @ w_out


def setup_inputs(seed: int = 0) -> dict:
    key = jax.random.key(seed)
    ks = jax.random.split(key, 24)
    f32 = jnp.float32
    nrm = lambda k, shape, fan_in: jax.random.normal(k, shape, f32) * (fan_in ** -0.5)
    gain = lambda k, n: 1.0 + 0.02 * jax.random.normal(k, (DEPTH, n), f32)
    x = jax.random.normal(ks[0], (BATCH, SEQ, D_MODEL), f32)
    offset = jax.random.randint(ks[1], (BATCH, 1), 0, SEQ, dtype=jnp.int32)
    positions = offset + jnp.arange(SEQ, dtype=jnp.int32)[None, :]
    return {
        "x": x,
        "positions": positions,
        "ffn1_pre_g": gain(ks[2], D_MODEL),
        "ffn1_post_g": gain(ks[3], D_MODEL),
        "ffn1_w_gate": nrm(ks[4], (DEPTH, D_MODEL, D_FF), D_MODEL),
        "ffn1_w_up": nrm(ks[5], (DEPTH, D_MODEL, D_FF), D_MODEL),
        "ffn1_w_down": nrm(ks[6], (DEPTH, D_FF, D_MODEL), D_FF),
        "mix_pre_g": gain(ks[7], D_MODEL),
        "mix_post_g": gain(ks[8], D_MODEL),
        "w_in": nrm(ks[9], (DEPTH, D_MODEL, D_IN), D_MODEL),
        "mla_q_norm_g": gain(ks[10], MLA_Q_RANK),
        "mla_w_q_up": nrm(ks[11], (DEPTH, MLA_Q_RANK, MLA_HEADS * (MLA_NOPE + MLA_ROPE)), MLA_Q_RANK),
        "mla_kv_norm_g": gain(ks[12], MLA_KV_RANK),
        "mla_w_kv_up": nrm(ks[13], (DEPTH, MLA_KV_RANK, MLA_HEADS * (MLA_NOPE + MLA_V)), MLA_KV_RANK),
        "gla_w_gate_up": nrm(ks[14], (DEPTH, GLA_GATE_RANK, GLA_HEADS * GLA_DK), GLA_GATE_RANK),
        "gla_b_gate": 0.1 * jax.random.normal(ks[15], (DEPTH, GLA_HEADS * GLA_DK), f32),
        "gla_norm_g": gain(ks[16], GLA_DV),
        "ret_norm_g": gain(ks[17], RET_DV),
        "w_out": nrm(ks[18], (DEPTH, MIX_WIDTH, D_MODEL), MIX_WIDTH),
        "ffn2_pre_g": gain(ks[19], D_MODEL),
        "ffn2_post_g": gain(ks[20], D_MODEL),
        "ffn2_w_gate": nrm(ks[21], (DEPTH, D_MODEL, D_FF), D_MODEL),
        "ffn2_w_up": nrm(ks[22], (DEPTH, D_MODEL, D_FF), D_MODEL),
        "ffn2_w_down": nrm(ks[23], (DEPTH, D_FF, D_MODEL), D_FF),
    }


def reference(x, positions, ffn1_pre_g, ffn1_post_g, ffn1_w_gate, ffn1_w_up, ffn1_w_down,
              mix_pre_g, mix_post_g, w_in, mla_q_norm_g, mla_w_q_up, mla_kv_norm_g, mla_w_kv_up,
              gla_w_gate_up, gla_b_gate, gla_norm_g, ret_norm_g, w_out,
              ffn2_pre_g, ffn2_post_g, ffn2_w_gate, ffn2_w_up, ffn2_w_down):
    cos, sin = rope_tables(positions)
    for l in range(DEPTH):
        f = swiglu(rms_norm(x, ffn1_pre_g[l]), ffn1_w_gate[l], ffn1_w_up[l], ffn1_w_down[l])
        x = x + 0.5 * rms_norm(f, ffn1_post_g[l])
        m = hybrid_mixer(rms_norm(x, mix_pre_g[l]), cos, sin, w_in[l],
                         mla_q_norm_g[l], mla_w_q_up[l], mla_kv_norm_g[l], mla_w_kv_up[l],
                         gla_w_gate_up[l], gla_b_gate[l], gla_norm_g[l], ret_norm_g[l], w_out[l])
        x = x + rms_norm(m, mix_post_g[l])
        f = swiglu(rms_norm(x, ffn2_pre_g[l]), ffn2_w_gate[l], ffn2_w_up[l], ffn2_w_down[l])
        x = x + 0.5 * rms_norm(f, ffn2_post_g[l])
    return x
```

```python
import functools
import math

import jax
import jax.numpy as jnp
from jax import lax
from jax.experimental import pallas as pl
from jax.experimental.pallas import tpu as pltpu

F32 = jnp.float32
BF16 = jnp.bfloat16

D_MODEL = 1024
D_FF = 2816
CHUNK = 64
EPS = 1e-6
ROPE_BASE = 10000.0
ROPE_DIM = 32
HALF_ROPE = ROPE_DIM // 2

MLA_HEADS = 8
MLA_Q_RANK = 256
MLA_KV_RANK = 128
MLA_NOPE = 64
MLA_V = 64
GLA_HEADS = 4
GLA_DK = 32
GLA_DV = 64
GLA_GATE_RANK = 16
GLA_TAU = 16.0
RET_HEADS = 4
RET_DK = 32
RET_DV = 64

LANES = 128
MLA_WIDTH = MLA_HEADS * MLA_V
GLA_WIDTH = GLA_HEADS * GLA_DV
RET_WIDTH = RET_HEADS * RET_DV
QK_WIDTH = GLA_HEADS * GLA_DK
HEAD_PAD = MLA_HEADS * LANES

C_MQ = 0
C_MKV = C_MQ + MLA_Q_RANK
C_KPE = C_MKV + MLA_KV_RANK
C_KPE_ROT = C_KPE + LANES
C_GQ = C_KPE_ROT + LANES
C_GK = C_GQ + QK_WIDTH
C_GV = C_GK + QK_WIDTH
C_GR = C_GV + GLA_WIDTH
C_RQ = C_GR + GLA_WIDTH
C_RQ_ROT = C_RQ + QK_WIDTH
C_RK = C_RQ_ROT + QK_WIDTH
C_RK_ROT = C_RK + QK_WIDTH
C_RV = C_RK_ROT + QK_WIDTH
C_RG = C_RV + RET_WIDTH
D_IN_EXT = C_RG + RET_WIDTH

VMEM_LIMIT = 56 * 1024 * 1024


def _rms(x, g):
    return x * lax.rsqrt(jnp.mean(x * x, axis=-1, keepdims=True) + EPS) * g


def _silu(x):
    return x * jax.nn.sigmoid(x)


def _cparams(*sem):
    return pltpu.CompilerParams(dimension_semantics=sem, vmem_limit_bytes=VMEM_LIMIT)


def _const_spec(shape):
    nd = len(shape)
    return pl.BlockSpec(shape, lambda *_: (0,) * nd)


def _ffn_kernel(x_ref, pre_ref, post_ref, wg_ref, wu_ref, wd_ref, o_ref, h_ref, *, tf):
    x = x_ref[...]
    xn = _rms(x, pre_ref[...]).astype(BF16)
    for c in range(D_FF // tf):
        sl = slice(c * tf, (c + 1) * tf)
        g = jnp.dot(xn, wg_ref[:, sl], preferred_element_type=F32)
        u = jnp.dot(xn, wu_ref[:, sl], preferred_element_type=F32)
        h_ref[:, sl] = (_silu(g) * u).astype(BF16)
    f = jnp.dot(h_ref[...], wd_ref[...], preferred_element_type=F32)
    o_ref[...] = x + 0.5 * _rms(f, post_ref[...])


def _ffn(x, pre_g, post_g, wg, wu, wd, *, tm, tf):
    t = x.shape[0]
    return pl.pallas_call(
        functools.partial(_ffn_kernel, tf=tf),
        out_shape=jax.ShapeDtypeStruct(x.shape, F32),
        grid=(t // tm,),
        in_specs=[
            pl.BlockSpec((tm, D_MODEL), lambda i: (i, 0)),
            _const_spec((1, D_MODEL)),
            _const_spec((1, D_MODEL)),
            _const_spec((D_MODEL, D_FF)),
            _const_spec((D_MODEL, D_FF)),
            _const_spec((D_FF, D_MODEL)),
        ],
        out_specs=pl.BlockSpec((tm, D_MODEL), lambda i: (i, 0)),
        scratch_shapes=[pltpu.VMEM((tm, D_FF), BF16)],
        compiler_params=_cparams("parallel"),
        name="ffn",
    )(x, pre_g, post_g, wg, wu, wd)


def _rope_kernel(pos_ref, inv_ref, cos_ref, sin_ref):
    ang = pos_ref[...].astype(F32) * inv_ref[...]
    cos_ref[...] = jnp.cos(ang)
    sin_ref[...] = jnp.sin(ang)


def _rope_tables(pos_col, inv_lane, *, tm):
    t = pos_col.shape[0]
    return pl.pallas_call(
        _rope_kernel,
        out_shape=(jax.ShapeDtypeStruct((t, LANES), F32),) * 2,
        grid=(t // tm,),
        in_specs=[pl.BlockSpec((tm, 1), lambda i: (i, 0)), _const_spec((1, LANES))],
        out_specs=(pl.BlockSpec((tm, LANES), lambda i: (i, 0)),) * 2,
        compiler_params=_cparams("parallel"),
        name="rope_tables",
    )(pos_col, inv_lane)


def _proj_kernel(x_ref, cos_ref, sin_ref, pre_ref, win_ref, qg_ref, wq_ref, kvg_ref, wkv_ref,
                 wgate_ref, bgate_ref, tri_ref,
                 q_ref, k_ref, v_ref, gq_ref, gk_ref, b_ref, gv_ref, sg_ref,
                 rq_ref, rk_ref, rv_ref, sr_ref):
    tm = x_ref.shape[0]
    xn = _rms(x_ref[...], pre_ref[...]).astype(BF16)

    def proj(c0, width):
        return jnp.dot(xn, win_ref[:, c0:c0 + width], preferred_element_type=F32)

    cosb = cos_ref[...]
    lane = lax.broadcasted_iota(jnp.int32, (tm, LANES), 1)
    sins = jnp.where((lane % ROPE_DIM) < HALF_ROPE, -sin_ref[...], sin_ref[...])

    hq = _rms(proj(C_MQ, MLA_Q_RANK), qg_ref[...]).astype(BF16)
    in_rope = (lane >= MLA_NOPE) & (lane < MLA_NOPE + ROPE_DIM)
    qcos = jnp.where(in_rope, cosb, 1.0)
    qsin = jnp.where(in_rope, sins, 0.0)
    q_scale = (MLA_NOPE + ROPE_DIM) ** -0.5
    for h in range(MLA_HEADS):
        sl = slice(h * LANES, (h + 1) * LANES)
        qa = jnp.dot(hq, wq_ref[:, sl], preferred_element_type=F32)
        qb = jnp.dot(hq, wq_ref[:, HEAD_PAD + h * LANES:HEAD_PAD + (h + 1) * LANES],
                     preferred_element_type=F32)
        q_ref[:, sl] = ((qa * qcos + qb * qsin) * q_scale).astype(BF16)

    hkv = _rms(proj(C_MKV, MLA_KV_RANK), kvg_ref[...]).astype(BF16)
    g1 = proj(C_KPE, LANES)
    g2 = proj(C_KPE_ROT, LANES)
    kpe = jnp.where(in_rope, g1 * cosb + g2 * sins, 0.0)
    ones_col = jnp.where(lane == MLA_V, 1.0, 0.0)
    for h in range(MLA_HEADS):
        sl = slice(h * LANES, (h + 1) * LANES)
        kn = jnp.dot(hkv, wkv_ref[:, sl], preferred_element_type=F32)
        k_ref[:, sl] = (kn + kpe).astype(BF16)
        vv = jnp.dot(hkv, wkv_ref[:, HEAD_PAD + h * LANES:HEAD_PAD + (h + 1) * LANES],
                     preferred_element_type=F32)
        v_ref[:, sl] = (vv + ones_col).astype(BF16)

    z = jnp.dot(g1.astype(BF16), wgate_ref[...], preferred_element_type=F32) + bgate_ref[...]
    log_a = (jnp.minimum(z, 0.0) - jnp.log1p(jnp.exp(-jnp.abs(z)))) / GLA_TAU
    la_hi = log_a.astype(BF16)
    la_lo = (log_a - la_hi.astype(F32)).astype(BF16)
    tri = tri_ref[...]
    tb = tri.shape[0]
    for r in range(tm // tb):
        rs = slice(r * tb, (r + 1) * tb)
        b_ref[rs, :] = (jnp.dot(tri, la_hi[rs], preferred_element_type=F32)
                        + jnp.dot(tri, la_lo[rs], preferred_element_type=F32))
    gq_ref[...] = (proj(C_GQ, QK_WIDTH) * (GLA_DK ** -0.5)).astype(BF16)
    gk_ref[...] = proj(C_GK, QK_WIDTH).astype(BF16)
    gv_ref[...] = proj(C_GV, GLA_WIDTH).astype(BF16)
    sg_ref[...] = _silu(proj(C_GR, GLA_WIDTH)).astype(BF16)

    rq_ref[...] = (proj(C_RQ, QK_WIDTH) * cosb + proj(C_RQ_ROT, QK_WIDTH) * sins).astype(BF16)
    rk_ref[...] = ((proj(C_RK, QK_WIDTH) * cosb + proj(C_RK_ROT, QK_WIDTH) * sins)
                   * (RET_DK ** -0.5)).astype(BF16)
    rv_ref[...] = proj(C_RV, RET_WIDTH).astype(BF16)
    sr_ref[...] = _silu(proj(C_RG, RET_WIDTH)).astype(BF16)


def _proj(x, cosb, sinb, lw, tri, *, tm):
    t = x.shape[0]
    row = lambda w: pl.BlockSpec((tm, w), lambda i: (i, 0))
    out_widths = [(HEAD_PAD, BF16), (HEAD_PAD, BF16), (HEAD_PAD, BF16),
                  (QK_WIDTH, BF16), (QK_WIDTH, BF16), (QK_WIDTH, F32), (GLA_WIDTH, BF16), (GLA_WIDTH, BF16),
                  (QK_WIDTH, BF16), (QK_WIDTH, BF16), (RET_WIDTH, BF16), (RET_WIDTH, BF16)]
    consts = [lw["mix_pre_g"], lw["w_in"], lw["q_norm_g"], lw["wq"], lw["kv_norm_g"], lw["wkv"],
              lw["w_gate"], lw["b_gate"], tri]
    return pl.pallas_call(
        _proj_kernel,
        out_shape=tuple(jax.ShapeDtypeStruct((t, w), dt) for w, dt in out_widths),
        grid=(t // tm,),
        in_specs=[row(D_MODEL), row(LANES), row(LANES)] + [_const_spec(c.shape) for c in consts],
        out_specs=tuple(row(w) for w, _ in out_widths),
        compiler_params=_cparams("parallel"),
        name="mixer_proj",
    )(x, cosb, sinb, *consts)


def _attn_kernel(q_ref, k_ref, v_ref, o_ref, m_ref, acc_ref, *, tk):
    tq = q_ref.shape[0]
    qi = pl.program_id(1)
    row_chunk = lax.broadcasted_iota(jnp.int32, (tq, tk), 0) // CHUNK
    col_chunk = lax.broadcasted_iota(jnp.int32, (tq, tk), 1) // CHUNK
    diag_mask = col_chunk <= row_chunk
    lane = lax.broadcasted_iota(jnp.int32, (tq, LANES), 1)

    outs = []
    for h in range(MLA_HEADS):
        hs = slice(h * LANES, (h + 1) * LANES)
        q = q_ref[:, hs]
        m_ref[...] = jnp.full(m_ref.shape, -jnp.inf, F32)
        acc_ref[...] = jnp.zeros(acc_ref.shape, F32)

        def block(j, masked):
            rows = pl.ds(pl.multiple_of(j * tk, tk), tk)
            s = lax.dot_general(q, k_ref[rows, hs], (((1,), (1,)), ((), ())),
                                preferred_element_type=F32)
            if masked:
                s = jnp.where(diag_mask, s, -jnp.inf)
            m_old = m_ref[...]
            m_new = jnp.maximum(m_old, jnp.max(s, axis=-1, keepdims=True))
            p = jnp.exp(s - m_new)
            acc_ref[...] = (acc_ref[...] * jnp.exp(m_old - m_new)
                            + jnp.dot(p.astype(BF16), v_ref[rows, hs], preferred_element_type=F32))
            m_ref[...] = m_new

        def body(j, carry):
            block(j, False)
            return carry

        lax.fori_loop(0, qi, body, 0)
        block(qi, True)
        acc = acc_ref[...]
        outs.append(acc / acc[:, MLA_V:MLA_V + 1])

    for p in range(MLA_HEADS // 2):
        odd = pltpu.roll(outs[2 * p + 1], MLA_V, axis=1)
        o_ref[:, p * LANES:(p + 1) * LANES] = jnp.where(lane < MLA_V, outs[2 * p], odd).astype(BF16)


def _attention(q, k, v, *, batch, seq, tq, tk):
    nq = seq // tq
    assert tq == tk
    return pl.pallas_call(
        functools.partial(_attn_kernel, tk=tk),
        out_shape=jax.ShapeDtypeStruct((batch * seq, MLA_WIDTH), BF16),
        grid=(batch, nq),
        in_specs=[
            pl.BlockSpec((tq, HEAD_PAD), lambda b, i: (b * nq + i, 0)),
            pl.BlockSpec((seq, HEAD_PAD), lambda b, i: (b, 0)),
            pl.BlockSpec((seq, HEAD_PAD), lambda b, i: (b, 0)),
        ],
        out_specs=pl.BlockSpec((tq, MLA_WIDTH), lambda b, i: (b * nq + i, 0)),
        scratch_shapes=[pltpu.VMEM((tq, 1), F32), pltpu.VMEM((tq, LANES), F32)],
        compiler_params=_cparams("parallel", "arbitrary"),
        name="mla_attention",
    )(q, k, v)


def _head_rows(x, n_heads, head_lanes):
    lane_head = lax.broadcasted_iota(jnp.int32, x.shape, 1) // head_lanes
    return jnp.concatenate([jnp.where(lane_head == h, x, 0.0).astype(BF16) for h in range(n_heads)], axis=0)


def _linmix_kernel(gq_ref, gk_ref, b_ref, gv_ref, sg_ref, rq_ref, rk_ref, rv_ref, sr_ref,
                   gng_ref, rng_ref, lgam_ref, ones_ref, o_ref, sg_state, sr_state, og_ref, or_ref):
    tc = gq_ref.shape[0]
    L = CHUNK

    @pl.when(pl.program_id(1) == 0)
    def _():
        sg_state[...] = jnp.zeros(sg_state.shape, F32)
        sr_state[...] = jnp.zeros(sr_state.shape, F32)

    ri = lax.broadcasted_iota(jnp.int32, (L, GLA_HEADS * L), 0)
    cj = lax.broadcasted_iota(jnp.int32, (L, GLA_HEADS * L), 1) % L
    lower = cj <= ri
    lgam = lgam_ref[...]
    lgam_cols = jnp.concatenate(
        [jnp.broadcast_to(lgam[:, h * RET_DK:h * RET_DK + 1], (1, L)) for h in range(RET_HEADS)], axis=1)
    ret_decay = jnp.exp(lgam_cols * jnp.abs(ri - cj).astype(F32))
    pos = lax.broadcasted_iota(jnp.int32, (L, QK_WIDTH), 0).astype(F32)
    q_dec = jnp.exp(lgam * (pos + 1.0))
    k_dec = jnp.exp(lgam * (L - 1.0 - pos))
    chunk_dec = jnp.exp(lgam * float(L))
    state_head = lax.broadcasted_iota(jnp.int32, (GLA_WIDTH, QK_WIDTH), 0) // GLA_DV
    state_lane_head = lax.broadcasted_iota(jnp.int32, (GLA_WIDTH, QK_WIDTH), 1) // GLA_DK
    state_diag = state_head == state_lane_head
    nt = (((1,), (1,)), ((), ()))

    for c in range(tc // L):
        rs = slice(c * L, (c + 1) * L)
        b = b_ref[rs, :]
        b_last = b[L - 1:L, :]
        eb, enb = jnp.exp(b), jnp.exp(-b)
        q = gq_ref[rs, :].astype(F32)
        k = gk_ref[rs, :].astype(F32)
        v = gv_ref[rs, :].astype(F32)
        q_fwd = (q * eb).astype(BF16)
        a_low = lax.dot_general(q_fwd, _head_rows(k * enb, GLA_HEADS, GLA_DK), nt, preferred_element_type=F32)
        a_up = lax.dot_general((q * enb).astype(BF16), _head_rows(k * eb, GLA_HEADS, GLA_DK), nt,
                               preferred_element_type=F32)
        attn = jnp.where(lower, a_low, a_up).astype(BF16)
        o_intra = jnp.dot(attn, _head_rows(v, GLA_HEADS, GLA_DV), preferred_element_type=F32)
        s_t = sg_state[...]
        o_inter = lax.dot_general(q_fwd, s_t.astype(BF16), nt, preferred_element_type=F32)
        og_ref[rs, :] = o_intra + o_inter
        k_state = (k * jnp.exp(b_last - b)).astype(BF16)
        d_state = jnp.dot(v.T.astype(BF16), k_state, preferred_element_type=F32)
        sg_state[...] = s_t * jnp.exp(b_last) + jnp.where(state_diag, d_state, 0.0)

        q = rq_ref[rs, :].astype(F32)
        k = rk_ref[rs, :].astype(F32)
        v = rv_ref[rs, :].astype(F32)
        scores = lax.dot_general(q.astype(BF16), _head_rows(k, RET_HEADS, RET_DK), nt,
                                 preferred_element_type=F32) * ret_decay
        o_intra = jnp.dot(scores.astype(BF16), _head_rows(v, RET_HEADS, RET_DV), preferred_element_type=F32)
        s_t = sr_state[...]
        o_inter = lax.dot_general((q * q_dec).astype(BF16), s_t.astype(BF16), nt, preferred_element_type=F32)
        or_ref[rs, :] = o_intra + o_inter
        d_state = jnp.dot(v.T.astype(BF16), (k * k_dec).astype(BF16), preferred_element_type=F32)
        sr_state[...] = s_t * chunk_dec + jnp.where(state_diag, d_state, 0.0)

    def head_norm(o, g, gate):
        o2 = o * o
        hi = o2.astype(BF16)
        lo = (o2 - hi.astype(F32)).astype(BF16)
        ss = (jnp.dot(hi, ones_ref[...], preferred_element_type=F32)
              + jnp.dot(lo, ones_ref[...], preferred_element_type=F32))
        return o * lax.rsqrt(ss * (1.0 / GLA_DV) + EPS) * g * gate

    o_ref[:, :GLA_WIDTH] = head_norm(og_ref[...], gng_ref[...], sg_ref[...].astype(F32)).astype(BF16)
    o_ref[:, GLA_WIDTH:] = head_norm(or_ref[...], rng_ref[...], sr_ref[...].astype(F32)).astype(BF16)


def _linmix(gq, gk, b, gv, sg, rq, rk, rv, sr, gng, rng, lgam, ones_bd, *, batch, seq, tc):
    nt = seq // tc
    row = lambda w: pl.BlockSpec((tc, w), lambda bb, i: (bb * nt + i, 0))
    consts = [gng, rng, lgam, ones_bd]
    return pl.pallas_call(
        _linmix_kernel,
        out_shape=jax.ShapeDtypeStruct((batch * seq, GLA_WIDTH + RET_WIDTH), BF16),
        grid=(batch, nt),
        in_specs=[row(QK_WIDTH), row(QK_WIDTH), row(QK_WIDTH), row(GLA_WIDTH), row(GLA_WIDTH),
                  row(QK_WIDTH), row(QK_WIDTH), row(RET_WIDTH), row(RET_WIDTH)]
                 + [_const_spec(c.shape) for c in consts],
        out_specs=row(GLA_WIDTH + RET_WIDTH),
        scratch_shapes=[pltpu.VMEM((GLA_WIDTH, QK_WIDTH), F32), pltpu.VMEM((RET_WIDTH, QK_WIDTH), F32),
                        pltpu.VMEM((tc, GLA_WIDTH), F32), pltpu.VMEM((tc, RET_WIDTH), F32)],
        compiler_params=_cparams("parallel", "arbitrary"),
        name="gla_retention",
    )(gq, gk, b, gv, sg, rq, rk, rv, sr, *consts)


def _outproj_kernel(x_ref, om_ref, ol_ref, wa_ref, wb_ref, post_ref, o_ref):
    m = (jnp.dot(om_ref[...], wa_ref[...], preferred_element_type=F32)
         + jnp.dot(ol_ref[...], wb_ref[...], preferred_element_type=F32))
    o_ref[...] = x_ref[...] + _rms(m, post_ref[...])


def _outproj(x, o_mla, o_lin, wa, wb, post_g, *, tm):
    t = x.shape[0]
    row = lambda w: pl.BlockSpec((tm, w), lambda i: (i, 0))
    return pl.pallas_call(
        _outproj_kernel,
        out_shape=jax.ShapeDtypeStruct(x.shape, F32),
        grid=(t // tm,),
        in_specs=[row(D_MODEL), row(MLA_WIDTH), row(GLA_WIDTH + RET_WIDTH),
                  _const_spec(wa.shape), _const_spec(wb.shape), _const_spec((1, D_MODEL))],
        out_specs=row(D_MODEL),
        compiler_params=_cparams("parallel"),
        name="mixer_out",
    )(x, o_mla, o_lin, wa, wb, post_g)


def _rot_cols(w):
    return jnp.concatenate([w[..., HALF_ROPE:], w[..., :HALF_ROPE]], axis=-1)


def _prep_layer(l, p):
    w_in = p["w_in"][l]
    splits = (MLA_Q_RANK, MLA_KV_RANK, ROPE_DIM, QK_WIDTH, QK_WIDTH, GLA_WIDTH, GLA_GATE_RANK, GLA_WIDTH,
              QK_WIDTH, QK_WIDTH, RET_WIDTH, RET_WIDTH)
    offs = [0]
    for s in splits:
        offs.append(offs[-1] + s)
    mq, mkv, mkr, gq, gk, gv, gg, gr, rq, rk, rv, rg = (w_in[:, offs[i]:offs[i + 1]] for i in range(12))
    d = w_in.shape[0]
    zeros = lambda n: jnp.zeros((d, n), F32)
    kpe_grp = jnp.concatenate([gg, zeros(MLA_NOPE - GLA_GATE_RANK), mkr, zeros(LANES - MLA_NOPE - ROPE_DIM)], 1)
    kpe_rot_grp = jnp.concatenate([zeros(MLA_NOPE), _rot_cols(mkr), zeros(LANES - MLA_NOPE - ROPE_DIM)], 1)
    per_head_rot = lambda w, nh: _rot_cols(w.reshape(d, nh, ROPE_DIM)).reshape(d, nh * ROPE_DIM)
    w_in_ext = jnp.concatenate(
        [mq, mkv, kpe_grp, kpe_rot_grp, gq, gk, gv, gr,
         rq, per_head_rot(rq, RET_HEADS), rk, per_head_rot(rk, RET_HEADS), rv, rg], axis=1).astype(BF16)

    wq = p["mla_w_q_up"][l].reshape(MLA_Q_RANK, MLA_HEADS, MLA_NOPE + ROPE_DIM)
    pad_q = jnp.zeros((MLA_Q_RANK, MLA_HEADS, LANES - MLA_NOPE - ROPE_DIM), F32)
    wq_main = jnp.concatenate([wq, pad_q], axis=-1).reshape(MLA_Q_RANK, HEAD_PAD)
    wq_rot = jnp.concatenate([jnp.zeros((MLA_Q_RANK, MLA_HEADS, MLA_NOPE), F32),
                              _rot_cols(wq[..., MLA_NOPE:]), pad_q], axis=-1).reshape(MLA_Q_RANK, HEAD_PAD)
    wkv = p["mla_w_kv_up"][l].reshape(MLA_KV_RANK, MLA_HEADS, MLA_NOPE + MLA_V)
    pad_kv = jnp.zeros((MLA_KV_RANK, MLA_HEADS, LANES - MLA_NOPE), F32)
    wk_pad = jnp.concatenate([wkv[..., :MLA_NOPE], pad_kv], axis=-1).reshape(MLA_KV_RANK, HEAD_PAD)
    wv_pad = jnp.concatenate([wkv[..., MLA_NOPE:], pad_kv], axis=-1).reshape(MLA_KV_RANK, HEAD_PAD)
    w_gate = jnp.concatenate([p["gla_w_gate_up"][l],
                              jnp.zeros((LANES - GLA_GATE_RANK, QK_WIDTH), F32)], axis=0)
    row = lambda a: a.reshape(1, -1)
    w_out = p["w_out"][l].astype(BF16)
    return {
        "mix_pre_g": row(p["mix_pre_g"][l]), "mix_post_g": row(p["mix_post_g"][l]),
        "w_in": w_in_ext,
        "q_norm_g": row(p["mla_q_norm_g"][l]),
        "wq": jnp.concatenate([wq_main, wq_rot], axis=1).astype(BF16),
        "kv_norm_g": row(p["mla_kv_norm_g"][l]),
        "wkv": jnp.concatenate([wk_pad, wv_pad], axis=1).astype(BF16),
        "w_gate": w_gate.astype(BF16), "b_gate": row(p["gla_b_gate"][l]),
        "gla_norm_g": row(jnp.tile(p["gla_norm_g"][l], GLA_HEADS)),
        "ret_norm_g": row(jnp.tile(p["ret_norm_g"][l], RET_HEADS)),
        "w_out_mla": w_out[:MLA_WIDTH], "w_out_lin": w_out[MLA_WIDTH:],
    }


def _ffn_params(l, p, prefix):
    return (p[prefix + "_pre_g"][l].reshape(1, -1), p[prefix + "_post_g"][l].reshape(1, -1),
            p[prefix + "_w_gate"][l].astype(BF16), p[prefix + "_w_up"][l].astype(BF16),
            p[prefix + "_w_down"][l].astype(BF16))


def _tile(n, pref):
    while n % pref:
        pref //= 2
    return pref


def kernel(x, positions, ffn1_pre_g, ffn1_post_g, ffn1_w_gate, ffn1_w_up, ffn1_w_down, mix_pre_g, mix_post_g, w_in, mla_q_norm_g, mla_w_q_up, mla_kv_norm_g, mla_w_kv_up, gla_w_gate_up, gla_b_gate, gla_norm_g, ret_norm_g, w_out, ffn2_pre_g, ffn2_post_g, ffn2_w_gate, ffn2_w_up, ffn2_w_down):
    p = dict(ffn1_pre_g=ffn1_pre_g, ffn1_post_g=ffn1_post_g, ffn1_w_gate=ffn1_w_gate, ffn1_w_up=ffn1_w_up,
             ffn1_w_down=ffn1_w_down, mix_pre_g=mix_pre_g, mix_post_g=mix_post_g, w_in=w_in,
             mla_q_norm_g=mla_q_norm_g, mla_w_q_up=mla_w_q_up, mla_kv_norm_g=mla_kv_norm_g,
             mla_w_kv_up=mla_w_kv_up, gla_w_gate_up=gla_w_gate_up, gla_b_gate=gla_b_gate,
             gla_norm_g=gla_norm_g, ret_norm_g=ret_norm_g, w_out=w_out, ffn2_pre_g=ffn2_pre_g,
             ffn2_post_g=ffn2_post_g, ffn2_w_gate=ffn2_w_gate, ffn2_w_up=ffn2_w_up, ffn2_w_down=ffn2_w_down)
    batch, seq, d = x.shape
    t = batch * seq
    depth = w_in.shape[0]
    tm = _tile(seq, 512)
    tq = _tile(seq, 512)
    tc = _tile(seq, 512)

    inv = ROPE_BASE ** (-jnp.arange(0, ROPE_DIM, 2, dtype=F32) / ROPE_DIM)
    inv_lane = jnp.tile(inv, LANES // HALF_ROPE).reshape(1, LANES)
    tb = _tile(tm, 256)
    idx = jnp.arange(tb)
    tri = ((idx[:, None] >= idx[None, :]) & (idx[:, None] // CHUNK == idx[None, :] // CHUNK)).astype(BF16)
    hd = jnp.arange(GLA_WIDTH) // GLA_DV
    ones_bd = (hd[:, None] == hd[None, :]).astype(BF16)
    log_gamma = jnp.log1p(-jnp.exp2(-5.0 - jnp.arange(RET_HEADS, dtype=F32)))
    lgam = jnp.repeat(log_gamma, RET_DK).reshape(1, QK_WIDTH)

    cosb, sinb = _rope_tables(positions.reshape(t, 1), inv_lane, tm=tm)

    xf = x.reshape(t, d)
    for l in range(depth):
        xf = _ffn(xf, *_ffn_params(l, p, "ffn1"), tm=tm, tf=256)
        lw = _prep_layer(l, p)
        (q, k, v, gq, gk, b, gv, sg, rq, rk, rv, sr) = _proj(xf, cosb, sinb, lw, tri, tm=tm)
        o_mla = _attention(q, k, v, batch=batch, seq=seq, tq=tq, tk=tq)
        o_lin = _linmix(gq, gk, b, gv, sg, rq, rk, rv, sr, lw["gla_norm_g"], lw["ret_norm_g"], lgam, ones_bd,
                        batch=batch, seq=seq, tc=tc)
        xf = _outproj(xf, o_mla, o_lin, lw["w_out_mla"], lw["w_out_lin"], lw["mix_post_g"], tm=tm)
        xf = _ffn(xf, *_ffn_params(l, p, "ffn2"), tm=tm, tf=256)
    return xf.reshape(batch, seq, d)
```

```python
import functools
import math

import jax
import jax.numpy as jnp
from jax import lax
from jax.experimental import pallas as pl
from jax.experimental.pallas import tpu as pltpu

F32 = jnp.float32
BF16 = jnp.bfloat16

D_MODEL = 1024
D_FF = 2816
CHUNK = 64
EPS = 1e-6
ROPE_BASE = 10000.0
ROPE_DIM = 32
HALF_ROPE = ROPE_DIM // 2

MLA_HEADS = 8
MLA_Q_RANK = 256
MLA_KV_RANK = 128
MLA_NOPE = 64
MLA_V = 64
GLA_HEADS = 4
GLA_DK = 32
GLA_DV = 64
GLA_GATE_RANK = 16
GLA_TAU = 16.0
RET_HEADS = 4
RET_DK = 32
RET_DV = 64

LANES = 128
MLA_WIDTH = MLA_HEADS * MLA_V
GLA_WIDTH = GLA_HEADS * GLA_DV
RET_WIDTH = RET_HEADS * RET_DV
QK_WIDTH = GLA_HEADS * GLA_DK
HEAD_PAD = MLA_HEADS * LANES

C_MQ = 0
C_MKV = C_MQ + MLA_Q_RANK
C_KPE = C_MKV + MLA_KV_RANK
C_KPE_ROT = C_KPE + LANES
C_GQ = C_KPE_ROT + LANES
C_GK = C_GQ + QK_WIDTH
C_GV = C_GK + QK_WIDTH
C_GR = C_GV + GLA_WIDTH
C_RQ = C_GR + GLA_WIDTH
C_RQ_ROT = C_RQ + QK_WIDTH
C_RK = C_RQ_ROT + QK_WIDTH
C_RK_ROT = C_RK + QK_WIDTH
C_RV = C_RK_ROT + QK_WIDTH
C_RG = C_RV + RET_WIDTH
D_IN_EXT = C_RG + RET_WIDTH

LOG2_E = math.log2(math.e)
VMEM_LIMIT = 56 * 1024 * 1024


def _rms(x, g):
    return x * lax.rsqrt(jnp.mean(x * x, axis=-1, keepdims=True) + EPS) * g


def _silu(x):
    return x * jax.nn.sigmoid(x)


def _cparams(*sem):
    return pltpu.CompilerParams(dimension_semantics=sem, vmem_limit_bytes=VMEM_LIMIT)


def _const_spec(shape):
    nd = len(shape)
    return pl.BlockSpec(shape, lambda *_: (0,) * nd)


def _ffn_kernel(x_ref, pre_ref, post_ref, wg_ref, wu_ref, wd_ref, o_ref, h_ref, *, tf):
    x = x_ref[...]
    xn = _rms(x, pre_ref[...]).astype(BF16)
    for c in range(D_FF // tf):
        sl = slice(c * tf, (c + 1) * tf)
        g = jnp.dot(xn, wg_ref[:, sl], preferred_element_type=F32)
        u = jnp.dot(xn, wu_ref[:, sl], preferred_element_type=F32)
        h_ref[:, sl] = (_silu(g) * u).astype(BF16)
    f = jnp.dot(h_ref[...], wd_ref[...], preferred_element_type=F32)
    o_ref[...] = x + 0.5 * _rms(f, post_ref[...])


def _ffn(x, pre_g, post_g, wg, wu, wd, *, tm, tf):
    t = x.shape[0]
    return pl.pallas_call(
        functools.partial(_ffn_kernel, tf=tf),
        out_shape=jax.ShapeDtypeStruct(x.shape, F32),
        grid=(t // tm,),
        in_specs=[
            pl.BlockSpec((tm, D_MODEL), lambda i: (i, 0)),
            _const_spec((1, D_MODEL)),
            _const_spec((1, D_MODEL)),
            _const_spec((D_MODEL, D_FF)),
            _const_spec((D_MODEL, D_FF)),
            _const_spec((D_FF, D_MODEL)),
        ],
        out_specs=pl.BlockSpec((tm, D_MODEL), lambda i: (i, 0)),
        scratch_shapes=[pltpu.VMEM((tm, D_FF), BF16)],
        compiler_params=_cparams("parallel"),
        name="ffn",
    )(x, pre_g, post_g, wg, wu, wd)


def _rope_kernel(pos_ref, inv_ref, cos_ref, sin_ref):
    ang = pos_ref[...].astype(F32) * inv_ref[...]
    cos_ref[...] = jnp.cos(ang)
    sin_ref[...] = jnp.sin(ang)


def _rope_tables(pos_col, inv_lane, *, tm):
    t = pos_col.shape[0]
    return pl.pallas_call(
        _rope_kernel,
        out_shape=(jax.ShapeDtypeStruct((t, LANES), F32),) * 2,
        grid=(t // tm,),
        in_specs=[pl.BlockSpec((tm, 1), lambda i: (i, 0)), _const_spec((1, LANES))],
        out_specs=(pl.BlockSpec((tm, LANES), lambda i: (i, 0)),) * 2,
        compiler_params=_cparams("parallel"),
        name="rope_tables",
    )(pos_col, inv_lane)


def _proj_kernel(x_ref, cos_ref, sin_ref, pre_ref, win_ref, qg_ref, wq_ref, kvg_ref, wk_ref, wvt_ref,
                 wgate_ref, bgate_ref, tri_ref,
                 q_ref, k_ref, gq_ref, gk_ref, b_ref, gv_ref, sg_ref,
                 rq_ref, rk_ref, rv_ref, sr_ref, vt_ref):
    tm = x_ref.shape[0]
    xn = _rms(x_ref[...], pre_ref[...]).astype(BF16)

    chunk_starts = (0, C_KPE_ROT, C_RQ_ROT, D_IN_EXT)
    chunks = [jnp.dot(xn, win_ref[:, a:b], preferred_element_type=F32)
              for a, b in zip(chunk_starts[:-1], chunk_starts[1:])]

    def proj(c0, width):
        for a, b, chunk in zip(chunk_starts[:-1], chunk_starts[1:], chunks):
            if a <= c0 and c0 + width <= b:
                return chunk[:, c0 - a:c0 - a + width]
        raise ValueError("column group straddles two chunks")

    cosb = cos_ref[...]
    lane = lax.broadcasted_iota(jnp.int32, (tm, LANES), 1)
    sins = jnp.where((lane % ROPE_DIM) < HALF_ROPE, -sin_ref[...], sin_ref[...])

    hq = _rms(proj(C_MQ, MLA_Q_RANK), qg_ref[...]).astype(BF16)
    in_rope = (lane >= MLA_NOPE) & (lane < MLA_NOPE + ROPE_DIM)
    qcos = jnp.where(in_rope, cosb, 1.0)
    qsin = jnp.where(in_rope, sins, 0.0)
    q_scale = (MLA_NOPE + ROPE_DIM) ** -0.5 * LOG2_E
    qa = jnp.dot(hq, wq_ref[:, :HEAD_PAD], preferred_element_type=F32)
    qb = jnp.dot(hq, wq_ref[:, HEAD_PAD:], preferred_element_type=F32)
    for h in range(MLA_HEADS):
        sl = slice(h * LANES, (h + 1) * LANES)
        q_ref[:, sl] = ((qa[:, sl] * qcos + qb[:, sl] * qsin) * q_scale).astype(BF16)

    hkv = _rms(proj(C_MKV, MLA_KV_RANK), kvg_ref[...]).astype(BF16)
    g1 = proj(C_KPE, LANES)
    g2 = proj(C_KPE_ROT, LANES)
    kpe = jnp.where(in_rope, g1 * cosb + g2 * sins, 0.0)
    ones_row = jnp.where(lax.broadcasted_iota(jnp.int32, (LANES, tm), 0) == MLA_V, 1.0, 0.0)
    kn = jnp.dot(hkv, wk_ref[...], preferred_element_type=F32)
    vt = lax.dot_general(wvt_ref[...], hkv, (((1,), (1,)), ((), ())), preferred_element_type=F32)
    for h in range(MLA_HEADS):
        sl = slice(h * LANES, (h + 1) * LANES)
        k_ref[:, sl] = (kn[:, sl] + kpe).astype(BF16)
        vt_ref[sl, :] = (vt[sl, :] + ones_row).astype(BF16)

    z = jnp.dot(g1.astype(BF16), wgate_ref[...], preferred_element_type=F32) + bgate_ref[...]
    log_a = (jnp.minimum(z, 0.0) - jnp.log1p(jnp.exp(-jnp.abs(z)))) / GLA_TAU
    la_hi = log_a.astype(BF16)
    la_lo = (log_a - la_hi.astype(F32)).astype(BF16)
    la_split = jnp.concatenate([la_hi, la_lo], axis=1)
    tri = tri_ref[...]
    tb = tri.shape[0]
    for r in range(tm // tb):
        rs = slice(r * tb, (r + 1) * tb)
        cum = jnp.dot(tri, la_split[rs], preferred_element_type=F32)
        b_ref[rs, :] = cum[:, :QK_WIDTH] + cum[:, QK_WIDTH:]
    gq_ref[...] = (proj(C_GQ, QK_WIDTH) * (GLA_DK ** -0.5)).astype(BF16)
    gk_ref[...] = proj(C_GK, QK_WIDTH).astype(BF16)
    gv_ref[...] = proj(C_GV, GLA_WIDTH).astype(BF16)
    sg_ref[...] = _silu(proj(C_GR, GLA_WIDTH)).astype(BF16)

    rq_ref[...] = (proj(C_RQ, QK_WIDTH) * cosb + proj(C_RQ_ROT, QK_WIDTH) * sins).astype(BF16)
    rk_ref[...] = ((proj(C_RK, QK_WIDTH) * cosb + proj(C_RK_ROT, QK_WIDTH) * sins)
                   * (RET_DK ** -0.5)).astype(BF16)
    rv_ref[...] = proj(C_RV, RET_WIDTH).astype(BF16)
    sr_ref[...] = _silu(proj(C_RG, RET_WIDTH)).astype(BF16)


def _proj(x, cosb, sinb, lw, tri, *, tm):
    t = x.shape[0]
    row = lambda w: pl.BlockSpec((tm, w), lambda i: (i, 0))
    out_widths = [(HEAD_PAD, BF16), (HEAD_PAD, BF16),
                  (QK_WIDTH, BF16), (QK_WIDTH, BF16), (QK_WIDTH, F32), (GLA_WIDTH, BF16), (GLA_WIDTH, BF16),
                  (QK_WIDTH, BF16), (QK_WIDTH, BF16), (RET_WIDTH, BF16), (RET_WIDTH, BF16)]
    consts = [lw["mix_pre_g"], lw["w_in"], lw["q_norm_g"], lw["wq"], lw["kv_norm_g"], lw["wk"], lw["wvt"],
              lw["w_gate"], lw["b_gate"], tri]
    return pl.pallas_call(
        _proj_kernel,
        out_shape=tuple(jax.ShapeDtypeStruct((t, w), dt) for w, dt in out_widths)
        + (jax.ShapeDtypeStruct((HEAD_PAD, t), BF16),),
        grid=(t // tm,),
        in_specs=[row(D_MODEL), row(LANES), row(LANES)] + [_const_spec(c.shape) for c in consts],
        out_specs=tuple(row(w) for w, _ in out_widths) + (pl.BlockSpec((HEAD_PAD, tm), lambda i: (0, i)),),
        compiler_params=_cparams("parallel"),
        name="mixer_proj",
    )(x, cosb, sinb, *consts)


def _attn_kernel(q_ref, k_ref, vt_ref, o_ref, m_ref, acc_ref, st_ref, *, tk):
    tq = q_ref.shape[0]
    qi = pl.program_id(1)
    key_chunk = lax.broadcasted_iota(jnp.int32, (tk, tq), 0) // CHUNK
    qry_chunk = lax.broadcasted_iota(jnp.int32, (tk, tq), 1) // CHUNK
    diag_mask = key_chunk <= qry_chunk
    lane = lax.broadcasted_iota(jnp.int32, (tq, LANES), 1)
    nt = (((1,), (1,)), ((), ()))

    def block_rows(j):
        return pl.ds(pl.multiple_of(j * tk, tk), tk)

    def scores(j, h):
        hs = slice(h * LANES, (h + 1) * LANES)
        return lax.dot_general(k_ref[block_rows(j), hs], q_ref[:, hs], nt, preferred_element_type=F32)

    def consume(j, h, st, masked):
        hs = slice(h * LANES, (h + 1) * LANES)
        if masked:
            st = jnp.where(diag_mask, st, -jnp.inf)
        m_old = m_ref[h]
        m_new = jnp.maximum(m_old, jnp.max(st, axis=0, keepdims=True))
        p = jnp.exp2(st - m_new).astype(BF16)
        acc_ref[h] = (acc_ref[h] * jnp.exp2(m_old - m_new)
                      + jnp.dot(vt_ref[hs, block_rows(j)], p, preferred_element_type=F32))
        m_ref[h] = m_new

    for h in range(MLA_HEADS):
        m_ref[h] = jnp.full((1, tq), -jnp.inf, F32)
        acc_ref[h] = jnp.zeros((LANES, tq), F32)

    st_ref[...] = scores(0, 0)

    def body(j, carry):
        st = st_ref[...]
        for h in range(MLA_HEADS):
            nxt = scores(j, h + 1) if h + 1 < MLA_HEADS else scores(j + 1, 0)
            consume(j, h, st, False)
            st = nxt
        st_ref[...] = st
        return carry

    lax.fori_loop(0, qi, body, 0)
    st = st_ref[...]
    for h in range(MLA_HEADS):
        nxt = scores(qi, h + 1) if h + 1 < MLA_HEADS else None
        consume(qi, h, st, True)
        st = nxt

    def head_out(h):
        acc = acc_ref[h]
        return (acc / acc[MLA_V:MLA_V + 1, :]).T

    for p in range(MLA_HEADS // 2):
        odd = pltpu.roll(head_out(2 * p + 1), MLA_V, axis=1)
        o_ref[:, p * LANES:(p + 1) * LANES] = jnp.where(lane < MLA_V, head_out(2 * p), odd).astype(BF16)


def _attention(q, k, vt, *, batch, seq, tq, tk):
    nq = seq // tq
    assert tq == tk
    return pl.pallas_call(
        functools.partial(_attn_kernel, tk=tk),
        out_shape=jax.ShapeDtypeStruct((batch * seq, MLA_WIDTH), BF16),
        grid=(batch, nq),
        in_specs=[
            pl.BlockSpec((tq, HEAD_PAD), lambda b, i: (b * nq + i, 0)),
            pl.BlockSpec((seq, HEAD_PAD), lambda b, i: (b, 0)),
            pl.BlockSpec((HEAD_PAD, seq), lambda b, i: (0, b)),
        ],
        out_specs=pl.BlockSpec((tq, MLA_WIDTH), lambda b, i: (b * nq + i, 0)),
        scratch_shapes=[pltpu.VMEM((MLA_HEADS, 1, tq), F32), pltpu.VMEM((MLA_HEADS, LANES, tq), F32),
                        pltpu.VMEM((tk, tq), F32)],
        compiler_params=_cparams("parallel", "arbitrary"),
        name="mla_attention",
    )(q, k, vt)


def _head_rows(x, n_heads, head_lanes):
    lane_head = lax.broadcasted_iota(jnp.int32, x.shape, 1) // head_lanes
    return jnp.concatenate([jnp.where(lane_head == h, x, 0.0).astype(BF16) for h in range(n_heads)], axis=0)


def _linmix_kernel(gq_ref, gk_ref, b_ref, gv_ref, sg_ref, rq_ref, rk_ref, rv_ref, sr_ref,
                   gng_ref, rng_ref, lgam_ref, ones_ref, o_ref, sg_state, sr_state, og_ref, or_ref):
    tc = gq_ref.shape[0]
    L = CHUNK

    @pl.when(pl.program_id(1) == 0)
    def _():
        sg_state[...] = jnp.zeros(sg_state.shape, F32)
        sr_state[...] = jnp.zeros(sr_state.shape, F32)

    ri = lax.broadcasted_iota(jnp.int32, (L, GLA_HEADS * L), 0)
    cj = lax.broadcasted_iota(jnp.int32, (L, GLA_HEADS * L), 1) % L
    lower = cj <= ri
    lgam = lgam_ref[...]
    lgam_cols = jnp.concatenate(
        [jnp.broadcast_to(lgam[:, h * RET_DK:h * RET_DK + 1], (1, L)) for h in range(RET_HEADS)], axis=1)
    ret_decay = jnp.exp(lgam_cols * jnp.abs(ri - cj).astype(F32))
    pos = lax.broadcasted_iota(jnp.int32, (L, QK_WIDTH), 0).astype(F32)
    q_dec = jnp.exp(lgam * (pos + 1.0))
    k_dec = jnp.exp(lgam * (L - 1.0 - pos))
    chunk_dec = jnp.exp(lgam * float(L))
    state_head = lax.broadcasted_iota(jnp.int32, (GLA_WIDTH, QK_WIDTH), 0) // GLA_DV
    state_lane_head = lax.broadcasted_iota(jnp.int32, (GLA_WIDTH, QK_WIDTH), 1) // GLA_DK
    state_diag = state_head == state_lane_head
    nt = (((1,), (1,)), ((), ()))

    for c in range(tc // L):
        rs = slice(c * L, (c + 1) * L)
        b = b_ref[rs, :]
        b_last = b[L - 1:L, :]
        eb, enb = jnp.exp(b), jnp.exp(-b)
        q = gq_ref[rs, :].astype(F32)
        k = gk_ref[rs, :].astype(F32)
        v = gv_ref[rs, :].astype(F32)
        q_fwd = (q * eb).astype(BF16)
        a_low = lax.dot_general(q_fwd, _head_rows(k * enb, GLA_HEADS, GLA_DK), nt, preferred_element_type=F32)
        a_up = lax.dot_general((q * enb).astype(BF16), _head_rows(k * eb, GLA_HEADS, GLA_DK), nt,
                               preferred_element_type=F32)
        attn = jnp.where(lower, a_low, a_up).astype(BF16)
        o_intra = jnp.dot(attn, _head_rows(v, GLA_HEADS, GLA_DV), preferred_element_type=F32)
        s_t = sg_state[...]
        o_inter = lax.dot_general(q_fwd, s_t.astype(BF16), nt, preferred_element_type=F32)
        og_ref[rs, :] = o_intra + o_inter
        k_state = (k * jnp.exp(b_last - b)).astype(BF16)
        d_state = jnp.dot(v.T.astype(BF16), k_state, preferred_element_type=F32)
        sg_state[...] = s_t * jnp.exp(b_last) + jnp.where(state_diag, d_state, 0.0)

        q = rq_ref[rs, :].astype(F32)
        k = rk_ref[rs, :].astype(F32)
        v = rv_ref[rs, :].astype(F32)
        scores = lax.dot_general(q.astype(BF16), _head_rows(k, RET_HEADS, RET_DK), nt,
                                 preferred_element_type=F32) * ret_decay
        o_intra = jnp.dot(scores.astype(BF16), _head_rows(v, RET_HEADS, RET_DV), preferred_element_type=F32)
        s_t = sr_state[...]
        o_inter = lax.dot_general((q * q_dec).astype(BF16), s_t.astype(BF16), nt, preferred_element_type=F32)
        or_ref[rs, :] = o_intra + o_inter
        d_state = jnp.dot(v.T.astype(BF16), (k * k_dec).astype(BF16), preferred_element_type=F32)
        sr_state[...] = s_t * chunk_dec + jnp.where(state_diag, d_state, 0.0)

    def head_norm(o, g, gate):
        o2 = o * o
        hi = o2.astype(BF16)
        lo = (o2 - hi.astype(F32)).astype(BF16)
        ss = (jnp.dot(hi, ones_ref[...], preferred_element_type=F32)
              + jnp.dot(lo, ones_ref[...], preferred_element_type=F32))
        return o * lax.rsqrt(ss * (1.0 / GLA_DV) + EPS) * g * gate

    o_ref[:, :GLA_WIDTH] = head_norm(og_ref[...], gng_ref[...], sg_ref[...].astype(F32)).astype(BF16)
    o_ref[:, GLA_WIDTH:] = head_norm(or_ref[...], rng_ref[...], sr_ref[...].astype(F32)).astype(BF16)


def _linmix(gq, gk, b, gv, sg, rq, rk, rv, sr, gng, rng, lgam, ones_bd, *, batch, seq, tc):
    nt = seq // tc
    row = lambda w: pl.BlockSpec((tc, w), lambda bb, i: (bb * nt + i, 0))
    consts = [gng, rng, lgam, ones_bd]
    return pl.pallas_call(
        _linmix_kernel,
        out_shape=jax.ShapeDtypeStruct((batch * seq, GLA_WIDTH + RET_WIDTH), BF16),
        grid=(batch, nt),
        in_specs=[row(QK_WIDTH), row(QK_WIDTH), row(QK_WIDTH), row(GLA_WIDTH), row(GLA_WIDTH),
                  row(QK_WIDTH), row(QK_WIDTH), row(RET_WIDTH), row(RET_WIDTH)]
                 + [_const_spec(c.shape) for c in consts],
        out_specs=row(GLA_WIDTH + RET_WIDTH),
        scratch_shapes=[pltpu.VMEM((GLA_WIDTH, QK_WIDTH), F32), pltpu.VMEM((RET_WIDTH, QK_WIDTH), F32),
                        pltpu.VMEM((tc, GLA_WIDTH), F32), pltpu.VMEM((tc, RET_WIDTH), F32)],
        compiler_params=_cparams("parallel", "arbitrary"),
        name="gla_retention",
    )(gq, gk, b, gv, sg, rq, rk, rv, sr, *consts)


def _outproj_kernel(x_ref, om_ref, ol_ref, wa_ref, wb_ref, post_ref, o_ref):
    m = (jnp.dot(om_ref[...], wa_ref[...], preferred_element_type=F32)
         + jnp.dot(ol_ref[...], wb_ref[...], preferred_element_type=F32))
    o_ref[...] = x_ref[...] + _rms(m, post_ref[...])


def _outproj(x, o_mla, o_lin, wa, wb, post_g, *, tm):
    t = x.shape[0]
    row = lambda w: pl.BlockSpec((tm, w), lambda i: (i, 0))
    return pl.pallas_call(
        _outproj_kernel,
        out_shape=jax.ShapeDtypeStruct(x.shape, F32),
        grid=(t // tm,),
        in_specs=[row(D_MODEL), row(MLA_WIDTH), row(GLA_WIDTH + RET_WIDTH),
                  _const_spec(wa.shape), _const_spec(wb.shape), _const_spec((1, D_MODEL))],
        out_specs=row(D_MODEL),
        compiler_params=_cparams("parallel"),
        name="mixer_out",
    )(x, o_mla, o_lin, wa, wb, post_g)


def _rot_cols(w):
    return jnp.concatenate([w[..., HALF_ROPE:], w[..., :HALF_ROPE]], axis=-1)


def _prep_layer(l, p):
    w_in = p["w_in"][l]
    splits = (MLA_Q_RANK, MLA_KV_RANK, ROPE_DIM, QK_WIDTH, QK_WIDTH, GLA_WIDTH, GLA_GATE_RANK, GLA_WIDTH,
              QK_WIDTH, QK_WIDTH, RET_WIDTH, RET_WIDTH)
    offs = [0]
    for s in splits:
        offs.append(offs[-1] + s)
    mq, mkv, mkr, gq, gk, gv, gg, gr, rq, rk, rv, rg = (w_in[:, offs[i]:offs[i + 1]] for i in range(12))
    d = w_in.shape[0]
    zeros = lambda n: jnp.zeros((d, n), F32)
    kpe_grp = jnp.concatenate([gg, zeros(MLA_NOPE - GLA_GATE_RANK), mkr, zeros(LANES - MLA_NOPE - ROPE_DIM)], 1)
    kpe_rot_grp = jnp.concatenate([zeros(MLA_NOPE), _rot_cols(mkr), zeros(LANES - MLA_NOPE - ROPE_DIM)], 1)
    per_head_rot = lambda w, nh: _rot_cols(w.reshape(d, nh, ROPE_DIM)).reshape(d, nh * ROPE_DIM)
    w_in_ext = jnp.concatenate(
        [mq, mkv, kpe_grp, kpe_rot_grp, gq, gk, gv, gr,
         rq, per_head_rot(rq, RET_HEADS), rk, per_head_rot(rk, RET_HEADS), rv, rg], axis=1).astype(BF16)

    wq = p["mla_w_q_up"][l].reshape(MLA_Q_RANK, MLA_HEADS, MLA_NOPE + ROPE_DIM)
    pad_q = jnp.zeros((MLA_Q_RANK, MLA_HEADS, LANES - MLA_NOPE - ROPE_DIM), F32)
    wq_main = jnp.concatenate([wq, pad_q], axis=-1).reshape(MLA_Q_RANK, HEAD_PAD)
    wq_rot = jnp.concatenate([jnp.zeros((MLA_Q_RANK, MLA_HEADS, MLA_NOPE), F32),
                              _rot_cols(wq[..., MLA_NOPE:]), pad_q], axis=-1).reshape(MLA_Q_RANK, HEAD_PAD)
    wkv = p["mla_w_kv_up"][l].reshape(MLA_KV_RANK, MLA_HEADS, MLA_NOPE + MLA_V)
    pad_kv = jnp.zeros((MLA_KV_RANK, MLA_HEADS, LANES - MLA_NOPE), F32)
    wk_pad = jnp.concatenate([wkv[..., :MLA_NOPE], pad_kv], axis=-1).reshape(MLA_KV_RANK, HEAD_PAD)
    wv_pad = jnp.concatenate([wkv[..., MLA_NOPE:], pad_kv], axis=-1).reshape(MLA_KV_RANK, HEAD_PAD)
    w_gate = jnp.concatenate([p["gla_w_gate_up"][l],
                              jnp.zeros((LANES - GLA_GATE_RANK, QK_WIDTH), F32)], axis=0)
    row = lambda a: a.reshape(1, -1)
    w_out = p["w_out"][l].astype(BF16)
    return {
        "mix_pre_g": row(p["mix_pre_g"][l]), "mix_post_g": row(p["mix_post_g"][l]),
        "w_in": w_in_ext,
        "q_norm_g": row(p["mla_q_norm_g"][l]),
        "wq": jnp.concatenate([wq_main, wq_rot], axis=1).astype(BF16),
        "kv_norm_g": row(p["mla_kv_norm_g"][l]),
        "wk": wk_pad.astype(BF16), "wvt": wv_pad.T.astype(BF16),
        "w_gate": w_gate.astype(BF16), "b_gate": row(p["gla_b_gate"][l]),
        "gla_norm_g": row(jnp.tile(p["gla_norm_g"][l], GLA_HEADS)),
        "ret_norm_g": row(jnp.tile(p["ret_norm_g"][l], RET_HEADS)),
        "w_out_mla": w_out[:MLA_WIDTH], "w_out_lin": w_out[MLA_WIDTH:],
    }


def _ffn_params(l, p, prefix):
    return (p[prefix + "_pre_g"][l].reshape(1, -1), p[prefix + "_post_g"][l].reshape(1, -1),
            p[prefix + "_w_gate"][l].astype(BF16), p[prefix + "_w_up"][l].astype(BF16),
            p[prefix + "_w_down"][l].astype(BF16))


def _tile(n, pref):
    while n % pref:
        pref //= 2
    return pref


def kernel(x, positions, ffn1_pre_g, ffn1_post_g, ffn1_w_gate, ffn1_w_up, ffn1_w_down, mix_pre_g, mix_post_g, w_in, mla_q_norm_g, mla_w_q_up, mla_kv_norm_g, mla_w_kv_up, gla_w_gate_up, gla_b_gate, gla_norm_g, ret_norm_g, w_out, ffn2_pre_g, ffn2_post_g, ffn2_w_gate, ffn2_w_up, ffn2_w_down):
    p = dict(ffn1_pre_g=ffn1_pre_g, ffn1_post_g=ffn1_post_g, ffn1_w_gate=ffn1_w_gate, ffn1_w_up=ffn1_w_up,
             ffn1_w_down=ffn1_w_down, mix_pre_g=mix_pre_g, mix_post_g=mix_post_g, w_in=w_in,
             mla_q_norm_g=mla_q_norm_g, mla_w_q_up=mla_w_q_up, mla_kv_norm_g=mla_kv_norm_g,
             mla_w_kv_up=mla_w_kv_up, gla_w_gate_up=gla_w_gate_up, gla_b_gate=gla_b_gate,
             gla_norm_g=gla_norm_g, ret_norm_g=ret_norm_g, w_out=w_out, ffn2_pre_g=ffn2_pre_g,
             ffn2_post_g=ffn2_post_g, ffn2_w_gate=ffn2_w_gate, ffn2_w_up=ffn2_w_up, ffn2_w_down=ffn2_w_down)
    batch, seq, d = x.shape
    t = batch * seq
    depth = w_in.shape[0]
    tm = _tile(seq, 512)
    tq = _tile(seq, 512)
    tc = _tile(seq, 512)

    inv = ROPE_BASE ** (-jnp.arange(0, ROPE_DIM, 2, dtype=F32) / ROPE_DIM)
    inv_lane = jnp.tile(inv, LANES // HALF_ROPE).reshape(1, LANES)
    tb = _tile(tm, 256)
    idx = jnp.arange(tb)
    tri = ((idx[:, None] >= idx[None, :]) & (idx[:, None] // CHUNK == idx[None, :] // CHUNK)).astype(BF16)
    hd = jnp.arange(GLA_WIDTH) // GLA_DV
    ones_bd = (hd[:, None] == hd[None, :]).astype(BF16)
    log_gamma = jnp.log1p(-jnp.exp2(-5.0 - jnp.arange(RET_HEADS, dtype=F32)))
    lgam = jnp.repeat(log_gamma, RET_DK).reshape(1, QK_WIDTH)

    cosb, sinb = _rope_tables(positions.reshape(t, 1), inv_lane, tm=tm)

    xf = x.reshape(t, d)
    for l in range(depth):
        xf = _ffn(xf, *_ffn_params(l, p, "ffn1"), tm=tm, tf=256)
        lw = _prep_layer(l, p)
        (q, k, gq, gk, b, gv, sg, rq, rk, rv, sr, vt) = _proj(xf, cosb, sinb, lw, tri, tm=tm)
        o_mla = _attention(q, k, vt, batch=batch, seq=seq, tq=tq, tk=tq)
        o_lin = _linmix(gq, gk, b, gv, sg, rq, rk, rv, sr, lw["gla_norm_g"], lw["ret_norm_g"], lgam, ones_bd,
                        batch=batch, seq=seq, tc=tc)
        xf = _outproj(xf, o_mla, o_lin, lw["w_out_mla"], lw["w_out_lin"], lw["mix_post_g"], tm=tm)
        xf = _ffn(xf, *_ffn_params(l, p, "ffn2"), tm=tm, tf=256)
    return xf.reshape(batch, seq, d)
```

```python
import functools
import math

import jax
import jax.numpy as jnp
from jax import lax
from jax.experimental import pallas as pl
from jax.experimental.pallas import tpu as pltpu

F32 = jnp.float32
BF16 = jnp.bfloat16

D_MODEL = 1024
D_FF = 2816
CHUNK = 64
EPS = 1e-6
ROPE_BASE = 10000.0
ROPE_DIM = 32
HALF_ROPE = ROPE_DIM // 2

MLA_HEADS = 8
MLA_Q_RANK = 256
MLA_KV_RANK = 128
MLA_NOPE = 64
MLA_V = 64
GLA_HEADS = 4
GLA_DK = 32
GLA_DV = 64
GLA_GATE_RANK = 16
GLA_TAU = 16.0
RET_HEADS = 4
RET_DK = 32
RET_DV = 64

LANES = 128
MLA_WIDTH = MLA_HEADS * MLA_V
GLA_WIDTH = GLA_HEADS * GLA_DV
RET_WIDTH = RET_HEADS * RET_DV
QK_WIDTH = GLA_HEADS * GLA_DK
HEAD_PAD = MLA_HEADS * LANES

C_MQ = 0
C_MKV = C_MQ + MLA_Q_RANK
C_KPE = C_MKV + MLA_KV_RANK
C_KPE_ROT = C_KPE + LANES
C_GQ = C_KPE_ROT + LANES
C_GK = C_GQ + QK_WIDTH
C_GV = C_GK + QK_WIDTH
C_GR = C_GV + GLA_WIDTH
C_RQ = C_GR + GLA_WIDTH
C_RQ_ROT = C_RQ + QK_WIDTH
C_RK = C_RQ_ROT + QK_WIDTH
C_RK_ROT = C_RK + QK_WIDTH
C_RV = C_RK_ROT + QK_WIDTH
C_RG = C_RV + RET_WIDTH
D_IN_EXT = C_RG + RET_WIDTH

LOG2_E = math.log2(math.e)
LOOKAHEAD = 2
VMEM_LIMIT = 56 * 1024 * 1024


def _rms(x, g):
    return x * lax.rsqrt(jnp.mean(x * x, axis=-1, keepdims=True) + EPS) * g


def _silu(x):
    return x * jax.nn.sigmoid(x)


def _cparams(*sem):
    return pltpu.CompilerParams(dimension_semantics=sem, vmem_limit_bytes=VMEM_LIMIT)


def _const_spec(shape):
    nd = len(shape)
    return pl.BlockSpec(shape, lambda *_: (0,) * nd, pipeline_mode=pl.Buffered(1))


def _layer_spec(stacked, layer):
    nd = stacked.ndim
    return pl.BlockSpec((None,) + stacked.shape[1:], lambda *_: (layer,) + (0,) * (nd - 1),
                        pipeline_mode=pl.Buffered(1))


def _ffn_kernel(*refs, tf, with_mixer_out):
    if with_mixer_out:
        x_ref, om_ref, ol_ref, wa_ref, wb_ref, mpost_ref = refs[:6]
        refs = refs[6:]
        m = (jnp.dot(om_ref[...], wa_ref[...], preferred_element_type=F32)
             + jnp.dot(ol_ref[...], wb_ref[...], preferred_element_type=F32))
        x = x_ref[...] + _rms(m, mpost_ref[...])
    else:
        x_ref = refs[0]
        refs = refs[1:]
        x = x_ref[...]
    pre_ref, post_ref, wg_ref, wu_ref, wd_ref, o_ref, h_ref = refs
    xn = _rms(x, pre_ref[...]).astype(BF16)
    for c in range(D_FF // tf):
        sl = slice(c * tf, (c + 1) * tf)
        g = jnp.dot(xn, wg_ref[:, sl], preferred_element_type=F32)
        u = jnp.dot(xn, wu_ref[:, sl], preferred_element_type=F32)
        h_ref[:, sl] = (_silu(g) * u).astype(BF16)
    f = jnp.dot(h_ref[...], wd_ref[...], preferred_element_type=F32)
    o_ref[...] = x + 0.5 * _rms(f, post_ref[...])


def _ffn(x, layer, fw, *, tm, tf, mixer_out=None):
    t = x.shape[0]
    row = lambda w: pl.BlockSpec((tm, w), lambda i: (i, 0))
    args, specs = [x], [row(D_MODEL)]
    if mixer_out is not None:
        o_mla, o_lin, wa, wb, mpost = mixer_out
        args += [o_mla, o_lin, wa, wb, mpost]
        specs += [row(MLA_WIDTH), row(GLA_WIDTH + RET_WIDTH)] + [_layer_spec(a, layer) for a in (wa, wb, mpost)]
    args += list(fw)
    specs += [_layer_spec(a, layer) for a in fw]
    return pl.pallas_call(
        functools.partial(_ffn_kernel, tf=tf, with_mixer_out=mixer_out is not None),
        out_shape=jax.ShapeDtypeStruct(x.shape, F32),
        grid=(t // tm,),
        in_specs=specs,
        out_specs=row(D_MODEL),
        scratch_shapes=[pltpu.VMEM((tm, D_FF), BF16)],
        compiler_params=_cparams("parallel"),
        name="ffn_mixer_out" if mixer_out is not None else "ffn",
    )(*args)


def _rope_kernel(pos_ref, inv_ref, cos_ref, sin_ref):
    ang = pos_ref[...].astype(F32) * inv_ref[...]
    cos_ref[...] = jnp.cos(ang)
    sin_ref[...] = jnp.sin(ang)


def _rope_tables(pos_col, inv_lane, *, tm):
    t = pos_col.shape[0]
    return pl.pallas_call(
        _rope_kernel,
        out_shape=(jax.ShapeDtypeStruct((t, LANES), F32),) * 2,
        grid=(t // tm,),
        in_specs=[pl.BlockSpec((tm, 1), lambda i: (i, 0)), _const_spec((1, LANES))],
        out_specs=(pl.BlockSpec((tm, LANES), lambda i: (i, 0)),) * 2,
        compiler_params=_cparams("parallel"),
        name="rope_tables",
    )(pos_col, inv_lane)


def _proj_kernel(x_ref, cos_ref, sin_ref, pre_ref, win_ref, qg_ref, wq_ref, kvg_ref, wk_ref, wvt_ref,
                 wgate_ref, bgate_ref, tri_ref,
                 q_ref, k_ref, gq_ref, gk_ref, b_ref, gv_ref, sg_ref,
                 rq_ref, rk_ref, rv_ref, sr_ref, vt_ref):
    tm = x_ref.shape[0]
    xn = _rms(x_ref[...], pre_ref[...]).astype(BF16)

    chunk_starts = (0, C_KPE_ROT, C_RQ_ROT, D_IN_EXT)
    chunks = [jnp.dot(xn, win_ref[:, a:b], preferred_element_type=F32)
              for a, b in zip(chunk_starts[:-1], chunk_starts[1:])]

    def proj(c0, width):
        for a, b, chunk in zip(chunk_starts[:-1], chunk_starts[1:], chunks):
            if a <= c0 and c0 + width <= b:
                return chunk[:, c0 - a:c0 - a + width]
        raise ValueError("column group straddles two chunks")

    cosb = cos_ref[...]
    lane = lax.broadcasted_iota(jnp.int32, (tm, LANES), 1)
    sins = jnp.where((lane % ROPE_DIM) < HALF_ROPE, -sin_ref[...], sin_ref[...])

    hq = _rms(proj(C_MQ, MLA_Q_RANK), qg_ref[...]).astype(BF16)
    in_rope = (lane >= MLA_NOPE) & (lane < MLA_NOPE + ROPE_DIM)
    qcos = jnp.where(in_rope, cosb, 1.0)
    qsin = jnp.where(in_rope, sins, 0.0)
    q_scale = (MLA_NOPE + ROPE_DIM) ** -0.5 * LOG2_E
    qa = jnp.dot(hq, wq_ref[:, :HEAD_PAD], preferred_element_type=F32)
    qb = jnp.dot(hq, wq_ref[:, HEAD_PAD:], preferred_element_type=F32)
    for h in range(MLA_HEADS):
        sl = slice(h * LANES, (h + 1) * LANES)
        q_ref[:, sl] = ((qa[:, sl] * qcos + qb[:, sl] * qsin) * q_scale).astype(BF16)

    hkv = _rms(proj(C_MKV, MLA_KV_RANK), kvg_ref[...]).astype(BF16)
    g1 = proj(C_KPE, LANES)
    g2 = proj(C_KPE_ROT, LANES)
    kpe = jnp.where(in_rope, g1 * cosb + g2 * sins, 0.0)
    ones_row = jnp.where(lax.broadcasted_iota(jnp.int32, (LANES, tm), 0) == MLA_V, 1.0, 0.0)
    kn = jnp.dot(hkv, wk_ref[...], preferred_element_type=F32)
    vt = lax.dot_general(wvt_ref[...], hkv, (((1,), (1,)), ((), ())), preferred_element_type=F32)
    for h in range(MLA_HEADS):
        sl = slice(h * LANES, (h + 1) * LANES)
        k_ref[:, sl] = (kn[:, sl] + kpe).astype(BF16)
        vt_ref[sl, :] = (vt[sl, :] + ones_row).astype(BF16)

    z = jnp.dot(g1.astype(BF16), wgate_ref[...], preferred_element_type=F32) + bgate_ref[...]
    log_a = (jnp.minimum(z, 0.0) - jnp.log1p(jnp.exp(-jnp.abs(z)))) / GLA_TAU
    la_hi = log_a.astype(BF16)
    la_lo = (log_a - la_hi.astype(F32)).astype(BF16)
    la_split = jnp.concatenate([la_hi, la_lo], axis=1)
    tri = tri_ref[...]
    tb = tri.shape[0]
    for r in range(tm // tb):
        rs = slice(r * tb, (r + 1) * tb)
        cum = jnp.dot(tri, la_split[rs], preferred_element_type=F32)
        b_ref[rs, :] = cum[:, :QK_WIDTH] + cum[:, QK_WIDTH:]
    gq_ref[...] = (proj(C_GQ, QK_WIDTH) * (GLA_DK ** -0.5)).astype(BF16)
    gk_ref[...] = proj(C_GK, QK_WIDTH).astype(BF16)
    gv_ref[...] = proj(C_GV, GLA_WIDTH).astype(BF16)
    sg_ref[...] = _silu(proj(C_GR, GLA_WIDTH)).astype(BF16)

    rq_ref[...] = (proj(C_RQ, QK_WIDTH) * cosb + proj(C_RQ_ROT, QK_WIDTH) * sins).astype(BF16)
    rk_ref[...] = ((proj(C_RK, QK_WIDTH) * cosb + proj(C_RK_ROT, QK_WIDTH) * sins)
                   * (RET_DK ** -0.5)).astype(BF16)
    rv_ref[...] = proj(C_RV, RET_WIDTH).astype(BF16)
    sr_ref[...] = _silu(proj(C_RG, RET_WIDTH)).astype(BF16)


def _proj(x, cosb, sinb, layer, mw, tri, *, tm):
    t = x.shape[0]
    row = lambda w: pl.BlockSpec((tm, w), lambda i: (i, 0))
    out_widths = [(HEAD_PAD, BF16), (HEAD_PAD, BF16),
                  (QK_WIDTH, BF16), (QK_WIDTH, BF16), (QK_WIDTH, F32), (GLA_WIDTH, BF16), (GLA_WIDTH, BF16),
                  (QK_WIDTH, BF16), (QK_WIDTH, BF16), (RET_WIDTH, BF16), (RET_WIDTH, BF16)]
    stacked = [mw["mix_pre_g"], mw["w_in"], mw["q_norm_g"], mw["wq"], mw["kv_norm_g"], mw["wk"], mw["wvt"],
               mw["w_gate"], mw["b_gate"]]
    return pl.pallas_call(
        _proj_kernel,
        out_shape=tuple(jax.ShapeDtypeStruct((t, w), dt) for w, dt in out_widths)
        + (jax.ShapeDtypeStruct((HEAD_PAD, t), BF16),),
        grid=(t // tm,),
        in_specs=[row(D_MODEL), row(LANES), row(LANES)] + [_layer_spec(a, layer) for a in stacked]
        + [_const_spec(tri.shape)],
        out_specs=tuple(row(w) for w, _ in out_widths) + (pl.BlockSpec((HEAD_PAD, tm), lambda i: (0, i)),),
        compiler_params=_cparams("parallel"),
        name="mixer_proj",
    )(x, cosb, sinb, *stacked, tri)


def _attn_kernel(q_ref, k_ref, vt_ref, o_ref, m_ref, acc_ref, st_ref, *, tk, tn):
    tq = q_ref.shape[0]
    qi = pl.program_id(1)
    lane = lax.broadcasted_iota(jnp.int32, (tq, LANES), 1)
    nt = (((1,), (1,)), ((), ()))
    items = [(h, c) for h in range(MLA_HEADS) for c in range(tq // tn)]

    def block_rows(j):
        return pl.ds(pl.multiple_of(j * tk, tk), tk)

    def scores(j, item):
        h, c = item
        hs = slice(h * LANES, (h + 1) * LANES)
        return lax.dot_general(k_ref[block_rows(j), hs], q_ref[c * tn:(c + 1) * tn, hs], nt,
                               preferred_element_type=F32)

    def consume(j, item, st, masked):
        h, c = item
        hs = slice(h * LANES, (h + 1) * LANES)
        cs = slice(c * tn, (c + 1) * tn)
        if masked:
            key_chunk = lax.broadcasted_iota(jnp.int32, (tk, tn), 0) // CHUNK
            qry_chunk = (lax.broadcasted_iota(jnp.int32, (tk, tn), 1) + c * tn) // CHUNK
            st = jnp.where(key_chunk <= qry_chunk, st, -jnp.inf)
        m_old = m_ref[h, :, cs]
        m_new = jnp.maximum(m_old, jnp.max(st, axis=0, keepdims=True))
        p = jnp.exp2((st - m_new).astype(BF16))
        acc_ref[h, :, cs] = (acc_ref[h, :, cs] * jnp.exp2(m_old - m_new)
                             + jnp.dot(vt_ref[hs, block_rows(j)], p, preferred_element_type=F32))
        m_ref[h, :, cs] = m_new

    for h in range(MLA_HEADS):
        m_ref[h] = jnp.full((1, tq), -jnp.inf, F32)
        acc_ref[h] = jnp.zeros((LANES, tq), F32)

    for a in range(LOOKAHEAD):
        st_ref[a] = scores(0, items[a])

    def sweep(j, masked, last):
        pending = [st_ref[a] for a in range(LOOKAHEAD)]
        for i, item in enumerate(items):
            ahead = i + LOOKAHEAD
            if ahead < len(items):
                pending.append(scores(j, items[ahead]))
            elif not last:
                pending.append(scores(j + 1, items[ahead - len(items)]))
            consume(j, item, pending.pop(0), masked)
        for a, st in enumerate(pending):
            st_ref[a] = st

    def body(j, carry):
        sweep(j, False, False)
        return carry

    lax.fori_loop(0, qi, body, 0)
    sweep(qi, True, True)

    def head_out(h):
        acc = acc_ref[h]
        return (acc / acc[MLA_V:MLA_V + 1, :]).T

    for p in range(MLA_HEADS // 2):
        odd = pltpu.roll(head_out(2 * p + 1), MLA_V, axis=1)
        o_ref[:, p * LANES:(p + 1) * LANES] = jnp.where(lane < MLA_V, head_out(2 * p), odd).astype(BF16)


def _attention(q, k, vt, *, batch, seq, tq, tk, tn):
    nq = seq // tq
    assert tq == tk and tq % tn == 0
    return pl.pallas_call(
        functools.partial(_attn_kernel, tk=tk, tn=tn),
        out_shape=jax.ShapeDtypeStruct((batch * seq, MLA_WIDTH), BF16),
        grid=(batch, nq),
        in_specs=[
            pl.BlockSpec((tq, HEAD_PAD), lambda b, i: (b * nq + i, 0)),
            pl.BlockSpec((seq, HEAD_PAD), lambda b, i: (b, 0)),
            pl.BlockSpec((HEAD_PAD, seq), lambda b, i: (0, b)),
        ],
        out_specs=pl.BlockSpec((tq, MLA_WIDTH), lambda b, i: (b * nq + i, 0)),
        scratch_shapes=[pltpu.VMEM((MLA_HEADS, 1, tq), F32), pltpu.VMEM((MLA_HEADS, LANES, tq), F32),
                        pltpu.VMEM((LOOKAHEAD, tk, tn), F32)],
        compiler_params=_cparams("parallel", "arbitrary"),
        name="mla_attention",
    )(q, k, vt)


def _head_rows(x, n_heads, head_lanes):
    lane_head = lax.broadcasted_iota(jnp.int32, x.shape, 1) // head_lanes
    return jnp.concatenate([jnp.where(lane_head == h, x, 0.0).astype(BF16) for h in range(n_heads)], axis=0)


def _linmix_kernel(gq_ref, gk_ref, b_ref, gv_ref, sg_ref, rq_ref, rk_ref, rv_ref, sr_ref,
                   gng_ref, rng_ref, lgam_ref, ones_ref, o_ref, sg_state, sr_state, og_ref, or_ref):
    tc = gq_ref.shape[0]
    L = CHUNK

    @pl.when(pl.program_id(1) == 0)
    def _():
        sg_state[...] = jnp.zeros(sg_state.shape, F32)
        sr_state[...] = jnp.zeros(sr_state.shape, F32)

    ri = lax.broadcasted_iota(jnp.int32, (L, GLA_HEADS * L), 0)
    cj = lax.broadcasted_iota(jnp.int32, (L, GLA_HEADS * L), 1) % L
    lower = cj <= ri
    lgam = lgam_ref[...]
    lgam_cols = jnp.concatenate(
        [jnp.broadcast_to(lgam[:, h * RET_DK:h * RET_DK + 1], (1, L)) for h in range(RET_HEADS)], axis=1)
    ret_decay = jnp.exp(lgam_cols * jnp.abs(ri - cj).astype(F32))
    pos = lax.broadcasted_iota(jnp.int32, (L, QK_WIDTH), 0).astype(F32)
    q_dec = jnp.exp(lgam * (pos + 1.0))
    k_dec = jnp.exp(lgam * (L - 1.0 - pos))
    chunk_dec = jnp.exp(lgam * float(L))
    state_head = lax.broadcasted_iota(jnp.int32, (GLA_WIDTH, QK_WIDTH), 0) // GLA_DV
    state_lane_head = lax.broadcasted_iota(jnp.int32, (GLA_WIDTH, QK_WIDTH), 1) // GLA_DK
    state_diag = state_head == state_lane_head
    nt = (((1,), (1,)), ((), ()))

    for c in range(tc // L):
        rs = slice(c * L, (c + 1) * L)
        b = b_ref[rs, :]
        b_last = b[L - 1:L, :]
        eb, enb = jnp.exp(b), jnp.exp(-b)
        q = gq_ref[rs, :].astype(F32)
        k = gk_ref[rs, :].astype(F32)
        v = gv_ref[rs, :].astype(F32)
        q_fwd = (q * eb).astype(BF16)
        a_low = lax.dot_general(q_fwd, _head_rows(k * enb, GLA_HEADS, GLA_DK), nt, preferred_element_type=F32)
        a_up = lax.dot_general((q * enb).astype(BF16), _head_rows(k * eb, GLA_HEADS, GLA_DK), nt,
                               preferred_element_type=F32)
        attn = jnp.where(lower, a_low, a_up).astype(BF16)
        o_intra = jnp.dot(attn, _head_rows(v, GLA_HEADS, GLA_DV), preferred_element_type=F32)
        s_t = sg_state[...]
        o_inter = lax.dot_general(q_fwd, s_t.astype(BF16), nt, preferred_element_type=F32)
        og_ref[rs, :] = o_intra + o_inter
        k_state = (k * jnp.exp(b_last - b)).astype(BF16)
        d_state = jnp.dot(v.T.astype(BF16), k_state, preferred_element_type=F32)
        sg_state[...] = s_t * jnp.exp(b_last) + jnp.where(state_diag, d_state, 0.0)

        q = rq_ref[rs, :].astype(F32)
        k = rk_ref[rs, :].astype(F32)
        v = rv_ref[rs, :].astype(F32)
        scores = lax.dot_general(q.astype(BF16), _head_rows(k, RET_HEADS, RET_DK), nt,
                                 preferred_element_type=F32) * ret_decay
        o_intra = jnp.dot(scores.astype(BF16), _head_rows(v, RET_HEADS, RET_DV), preferred_element_type=F32)
        s_t = sr_state[...]
        o_inter = lax.dot_general((q * q_dec).astype(BF16), s_t.astype(BF16), nt, preferred_element_type=F32)
        or_ref[rs, :] = o_intra + o_inter
        d_state = jnp.dot(v.T.astype(BF16), (k * k_dec).astype(BF16), preferred_element_type=F32)
        sr_state[...] = s_t * chunk_dec + jnp.where(state_diag, d_state, 0.0)

    def head_norm(o, g, gate):
        o2 = o * o
        hi = o2.astype(BF16)
        lo = (o2 - hi.astype(F32)).astype(BF16)
        ss = (jnp.dot(hi, ones_ref[...], preferred_element_type=F32)
              + jnp.dot(lo, ones_ref[...], preferred_element_type=F32))
        return o * lax.rsqrt(ss * (1.0 / GLA_DV) + EPS) * g * gate

    o_ref[:, :GLA_WIDTH] = head_norm(og_ref[...], gng_ref[...], sg_ref[...].astype(F32)).astype(BF16)
    o_ref[:, GLA_WIDTH:] = head_norm(or_ref[...], rng_ref[...], sr_ref[...].astype(F32)).astype(BF16)


def _linmix(gq, gk, b, gv, sg, rq, rk, rv, sr, layer, gng, rng, lgam, ones_bd, *, batch, seq, tc):
    nt = seq // tc
    row = lambda w: pl.BlockSpec((tc, w), lambda bb, i: (bb * nt + i, 0))
    return pl.pallas_call(
        _linmix_kernel,
        out_shape=jax.ShapeDtypeStruct((batch * seq, GLA_WIDTH + RET_WIDTH), BF16),
        grid=(batch, nt),
        in_specs=[row(QK_WIDTH), row(QK_WIDTH), row(QK_WIDTH), row(GLA_WIDTH), row(GLA_WIDTH),
                  row(QK_WIDTH), row(QK_WIDTH), row(RET_WIDTH), row(RET_WIDTH),
                  _layer_spec(gng, layer), _layer_spec(rng, layer), _const_spec(lgam.shape),
                  _const_spec(ones_bd.shape)],
        out_specs=row(GLA_WIDTH + RET_WIDTH),
        scratch_shapes=[pltpu.VMEM((GLA_WIDTH, QK_WIDTH), F32), pltpu.VMEM((RET_WIDTH, QK_WIDTH), F32),
                        pltpu.VMEM((tc, GLA_WIDTH), F32), pltpu.VMEM((tc, RET_WIDTH), F32)],
        compiler_params=_cparams("parallel", "arbitrary"),
        name="gla_retention",
    )(gq, gk, b, gv, sg, rq, rk, rv, sr, gng, rng, lgam, ones_bd)


def _rot_cols(w):
    return jnp.concatenate([w[..., HALF_ROPE:], w[..., :HALF_ROPE]], axis=-1)


def _prep_mixer_layer(p):
    w_in = p["w_in"]
    splits = (MLA_Q_RANK, MLA_KV_RANK, ROPE_DIM, QK_WIDTH, QK_WIDTH, GLA_WIDTH, GLA_GATE_RANK, GLA_WIDTH,
              QK_WIDTH, QK_WIDTH, RET_WIDTH, RET_WIDTH)
    offs = [0]
    for s in splits:
        offs.append(offs[-1] + s)
    mq, mkv, mkr, gq, gk, gv, gg, gr, rq, rk, rv, rg = (w_in[:, offs[i]:offs[i + 1]] for i in range(12))
    d = w_in.shape[0]
    zeros = lambda n: jnp.zeros((d, n), F32)
    kpe_grp = jnp.concatenate([gg, zeros(MLA_NOPE - GLA_GATE_RANK), mkr, zeros(LANES - MLA_NOPE - ROPE_DIM)], 1)
    kpe_rot_grp = jnp.concatenate([zeros(MLA_NOPE), _rot_cols(mkr), zeros(LANES - MLA_NOPE - ROPE_DIM)], 1)
    per_head_rot = lambda w, nh: _rot_cols(w.reshape(d, nh, ROPE_DIM)).reshape(d, nh * ROPE_DIM)
    w_in_ext = jnp.concatenate(
        [mq, mkv, kpe_grp, kpe_rot_grp, gq, gk, gv, gr,
         rq, per_head_rot(rq, RET_HEADS), rk, per_head_rot(rk, RET_HEADS), rv, rg], axis=1).astype(BF16)

    wq = p["mla_w_q_up"].reshape(MLA_Q_RANK, MLA_HEADS, MLA_NOPE + ROPE_DIM)
    pad_q = jnp.zeros((MLA_Q_RANK, MLA_HEADS, LANES - MLA_NOPE - ROPE_DIM), F32)
    wq_main = jnp.concatenate([wq, pad_q], axis=-1).reshape(MLA_Q_RANK, HEAD_PAD)
    wq_rot = jnp.concatenate([jnp.zeros((MLA_Q_RANK, MLA_HEADS, MLA_NOPE), F32),
                              _rot_cols(wq[..., MLA_NOPE:]), pad_q], axis=-1).reshape(MLA_Q_RANK, HEAD_PAD)
    wkv = p["mla_w_kv_up"].reshape(MLA_KV_RANK, MLA_HEADS, MLA_NOPE + MLA_V)
    pad_kv = jnp.zeros((MLA_KV_RANK, MLA_HEADS, LANES - MLA_NOPE), F32)
    wk_pad = jnp.concatenate([wkv[..., :MLA_NOPE], pad_kv], axis=-1).reshape(MLA_KV_RANK, HEAD_PAD)
    wv_pad = jnp.concatenate([wkv[..., MLA_NOPE:], pad_kv], axis=-1).reshape(MLA_KV_RANK, HEAD_PAD)
    w_gate = jnp.concatenate([p["gla_w_gate_up"], jnp.zeros((LANES - GLA_GATE_RANK, QK_WIDTH), F32)], axis=0)
    row = lambda a: a.reshape(1, -1)
    w_out = p["w_out"].astype(BF16)
    return {
        "mix_pre_g": row(p["mix_pre_g"]), "mix_post_g": row(p["mix_post_g"]),
        "w_in": w_in_ext,
        "q_norm_g": row(p["mla_q_norm_g"]),
        "wq": jnp.concatenate([wq_main, wq_rot], axis=1).astype(BF16),
        "kv_norm_g": row(p["mla_kv_norm_g"]),
        "wk": wk_pad.astype(BF16), "wvt": wv_pad.T.astype(BF16),
        "w_gate": w_gate.astype(BF16), "b_gate": row(p["gla_b_gate"]),
        "gla_norm_g": row(jnp.tile(p["gla_norm_g"], GLA_HEADS)),
        "ret_norm_g": row(jnp.tile(p["ret_norm_g"], RET_HEADS)),
        "w_out_mla": w_out[:MLA_WIDTH], "w_out_lin": w_out[MLA_WIDTH:],
    }


def _ffn_params(p, prefix):
    gain = lambda g: g.reshape(g.shape[0], 1, g.shape[1])
    return (gain(p[prefix + "_pre_g"]), gain(p[prefix + "_post_g"]),
            p[prefix + "_w_gate"].astype(BF16), p[prefix + "_w_up"].astype(BF16),
            p[prefix + "_w_down"].astype(BF16))


def _tile(n, pref):
    while n % pref:
        pref //= 2
    return pref


def kernel(x, positions, ffn1_pre_g, ffn1_post_g, ffn1_w_gate, ffn1_w_up, ffn1_w_down, mix_pre_g, mix_post_g, w_in, mla_q_norm_g, mla_w_q_up, mla_kv_norm_g, mla_w_kv_up, gla_w_gate_up, gla_b_gate, gla_norm_g, ret_norm_g, w_out, ffn2_pre_g, ffn2_post_g, ffn2_w_gate, ffn2_w_up, ffn2_w_down):
    p = dict(ffn1_pre_g=ffn1_pre_g, ffn1_post_g=ffn1_post_g, ffn1_w_gate=ffn1_w_gate, ffn1_w_up=ffn1_w_up,
             ffn1_w_down=ffn1_w_down, ffn2_pre_g=ffn2_pre_g, ffn2_post_g=ffn2_post_g, ffn2_w_gate=ffn2_w_gate,
             ffn2_w_up=ffn2_w_up, ffn2_w_down=ffn2_w_down)
    mixer = dict(mix_pre_g=mix_pre_g, mix_post_g=mix_post_g, w_in=w_in, mla_q_norm_g=mla_q_norm_g,
                 mla_w_q_up=mla_w_q_up, mla_kv_norm_g=mla_kv_norm_g, mla_w_kv_up=mla_w_kv_up,
                 gla_w_gate_up=gla_w_gate_up, gla_b_gate=gla_b_gate, gla_norm_g=gla_norm_g,
                 ret_norm_g=ret_norm_g, w_out=w_out)
    batch, seq, d = x.shape
    t = batch * seq
    depth = w_in.shape[0]
    tm = _tile(seq, 512)
    tq = _tile(seq, 512)
    tc = _tile(seq, 512)

    inv = ROPE_BASE ** (-jnp.arange(0, ROPE_DIM, 2, dtype=F32) / ROPE_DIM)
    inv_lane = jnp.tile(inv, LANES // HALF_ROPE).reshape(1, LANES)
    tb = _tile(tm, 256)
    idx = jnp.arange(tb)
    tri = ((idx[:, None] >= idx[None, :]) & (idx[:, None] // CHUNK == idx[None, :] // CHUNK)).astype(BF16)
    hd = jnp.arange(GLA_WIDTH) // GLA_DV
    ones_bd = (hd[:, None] == hd[None, :]).astype(BF16)
    log_gamma = jnp.log1p(-jnp.exp2(-5.0 - jnp.arange(RET_HEADS, dtype=F32)))
    lgam = jnp.repeat(log_gamma, RET_DK).reshape(1, QK_WIDTH)

    fw1 = _ffn_params(p, "ffn1")
    fw2 = _ffn_params(p, "ffn2")
    mw = jax.vmap(_prep_mixer_layer)(mixer)
    cosb, sinb = _rope_tables(positions.reshape(t, 1), inv_lane, tm=tm)

    xf = x.reshape(t, d)
    for l in range(depth):
        xf = _ffn(xf, l, fw1, tm=tm, tf=256)
        (q, k, gq, gk, b, gv, sg, rq, rk, rv, sr, vt) = _proj(xf, cosb, sinb, l, mw, tri, tm=tm)
        o_mla = _attention(q, k, vt, batch=batch, seq=seq, tq=tq, tk=tq, tn=_tile(tq, 256))
        o_lin = _linmix(gq, gk, b, gv, sg, rq, rk, rv, sr, l, mw["gla_norm_g"], mw["ret_norm_g"], lgam, ones_bd,
                        batch=batch, seq=seq, tc=tc)
        xf = _ffn(xf, l, fw2, tm=tm, tf=256,
                  mixer_out=(o_mla, o_lin, mw["w_out_mla"], mw["w_out_lin"], mw["mix_post_g"]))
    return xf.reshape(batch, seq, d)
```

```python
import functools
import math

import jax
import jax.numpy as jnp
from jax import lax
from jax.experimental import pallas as pl
from jax.experimental.pallas import tpu as pltpu

F32 = jnp.float32
BF16 = jnp.bfloat16

D_MODEL = 1024
D_FF = 2816
CHUNK = 64
EPS = 1e-6
ROPE_BASE = 10000.0
ROPE_DIM = 32
HALF_ROPE = ROPE_DIM // 2

MLA_HEADS = 8
MLA_Q_RANK = 256
MLA_KV_RANK = 128
MLA_NOPE = 64
MLA_V = 64
GLA_HEADS = 4
GLA_DK = 32
GLA_DV = 64
GLA_GATE_RANK = 16
GLA_TAU = 16.0
RET_HEADS = 4
RET_DK = 32
RET_DV = 64

LANES = 128
MLA_WIDTH = MLA_HEADS * MLA_V
GLA_WIDTH = GLA_HEADS * GLA_DV
RET_WIDTH = RET_HEADS * RET_DV
QK_WIDTH = GLA_HEADS * GLA_DK
HEAD_PAD = MLA_HEADS * LANES

C_MQ = 0
C_MKV = C_MQ + MLA_Q_RANK
C_KPE = C_MKV + MLA_KV_RANK
C_KPE_ROT = C_KPE + LANES
C_GQ = C_KPE_ROT + LANES
C_GK = C_GQ + QK_WIDTH
C_GV = C_GK + QK_WIDTH
C_GR = C_GV + GLA_WIDTH
C_RQ = C_GR + GLA_WIDTH
C_RQ_ROT = C_RQ + QK_WIDTH
C_RK = C_RQ_ROT + QK_WIDTH
C_RK_ROT = C_RK + QK_WIDTH
C_RV = C_RK_ROT + QK_WIDTH
C_RG = C_RV + RET_WIDTH
D_IN_EXT = C_RG + RET_WIDTH

LOG2_E = math.log2(math.e)
LOOKAHEAD = 4
KV_UNROLL = 2
V_ROWS = 80
VMEM_LIMIT = 56 * 1024 * 1024


def _rms(x, g):
    return x * lax.rsqrt(jnp.mean(x * x, axis=-1, keepdims=True) + EPS) * g


def _silu(x):
    return x * jax.nn.sigmoid(x)


def _cparams(*sem):
    return pltpu.CompilerParams(dimension_semantics=sem, vmem_limit_bytes=VMEM_LIMIT)


def _const_spec(shape):
    nd = len(shape)
    return pl.BlockSpec(shape, lambda *_: (0,) * nd, pipeline_mode=pl.Buffered(1))


def _layer_spec(stacked, layer):
    nd = stacked.ndim
    return pl.BlockSpec((None,) + stacked.shape[1:], lambda *_: (layer,) + (0,) * (nd - 1),
                        pipeline_mode=pl.Buffered(1))


def _ffn_kernel(*refs, tf, with_mixer_out):
    if with_mixer_out:
        x_ref, om_ref, ol_ref, wa_ref, wb_ref, mpost_ref = refs[:6]
        refs = refs[6:]
        m = (jnp.dot(om_ref[...], wa_ref[...], preferred_element_type=F32)
             + jnp.dot(ol_ref[...], wb_ref[...], preferred_element_type=F32))
        x = x_ref[...] + _rms(m, mpost_ref[...])
    else:
        x_ref = refs[0]
        refs = refs[1:]
        x = x_ref[...]
    pre_ref, post_ref, wg_ref, wu_ref, wd_ref, o_ref, h_ref = refs
    xn = _rms(x, pre_ref[...]).astype(BF16)
    for c in range(D_FF // tf):
        sl = slice(c * tf, (c + 1) * tf)
        g = jnp.dot(xn, wg_ref[:, sl].astype(BF16), preferred_element_type=F32)
        u = jnp.dot(xn, wu_ref[:, sl].astype(BF16), preferred_element_type=F32)
        h_ref[:, sl] = (_silu(g) * u).astype(BF16)
    f = jnp.dot(h_ref[...], wd_ref[...].astype(BF16), preferred_element_type=F32)
    o_ref[...] = x + 0.5 * _rms(f, post_ref[...])


def _ffn(x, layer, fw, *, tm, tf, mixer_out=None):
    t = x.shape[0]
    row = lambda w: pl.BlockSpec((tm, w), lambda i: (i, 0))
    args, specs = [x], [row(D_MODEL)]
    if mixer_out is not None:
        o_mla, o_lin, wa, wb, mpost = mixer_out
        args += [o_mla, o_lin, wa, wb, mpost]
        specs += [row(MLA_WIDTH), row(GLA_WIDTH + RET_WIDTH)] + [_layer_spec(a, layer) for a in (wa, wb, mpost)]
    args += list(fw)
    specs += [_layer_spec(a, layer) for a in fw]
    return pl.pallas_call(
        functools.partial(_ffn_kernel, tf=tf, with_mixer_out=mixer_out is not None),
        out_shape=jax.ShapeDtypeStruct(x.shape, F32),
        grid=(t // tm,),
        in_specs=specs,
        out_specs=row(D_MODEL),
        scratch_shapes=[pltpu.VMEM((tm, D_FF), BF16)],
        compiler_params=_cparams("parallel"),
        name="ffn_mixer_out" if mixer_out is not None else "ffn",
    )(*args)


def _rope_kernel(pos_ref, inv_ref, cos_ref, sin_ref):
    ang = pos_ref[...].astype(F32) * inv_ref[...]
    cos_ref[...] = jnp.cos(ang)
    sin_ref[...] = jnp.sin(ang)


def _rope_tables(pos_col, inv_lane, *, tm):
    t = pos_col.shape[0]
    return pl.pallas_call(
        _rope_kernel,
        out_shape=(jax.ShapeDtypeStruct((t, LANES), F32),) * 2,
        grid=(t // tm,),
        in_specs=[pl.BlockSpec((tm, 1), lambda i: (i, 0)), _const_spec((1, LANES))],
        out_specs=(pl.BlockSpec((tm, LANES), lambda i: (i, 0)),) * 2,
        compiler_params=_cparams("parallel"),
        name="rope_tables",
    )(pos_col, inv_lane)


def _proj_kernel(x_ref, cos_ref, sin_ref, pre_ref, win_ref, qg_ref, wq_ref, kvg_ref, wk_ref, wvt_ref,
                 wgate_ref, bgate_ref, tri_ref,
                 q_ref, k_ref, gq_ref, gk_ref, b_ref, gv_ref, sg_ref,
                 rq_ref, rk_ref, rv_ref, sr_ref, vt_ref):
    tm = x_ref.shape[0]
    xn = _rms(x_ref[...], pre_ref[...]).astype(BF16)

    chunk_starts = (0, C_KPE_ROT, C_RQ_ROT, D_IN_EXT)
    chunks = [jnp.dot(xn, win_ref[:, a:b], preferred_element_type=F32)
              for a, b in zip(chunk_starts[:-1], chunk_starts[1:])]

    def proj(c0, width):
        for a, b, chunk in zip(chunk_starts[:-1], chunk_starts[1:], chunks):
            if a <= c0 and c0 + width <= b:
                return chunk[:, c0 - a:c0 - a + width]
        raise ValueError("column group straddles two chunks")

    cosb = cos_ref[...]
    lane = lax.broadcasted_iota(jnp.int32, (tm, LANES), 1)
    sins = jnp.where((lane % ROPE_DIM) < HALF_ROPE, -sin_ref[...], sin_ref[...])

    hq = _rms(proj(C_MQ, MLA_Q_RANK), qg_ref[...]).astype(BF16)
    in_rope = (lane >= MLA_NOPE) & (lane < MLA_NOPE + ROPE_DIM)
    qcos = jnp.where(in_rope, cosb, 1.0)
    qsin = jnp.where(in_rope, sins, 0.0)
    q_scale = (MLA_NOPE + ROPE_DIM) ** -0.5 * LOG2_E
    qa = jnp.dot(hq, wq_ref[:, :HEAD_PAD], preferred_element_type=F32)
    qb = jnp.dot(hq, wq_ref[:, HEAD_PAD:], preferred_element_type=F32)
    for h in range(MLA_HEADS):
        sl = slice(h * LANES, (h + 1) * LANES)
        q_ref[:, sl] = ((qa[:, sl] * qcos + qb[:, sl] * qsin) * q_scale).astype(BF16)

    hkv = _rms(proj(C_MKV, MLA_KV_RANK), kvg_ref[...]).astype(BF16)
    g1 = proj(C_KPE, LANES)
    g2 = proj(C_KPE_ROT, LANES)
    kpe = jnp.where(in_rope, g1 * cosb + g2 * sins, 0.0)
    ones_row = jnp.where(lax.broadcasted_iota(jnp.int32, (V_ROWS, tm), 0) == MLA_V, 1.0, 0.0)
    kn = jnp.dot(hkv, wk_ref[...], preferred_element_type=F32)
    vt = lax.dot_general(wvt_ref[...], hkv, (((1,), (1,)), ((), ())), preferred_element_type=F32)
    for h in range(MLA_HEADS):
        sl = slice(h * LANES, (h + 1) * LANES)
        vs = slice(h * V_ROWS, (h + 1) * V_ROWS)
        k_ref[:, sl] = (kn[:, sl] + kpe).astype(BF16)
        vt_ref[vs, :] = (vt[vs, :] + ones_row).astype(BF16)

    z = jnp.dot(g1.astype(BF16), wgate_ref[...], preferred_element_type=F32) + bgate_ref[...]
    log_a = (jnp.minimum(z, 0.0) - jnp.log1p(jnp.exp(-jnp.abs(z)))) / GLA_TAU
    la_hi = log_a.astype(BF16)
    la_lo = (log_a - la_hi.astype(F32)).astype(BF16)
    la_split = jnp.concatenate([la_hi, la_lo], axis=1)
    tri = tri_ref[...]
    tb = tri.shape[0]
    for r in range(tm // tb):
        rs = slice(r * tb, (r + 1) * tb)
        cum = jnp.dot(tri, la_split[rs], preferred_element_type=F32)
        b_ref[rs, :] = cum[:, :QK_WIDTH] + cum[:, QK_WIDTH:]
    gq_ref[...] = (proj(C_GQ, QK_WIDTH) * (GLA_DK ** -0.5)).astype(BF16)
    gk_ref[...] = proj(C_GK, QK_WIDTH).astype(BF16)
    gv_ref[...] = proj(C_GV, GLA_WIDTH).astype(BF16)
    sg_ref[...] = _silu(proj(C_GR, GLA_WIDTH)).astype(BF16)

    rq_ref[...] = (proj(C_RQ, QK_WIDTH) * cosb + proj(C_RQ_ROT, QK_WIDTH) * sins).astype(BF16)
    rk_ref[...] = ((proj(C_RK, QK_WIDTH) * cosb + proj(C_RK_ROT, QK_WIDTH) * sins)
                   * (RET_DK ** -0.5)).astype(BF16)
    rv_ref[...] = proj(C_RV, RET_WIDTH).astype(BF16)
    sr_ref[...] = _silu(proj(C_RG, RET_WIDTH)).astype(BF16)


def _proj(x, cosb, sinb, layer, mw, tri, *, tm):
    t = x.shape[0]
    row = lambda w: pl.BlockSpec((tm, w), lambda i: (i, 0))
    out_widths = [(HEAD_PAD, BF16), (HEAD_PAD, BF16),
                  (QK_WIDTH, BF16), (QK_WIDTH, BF16), (QK_WIDTH, F32), (GLA_WIDTH, BF16), (GLA_WIDTH, BF16),
                  (QK_WIDTH, BF16), (QK_WIDTH, BF16), (RET_WIDTH, BF16), (RET_WIDTH, BF16)]
    stacked = [mw["mix_pre_g"], mw["w_in"], mw["q_norm_g"], mw["wq"], mw["kv_norm_g"], mw["wk"], mw["wvt"],
               mw["w_gate"], mw["b_gate"]]
    return pl.pallas_call(
        _proj_kernel,
        out_shape=tuple(jax.ShapeDtypeStruct((t, w), dt) for w, dt in out_widths)
        + (jax.ShapeDtypeStruct((MLA_HEADS * V_ROWS, t), BF16),),
        grid=(t // tm,),
        in_specs=[row(D_MODEL), row(LANES), row(LANES)] + [_layer_spec(a, layer) for a in stacked]
        + [_const_spec(tri.shape)],
        out_specs=tuple(row(w) for w, _ in out_widths) + (pl.BlockSpec((MLA_HEADS * V_ROWS, tm), lambda i: (0, i)),),
        compiler_params=_cparams("parallel"),
        name="mixer_proj",
    )(x, cosb, sinb, *stacked, tri)


def _attn_kernel(q_ref, k_ref, vt_ref, o_ref, m_ref, acc_ref, st_ref, *, tk, tn):
    tq = q_ref.shape[0]
    qi = pl.program_id(1)
    lane = lax.broadcasted_iota(jnp.int32, (tq, LANES), 1)
    nt = (((1,), (1,)), ((), ()))
    items = [(h, c) for h in range(MLA_HEADS) for c in range(tq // tn)]

    def key_rows(j, n):
        return pl.ds(pl.multiple_of(j * tk, tk), n)

    def n_keys(item, diag):
        return (item[1] + 1) * tn if diag else tk

    def scores(j, item, diag):
        h, c = item
        hs = slice(h * LANES, (h + 1) * LANES)
        return lax.dot_general(k_ref[key_rows(j, n_keys(item, diag)), hs], q_ref[c * tn:(c + 1) * tn, hs], nt,
                               preferred_element_type=F32)

    def consume(j, item, st, diag):
        h, c = item
        vs = slice(h * V_ROWS, (h + 1) * V_ROWS)
        cs = slice(c * tn, (c + 1) * tn)
        nk = n_keys(item, diag)
        st = st[:nk]
        if diag:
            key_chunk = lax.broadcasted_iota(jnp.int32, (nk, tn), 0) // CHUNK
            qry_chunk = (lax.broadcasted_iota(jnp.int32, (nk, tn), 1) + c * tn) // CHUNK
            st = jnp.where(key_chunk <= qry_chunk, st, -jnp.inf)
        m_old = m_ref[h, :, cs]
        m_new = jnp.maximum(m_old, jnp.max(st, axis=0, keepdims=True))
        p = jnp.exp2((st - m_new).astype(BF16))
        acc_ref[h, :, cs] = (acc_ref[h, :, cs] * jnp.exp2(m_old - m_new)
                             + jnp.dot(vt_ref[vs, key_rows(j, nk)], p, preferred_element_type=F32))
        m_ref[h, :, cs] = m_new

    for h in range(MLA_HEADS):
        m_ref[h] = jnp.full((1, tq), -jnp.inf, F32)
        acc_ref[h] = jnp.zeros((V_ROWS, tq), F32)

    for a in range(LOOKAHEAD):
        st_ref[a] = scores(0, items[a], False)

    def sweep(j, diag):
        pending = [st_ref[a] for a in range(LOOKAHEAD)]
        for i, item in enumerate(items):
            ahead = i + LOOKAHEAD
            if ahead < len(items):
                pending.append(scores(j, items[ahead], diag))
            elif not diag:
                pending.append(scores(j + 1, items[ahead - len(items)], False))
            consume(j, item, pending.pop(0), diag)
        for a, st in enumerate(pending):
            st_ref[a] = st

    def body(jj, carry):
        for u in range(KV_UNROLL):
            sweep(jj * KV_UNROLL + u, False)
        return carry

    n_full = qi // KV_UNROLL
    lax.fori_loop(0, n_full, body, 0)
    lax.fori_loop(n_full * KV_UNROLL, qi, lambda j, c: (sweep(j, False), c)[1], 0)
    sweep(qi, True)

    def head_out(h):
        acc = acc_ref[h]
        o = acc[:MLA_V] / acc[MLA_V:MLA_V + 1, :]
        return jnp.concatenate([o, jnp.zeros((LANES - MLA_V, tq), F32)], axis=0).T

    for p in range(MLA_HEADS // 2):
        odd = pltpu.roll(head_out(2 * p + 1), MLA_V, axis=1)
        o_ref[:, p * LANES:(p + 1) * LANES] = jnp.where(lane < MLA_V, head_out(2 * p), odd).astype(BF16)


def _attention(q, k, vt, *, batch, seq, tq, tk, tn):
    nq = seq // tq
    assert tq == tk and tq % tn == 0
    return pl.pallas_call(
        functools.partial(_attn_kernel, tk=tk, tn=tn),
        out_shape=jax.ShapeDtypeStruct((batch * seq, MLA_WIDTH), BF16),
        grid=(batch, nq),
        in_specs=[
            pl.BlockSpec((tq, HEAD_PAD), lambda b, i: (b * nq + i, 0)),
            pl.BlockSpec((seq, HEAD_PAD), lambda b, i: (b, 0)),
            pl.BlockSpec((MLA_HEADS * V_ROWS, seq), lambda b, i: (0, b)),
        ],
        out_specs=pl.BlockSpec((tq, MLA_WIDTH), lambda b, i: (b * nq + i, 0)),
        scratch_shapes=[pltpu.VMEM((MLA_HEADS, 1, tq), F32), pltpu.VMEM((MLA_HEADS, V_ROWS, tq), F32),
                        pltpu.VMEM((LOOKAHEAD, tk, tn), F32)],
        compiler_params=_cparams("parallel", "arbitrary"),
        name="mla_attention",
    )(q, k, vt)


def _head_rows(x, n_heads, head_lanes):
    lane_head = lax.broadcasted_iota(jnp.int32, x.shape, 1) // head_lanes
    return jnp.concatenate([jnp.where(lane_head == h, x, 0.0).astype(BF16) for h in range(n_heads)], axis=0)


def _linmix_kernel(gq_ref, gk_ref, b_ref, gv_ref, sg_ref, rq_ref, rk_ref, rv_ref, sr_ref,
                   gng_ref, rng_ref, lgam_ref, ones_ref, o_ref, sg_state, sr_state, og_ref, or_ref):
    tc = gq_ref.shape[0]
    L = CHUNK

    @pl.when(pl.program_id(1) == 0)
    def _():
        sg_state[...] = jnp.zeros(sg_state.shape, F32)
        sr_state[...] = jnp.zeros(sr_state.shape, F32)

    ri = lax.broadcasted_iota(jnp.int32, (L, GLA_HEADS * L), 0)
    cj = lax.broadcasted_iota(jnp.int32, (L, GLA_HEADS * L), 1) % L
    lower = cj <= ri
    lgam = lgam_ref[...]
    lgam_cols = jnp.concatenate(
        [jnp.broadcast_to(lgam[:, h * RET_DK:h * RET_DK + 1], (1, L)) for h in range(RET_HEADS)], axis=1)
    ret_decay = jnp.exp(lgam_cols * jnp.abs(ri - cj).astype(F32))
    pos = lax.broadcasted_iota(jnp.int32, (L, QK_WIDTH), 0).astype(F32)
    q_dec = jnp.exp(lgam * (pos + 1.0))
    k_dec = jnp.exp(lgam * (L - 1.0 - pos))
    chunk_dec = jnp.exp(lgam * float(L))
    state_head = lax.broadcasted_iota(jnp.int32, (GLA_WIDTH, QK_WIDTH), 0) // GLA_DV
    state_lane_head = lax.broadcasted_iota(jnp.int32, (GLA_WIDTH, QK_WIDTH), 1) // GLA_DK
    state_diag = state_head == state_lane_head
    nt = (((1,), (1,)), ((), ()))

    for c in range(tc // L):
        rs = slice(c * L, (c + 1) * L)
        b = b_ref[rs, :]
        b_last = b[L - 1:L, :]
        eb, enb = jnp.exp(b), jnp.exp(-b)
        q = gq_ref[rs, :].astype(F32)
        k = gk_ref[rs, :].astype(F32)
        v = gv_ref[rs, :].astype(F32)
        q_fwd = (q * eb).astype(BF16)
        a_low = lax.dot_general(q_fwd, _head_rows(k * enb, GLA_HEADS, GLA_DK), nt, preferred_element_type=F32)
        a_up = lax.dot_general((q * enb).astype(BF16), _head_rows(k * eb, GLA_HEADS, GLA_DK), nt,
                               preferred_element_type=F32)
        attn = jnp.where(lower, a_low, a_up).astype(BF16)
        o_intra = jnp.dot(attn, _head_rows(v, GLA_HEADS, GLA_DV), preferred_element_type=F32)
        s_t = sg_state[...]
        o_inter = lax.dot_general(q_fwd, s_t.astype(BF16), nt, preferred_element_type=F32)
        og_ref[rs, :] = o_intra + o_inter
        k_state = (k * jnp.exp(b_last - b)).astype(BF16)
        d_state = jnp.dot(v.T.astype(BF16), k_state, preferred_element_type=F32)
        sg_state[...] = s_t * jnp.exp(b_last) + jnp.where(state_diag, d_state, 0.0)

        q = rq_ref[rs, :].astype(F32)
        k = rk_ref[rs, :].astype(F32)
        v = rv_ref[rs, :].astype(F32)
        scores = lax.dot_general(q.astype(BF16), _head_rows(k, RET_HEADS, RET_DK), nt,
                                 preferred_element_type=F32) * ret_decay
        o_intra = jnp.dot(scores.astype(BF16), _head_rows(v, RET_HEADS, RET_DV), preferred_element_type=F32)
        s_t = sr_state[...]
        o_inter = lax.dot_general((q * q_dec).astype(BF16), s_t.astype(BF16), nt, preferred_element_type=F32)
        or_ref[rs, :] = o_intra + o_inter
        d_state = jnp.dot(v.T.astype(BF16), (k * k_dec).astype(BF16), preferred_element_type=F32)
        sr_state[...] = s_t * chunk_dec + jnp.where(state_diag, d_state, 0.0)

    def head_norm(o, g, gate):
        o2 = o * o
        hi = o2.astype(BF16)
        lo = (o2 - hi.astype(F32)).astype(BF16)
        ss = (jnp.dot(hi, ones_ref[...], preferred_element_type=F32)
              + jnp.dot(lo, ones_ref[...], preferred_element_type=F32))
        return o * lax.rsqrt(ss * (1.0 / GLA_DV) + EPS) * g * gate

    o_ref[:, :GLA_WIDTH] = head_norm(og_ref[...], gng_ref[...], sg_ref[...].astype(F32)).astype(BF16)
    o_ref[:, GLA_WIDTH:] = head_norm(or_ref[...], rng_ref[...], sr_ref[...].astype(F32)).astype(BF16)


def _linmix(gq, gk, b, gv, sg, rq, rk, rv, sr, layer, gng, rng, lgam, ones_bd, *, batch, seq, tc):
    nt = seq // tc
    row = lambda w: pl.BlockSpec((tc, w), lambda bb, i: (bb * nt + i, 0))
    return pl.pallas_call(
        _linmix_kernel,
        out_shape=jax.ShapeDtypeStruct((batch * seq, GLA_WIDTH + RET_WIDTH), BF16),
        grid=(batch, nt),
        in_specs=[row(QK_WIDTH), row(QK_WIDTH), row(QK_WIDTH), row(GLA_WIDTH), row(GLA_WIDTH),
                  row(QK_WIDTH), row(QK_WIDTH), row(RET_WIDTH), row(RET_WIDTH),
                  _layer_spec(gng, layer), _layer_spec(rng, layer), _const_spec(lgam.shape),
                  _const_spec(ones_bd.shape)],
        out_specs=row(GLA_WIDTH + RET_WIDTH),
        scratch_shapes=[pltpu.VMEM((GLA_WIDTH, QK_WIDTH), F32), pltpu.VMEM((RET_WIDTH, QK_WIDTH), F32),
                        pltpu.VMEM((tc, GLA_WIDTH), F32), pltpu.VMEM((tc, RET_WIDTH), F32)],
        compiler_params=_cparams("parallel", "arbitrary"),
        name="gla_retention",
    )(gq, gk, b, gv, sg, rq, rk, rv, sr, gng, rng, lgam, ones_bd)


def _rot_cols(w):
    return jnp.concatenate([w[..., HALF_ROPE:], w[..., :HALF_ROPE]], axis=-1)


def _prep_mixer_layer(p):
    w_in = p["w_in"]
    splits = (MLA_Q_RANK, MLA_KV_RANK, ROPE_DIM, QK_WIDTH, QK_WIDTH, GLA_WIDTH, GLA_GATE_RANK, GLA_WIDTH,
              QK_WIDTH, QK_WIDTH, RET_WIDTH, RET_WIDTH)
    offs = [0]
    for s in splits:
        offs.append(offs[-1] + s)
    mq, mkv, mkr, gq, gk, gv, gg, gr, rq, rk, rv, rg = (w_in[:, offs[i]:offs[i + 1]] for i in range(12))
    d = w_in.shape[0]
    zeros = lambda n: jnp.zeros((d, n), F32)
    kpe_grp = jnp.concatenate([gg, zeros(MLA_NOPE - GLA_GATE_RANK), mkr, zeros(LANES - MLA_NOPE - ROPE_DIM)], 1)
    kpe_rot_grp = jnp.concatenate([zeros(MLA_NOPE), _rot_cols(mkr), zeros(LANES - MLA_NOPE - ROPE_DIM)], 1)
    per_head_rot = lambda w, nh: _rot_cols(w.reshape(d, nh, ROPE_DIM)).reshape(d, nh * ROPE_DIM)
    w_in_ext = jnp.concatenate(
        [mq, mkv, kpe_grp, kpe_rot_grp, gq, gk, gv, gr,
         rq, per_head_rot(rq, RET_HEADS), rk, per_head_rot(rk, RET_HEADS), rv, rg], axis=1).astype(BF16)

    wq = p["mla_w_q_up"].reshape(MLA_Q_RANK, MLA_HEADS, MLA_NOPE + ROPE_DIM)
    pad_q = jnp.zeros((MLA_Q_RANK, MLA_HEADS, LANES - MLA_NOPE - ROPE_DIM), F32)
    wq_main = jnp.concatenate([wq, pad_q], axis=-1).reshape(MLA_Q_RANK, HEAD_PAD)
    wq_rot = jnp.concatenate([jnp.zeros((MLA_Q_RANK, MLA_HEADS, MLA_NOPE), F32),
                              _rot_cols(wq[..., MLA_NOPE:]), pad_q], axis=-1).reshape(MLA_Q_RANK, HEAD_PAD)
    wkv = p["mla_w_kv_up"].reshape(MLA_KV_RANK, MLA_HEADS, MLA_NOPE + MLA_V)
    pad_kv = jnp.zeros((MLA_KV_RANK, MLA_HEADS, LANES - MLA_NOPE), F32)
    wk_pad = jnp.concatenate([wkv[..., :MLA_NOPE], pad_kv], axis=-1).reshape(MLA_KV_RANK, HEAD_PAD)
    pad_v = jnp.zeros((MLA_KV_RANK, MLA_HEADS, V_ROWS - MLA_V), F32)
    wv_pad = jnp.concatenate([wkv[..., MLA_NOPE:], pad_v], axis=-1).reshape(MLA_KV_RANK, MLA_HEADS * V_ROWS)
    w_gate = jnp.concatenate([p["gla_w_gate_up"], jnp.zeros((LANES - GLA_GATE_RANK, QK_WIDTH), F32)], axis=0)
    row = lambda a: a.reshape(1, -1)
    w_out = p["w_out"].astype(BF16)
    return {
        "mix_pre_g": row(p["mix_pre_g"]), "mix_post_g": row(p["mix_post_g"]),
        "w_in": w_in_ext,
        "q_norm_g": row(p["mla_q_norm_g"]),
        "wq": jnp.concatenate([wq_main, wq_rot], axis=1).astype(BF16),
        "kv_norm_g": row(p["mla_kv_norm_g"]),
        "wk": wk_pad.astype(BF16), "wvt": wv_pad.T.astype(BF16),
        "w_gate": w_gate.astype(BF16), "b_gate": row(p["gla_b_gate"]),
        "gla_norm_g": row(jnp.tile(p["gla_norm_g"], GLA_HEADS)),
        "ret_norm_g": row(jnp.tile(p["ret_norm_g"], RET_HEADS)),
        "w_out_mla": w_out[:MLA_WIDTH], "w_out_lin": w_out[MLA_WIDTH:],
    }


def _ffn_params(p, prefix):
    gain = lambda g: g.reshape(g.shape[0], 1, g.shape[1])
    return (gain(p[prefix + "_pre_g"]), gain(p[prefix + "_post_g"]),
            p[prefix + "_w_gate"], p[prefix + "_w_up"], p[prefix + "_w_down"])


def _tile(n, pref):
    while n % pref:
        pref //= 2
    return pref


def kernel(x, positions, ffn1_pre_g, ffn1_post_g, ffn1_w_gate, ffn1_w_up, ffn1_w_down, mix_pre_g, mix_post_g, w_in, mla_q_norm_g, mla_w_q_up, mla_kv_norm_g, mla_w_kv_up, gla_w_gate_up, gla_b_gate, gla_norm_g, ret_norm_g, w_out, ffn2_pre_g, ffn2_post_g, ffn2_w_gate, ffn2_w_up, ffn2_w_down):
    p = dict(ffn1_pre_g=ffn1_pre_g, ffn1_post_g=ffn1_post_g, ffn1_w_gate=ffn1_w_gate, ffn1_w_up=ffn1_w_up,
             ffn1_w_down=ffn1_w_down, ffn2_pre_g=ffn2_pre_g, ffn2_post_g=ffn2_post_g, ffn2_w_gate=ffn2_w_gate,
             ffn2_w_up=ffn2_w_up, ffn2_w_down=ffn2_w_down)
    mixer = dict(mix_pre_g=mix_pre_g, mix_post_g=mix_post_g, w_in=w_in, mla_q_norm_g=mla_q_norm_g,
                 mla_w_q_up=mla_w_q_up, mla_kv_norm_g=mla_kv_norm_g, mla_w_kv_up=mla_w_kv_up,
                 gla_w_gate_up=gla_w_gate_up, gla_b_gate=gla_b_gate, gla_norm_g=gla_norm_g,
                 ret_norm_g=ret_norm_g, w_out=w_out)
    batch, seq, d = x.shape
    t = batch * seq
    depth = w_in.shape[0]
    tm = _tile(seq, 512)
    tq = _tile(seq, 512)
    tc = _tile(seq, 512)

    inv = ROPE_BASE ** (-jnp.arange(0, ROPE_DIM, 2, dtype=F32) / ROPE_DIM)
    inv_lane = jnp.tile(inv, LANES // HALF_ROPE).reshape(1, LANES)
    tb = _tile(tm, 256)
    idx = jnp.arange(tb)
    tri = ((idx[:, None] >= idx[None, :]) & (idx[:, None] // CHUNK == idx[None, :] // CHUNK)).astype(BF16)
    hd = jnp.arange(GLA_WIDTH) // GLA_DV
    ones_bd = (hd[:, None] == hd[None, :]).astype(BF16)
    log_gamma = jnp.log1p(-jnp.exp2(-5.0 - jnp.arange(RET_HEADS, dtype=F32)))
    lgam = jnp.repeat(log_gamma, RET_DK).reshape(1, QK_WIDTH)

    fw1 = _ffn_params(p, "ffn1")
    fw2 = _ffn_params(p, "ffn2")
    mw = jax.vmap(_prep_mixer_layer)(mixer)
    cosb, sinb = _rope_tables(positions.reshape(t, 1), inv_lane, tm=tm)

    xf = x.reshape(t, d)
    for l in range(depth):
        xf = _ffn(xf, l, fw1, tm=tm, tf=256)
        (q, k, gq, gk, b, gv, sg, rq, rk, rv, sr, vt) = _proj(xf, cosb, sinb, l, mw, tri, tm=tm)
        o_mla = _attention(q, k, vt, batch=batch, seq=seq, tq=tq, tk=tq, tn=_tile(tq, 256))
        o_lin = _linmix(gq, gk, b, gv, sg, rq, rk, rv, sr, l, mw["gla_norm_g"], mw["ret_norm_g"], lgam, ones_bd,
                        batch=batch, seq=seq, tc=tc)
        xf = _ffn(xf, l, fw2, tm=tm, tf=256,
                  mixer_out=(o_mla, o_lin, mw["w_out_mla"], mw["w_out_lin"], mw["mix_post_g"]))
    return xf.reshape(batch, seq, d)
```

```python
import functools
import math

import jax
import jax.numpy as jnp
from jax import lax
from jax.experimental import pallas as pl
from jax.experimental.pallas import tpu as pltpu

F32 = jnp.float32
BF16 = jnp.bfloat16

D_MODEL = 1024
D_FF = 2816
CHUNK = 64
EPS = 1e-6
ROPE_BASE = 10000.0
ROPE_DIM = 32
HALF_ROPE = ROPE_DIM // 2

MLA_HEADS = 8
MLA_Q_RANK = 256
MLA_KV_RANK = 128
MLA_NOPE = 64
MLA_V = 64
GLA_HEADS = 4
GLA_DK = 32
GLA_DV = 64
GLA_GATE_RANK = 16
GLA_TAU = 16.0
RET_HEADS = 4
RET_DK = 32
RET_DV = 64

LANES = 128
MLA_WIDTH = MLA_HEADS * MLA_V
GLA_WIDTH = GLA_HEADS * GLA_DV
RET_WIDTH = RET_HEADS * RET_DV
QK_WIDTH = GLA_HEADS * GLA_DK
HEAD_PAD = MLA_HEADS * LANES

C_MQ = 0
C_MKV = C_MQ + MLA_Q_RANK
C_KPE = C_MKV + MLA_KV_RANK
C_GQ = C_KPE + LANES
C_GK = C_GQ + QK_WIDTH
C_GV = C_GK + QK_WIDTH
C_GR = C_GV + GLA_WIDTH
C_RQ = C_GR + GLA_WIDTH
C_RK = C_RQ + QK_WIDTH
C_RV = C_RK + QK_WIDTH
C_RG = C_RV + RET_WIDTH
D_IN_EXT = C_RG + RET_WIDTH

LOG2_E = math.log2(math.e)
LOOKAHEAD = 4
KV_UNROLL = 2
V_ROWS = 80
VMEM_LIMIT = 56 * 1024 * 1024


def _rms(x, g):
    return x * lax.rsqrt(jnp.mean(x * x, axis=-1, keepdims=True) + EPS) * g


def _pack_rows(x):
    return pltpu.bitcast(x, jnp.uint32)


def _unpack_rows(x):
    return pltpu.bitcast(x, BF16)


def _silu(x):
    return x * jax.nn.sigmoid(x)


def _cparams(*sem):
    return pltpu.CompilerParams(dimension_semantics=sem, vmem_limit_bytes=VMEM_LIMIT)


def _const_spec(shape):
    nd = len(shape)
    return pl.BlockSpec(shape, lambda *_: (0,) * nd, pipeline_mode=pl.Buffered(1))


def _layer_spec(stacked, layer):
    nd = stacked.ndim
    return pl.BlockSpec((None,) + stacked.shape[1:], lambda *_: (layer,) + (0,) * (nd - 1),
                        pipeline_mode=pl.Buffered(1))


def _ffn_kernel(*refs, tf, with_mixer_out):
    if with_mixer_out:
        x_ref, om_ref, ol_ref, wa_ref, wb_ref, mpost_ref = refs[:6]
        refs = refs[6:]
        m = (jnp.dot(om_ref[...], wa_ref[...], preferred_element_type=F32)
             + jnp.dot(ol_ref[...], wb_ref[...], preferred_element_type=F32))
        x = x_ref[...] + _rms(m, mpost_ref[...])
    else:
        x_ref = refs[0]
        refs = refs[1:]
        x = x_ref[...]
    pre_ref, post_ref, wg_ref, wu_ref, wd_ref, o_ref, h_ref = refs
    xn = _rms(x, pre_ref[...]).astype(BF16)
    for c in range(D_FF // tf):
        sl = slice(c * tf, (c + 1) * tf)
        g = jnp.dot(xn, wg_ref[:, sl].astype(BF16), preferred_element_type=F32)
        u = jnp.dot(xn, wu_ref[:, sl].astype(BF16), preferred_element_type=F32)
        h_ref[:, sl] = (_silu(g) * u).astype(BF16)
    f = jnp.dot(h_ref[...], wd_ref[...].astype(BF16), preferred_element_type=F32)
    o_ref[...] = x + 0.5 * _rms(f, post_ref[...])


def _ffn(x, layer, fw, *, tm, tf, mixer_out=None):
    t = x.shape[0]
    row = lambda w: pl.BlockSpec((tm, w), lambda i: (i, 0))
    args, specs = [x], [row(D_MODEL)]
    if mixer_out is not None:
        o_mla, o_lin, wa, wb, mpost = mixer_out
        args += [o_mla, o_lin, wa, wb, mpost]
        specs += [row(MLA_WIDTH), row(GLA_WIDTH + RET_WIDTH)] + [_layer_spec(a, layer) for a in (wa, wb, mpost)]
    args += list(fw)
    specs += [_layer_spec(a, layer) for a in fw]
    return pl.pallas_call(
        functools.partial(_ffn_kernel, tf=tf, with_mixer_out=mixer_out is not None),
        out_shape=jax.ShapeDtypeStruct(x.shape, F32),
        grid=(t // tm,),
        in_specs=specs,
        out_specs=row(D_MODEL),
        scratch_shapes=[pltpu.VMEM((tm, D_FF), BF16)],
        compiler_params=_cparams("parallel"),
        name="ffn_mixer_out" if mixer_out is not None else "ffn",
    )(*args)


def _rope_kernel(pos_ref, inv_ref, cos_ref, sin_ref):
    ang = pos_ref[...].astype(F32) * inv_ref[...]
    cos_ref[...] = jnp.cos(ang)
    sin_ref[...] = jnp.sin(ang)


def _rope_tables(pos_col, inv_lane, *, tm):
    t = pos_col.shape[0]
    return pl.pallas_call(
        _rope_kernel,
        out_shape=(jax.ShapeDtypeStruct((t, LANES), F32),) * 2,
        grid=(t // tm,),
        in_specs=[pl.BlockSpec((tm, 1), lambda i: (i, 0)), _const_spec((1, LANES))],
        out_specs=(pl.BlockSpec((tm, LANES), lambda i: (i, 0)),) * 2,
        compiler_params=_cparams("parallel"),
        name="rope_tables",
    )(pos_col, inv_lane)


def _proj_kernel(x_ref, cos_ref, sin_ref, pre_ref, win_ref, qg_ref, wq_ref, kvg_ref, wk_ref, wvt_ref,
                 wgate_ref, bgate_ref, tri_ref,
                 q_ref, k_ref, gq_ref, gk_ref, b_ref, gv_ref, sg_ref,
                 rq_ref, rk_ref, rv_ref, sr_ref, vt_ref):
    tm = x_ref.shape[0]
    xn = _rms(x_ref[...], pre_ref[...]).astype(BF16)

    chunk_starts = (0, C_GQ, C_RQ, D_IN_EXT)
    chunks = [jnp.dot(xn, win_ref[:, a:b], preferred_element_type=F32)
              for a, b in zip(chunk_starts[:-1], chunk_starts[1:])]

    def proj(c0, width):
        for a, b, chunk in zip(chunk_starts[:-1], chunk_starts[1:], chunks):
            if a <= c0 and c0 + width <= b:
                return chunk[:, c0 - a:c0 - a + width]
        raise ValueError("column group straddles two chunks")

    cosb = cos_ref[...]
    lane = lax.broadcasted_iota(jnp.int32, (tm, LANES), 1)
    first_half = (lane % ROPE_DIM) < HALF_ROPE
    sins = jnp.where(first_half, -sin_ref[...], sin_ref[...])

    def rot_half(v):
        return jnp.where(first_half, pltpu.roll(v, LANES - HALF_ROPE, axis=1), pltpu.roll(v, HALF_ROPE, axis=1))

    hq = _rms(proj(C_MQ, MLA_Q_RANK), qg_ref[...]).astype(BF16)
    in_rope = (lane >= MLA_NOPE) & (lane < MLA_NOPE + ROPE_DIM)
    qcos = jnp.where(in_rope, cosb, 1.0)
    qsin = jnp.where(in_rope, sins, 0.0)
    q_scale = (MLA_NOPE + ROPE_DIM) ** -0.5 * LOG2_E
    qa = jnp.dot(hq, wq_ref[...], preferred_element_type=F32)
    for h in range(MLA_HEADS):
        sl = slice(h * LANES, (h + 1) * LANES)
        q_ref[:, sl] = _pack_rows(((qa[:, sl] * qcos + rot_half(qa[:, sl]) * qsin) * q_scale).astype(BF16))

    hkv = _rms(proj(C_MKV, MLA_KV_RANK), kvg_ref[...]).astype(BF16)
    g1 = proj(C_KPE, LANES)
    kpe = jnp.where(in_rope, g1 * cosb + rot_half(g1) * sins, 0.0)
    ones_row = jnp.where(lax.broadcasted_iota(jnp.int32, (V_ROWS, tm), 0) == MLA_V, 1.0, 0.0)
    kn = jnp.dot(hkv, wk_ref[...], preferred_element_type=F32)
    vt = lax.dot_general(wvt_ref[...], hkv, (((1,), (1,)), ((), ())), preferred_element_type=F32)
    for h in range(MLA_HEADS):
        sl = slice(h * LANES, (h + 1) * LANES)
        vs = slice(h * V_ROWS, (h + 1) * V_ROWS)
        k_ref[:, sl] = _pack_rows((kn[:, sl] + kpe).astype(BF16))
        vt_ref[h * V_ROWS // 2:(h + 1) * V_ROWS // 2, :] = _pack_rows((vt[vs, :] + ones_row).astype(BF16))

    z = jnp.dot(g1.astype(BF16), wgate_ref[...], preferred_element_type=F32) + bgate_ref[...]
    log_a = (jnp.minimum(z, 0.0) - jnp.log1p(jnp.exp(-jnp.abs(z)))) / GLA_TAU
    la_hi = log_a.astype(BF16)
    la_lo = (log_a - la_hi.astype(F32)).astype(BF16)
    la_split = jnp.concatenate([la_hi, la_lo], axis=1)
    tri = tri_ref[...]
    tb = tri.shape[0]
    for r in range(tm // tb):
        rs = slice(r * tb, (r + 1) * tb)
        cum = jnp.dot(tri, la_split[rs], preferred_element_type=F32)
        b_ref[rs, :] = cum[:, :QK_WIDTH] + cum[:, QK_WIDTH:]
    gq_ref[...] = (proj(C_GQ, QK_WIDTH) * (GLA_DK ** -0.5)).astype(BF16)
    gk_ref[...] = proj(C_GK, QK_WIDTH).astype(BF16)
    gv_ref[...] = proj(C_GV, GLA_WIDTH).astype(BF16)
    sg_ref[...] = _silu(proj(C_GR, GLA_WIDTH)).astype(BF16)

    rq = proj(C_RQ, QK_WIDTH)
    rk = proj(C_RK, QK_WIDTH)
    rq_ref[...] = (rq * cosb + rot_half(rq) * sins).astype(BF16)
    rk_ref[...] = ((rk * cosb + rot_half(rk) * sins) * (RET_DK ** -0.5)).astype(BF16)
    rv_ref[...] = proj(C_RV, RET_WIDTH).astype(BF16)
    sr_ref[...] = _silu(proj(C_RG, RET_WIDTH)).astype(BF16)


def _proj(x, cosb, sinb, layer, mw, tri, *, tm):
    t = x.shape[0]
    row = lambda w: pl.BlockSpec((tm, w), lambda i: (i, 0))
    out_widths = [(QK_WIDTH, BF16), (QK_WIDTH, BF16), (QK_WIDTH, F32), (GLA_WIDTH, BF16), (GLA_WIDTH, BF16),
                  (QK_WIDTH, BF16), (QK_WIDTH, BF16), (RET_WIDTH, BF16), (RET_WIDTH, BF16)]
    stacked = [mw["mix_pre_g"], mw["w_in"], mw["q_norm_g"], mw["wq"], mw["kv_norm_g"], mw["wk"], mw["wvt"],
               mw["w_gate"], mw["b_gate"]]
    return pl.pallas_call(
        _proj_kernel,
        out_shape=(jax.ShapeDtypeStruct((t // 2, HEAD_PAD), jnp.uint32),) * 2
        + tuple(jax.ShapeDtypeStruct((t, w), dt) for w, dt in out_widths)
        + (jax.ShapeDtypeStruct((MLA_HEADS * V_ROWS // 2, t), jnp.uint32),),
        grid=(t // tm,),
        in_specs=[row(D_MODEL), row(LANES), row(LANES)] + [_layer_spec(a, layer) for a in stacked]
        + [_const_spec(tri.shape)],
        out_specs=(pl.BlockSpec((tm // 2, HEAD_PAD), lambda i: (i, 0)),) * 2 + tuple(row(w) for w, _ in out_widths)
        + (pl.BlockSpec((MLA_HEADS * V_ROWS // 2, tm), lambda i: (0, i)),),
        compiler_params=_cparams("parallel"),
        name="mixer_proj",
    )(x, cosb, sinb, *stacked, tri)


def _attn_kernel(q_ref, k_ref, vt_ref, o_ref, m_ref, acc_ref, st_ref, *, tk, tn):
    tq = 2 * q_ref.shape[0]
    qi = pl.program_id(1)
    lane = lax.broadcasted_iota(jnp.int32, (tq, LANES), 1)
    nt = (((1,), (1,)), ((), ()))
    items = [(h, c) for h in range(MLA_HEADS) for c in range(tq // tn)]

    def key_rows(j, n):
        return pl.ds(pl.multiple_of(j * tk, tk), n)

    def packed_key_rows(j, n):
        return pl.ds(pl.multiple_of(j * (tk // 2), tk // 2), n // 2)

    def n_keys(item, diag):
        return (item[1] + 1) * tn if diag else tk

    def scores(j, item, diag):
        h, c = item
        hs = slice(h * LANES, (h + 1) * LANES)
        keys = _unpack_rows(k_ref[packed_key_rows(j, n_keys(item, diag)), hs])
        queries = _unpack_rows(q_ref[c * tn // 2:(c + 1) * tn // 2, hs])
        return lax.dot_general(keys, queries, nt, preferred_element_type=F32)

    def consume(j, item, st, diag):
        h, c = item
        vs = slice(h * V_ROWS // 2, (h + 1) * V_ROWS // 2)
        cs = slice(c * tn, (c + 1) * tn)
        nk = n_keys(item, diag)
        st = st[:nk]
        if diag:
            key_chunk = lax.broadcasted_iota(jnp.int32, (nk, tn), 0) // CHUNK
            qry_chunk = (lax.broadcasted_iota(jnp.int32, (nk, tn), 1) + c * tn) // CHUNK
            st = jnp.where(key_chunk <= qry_chunk, st, -jnp.inf)
        m_old = m_ref[h, :, cs]
        m_new = jnp.maximum(m_old, jnp.max(st, axis=0, keepdims=True))
        p = jnp.exp2((st - m_new).astype(BF16))
        acc_ref[h, :, cs] = (acc_ref[h, :, cs] * jnp.exp2(m_old - m_new)
                             + jnp.dot(_unpack_rows(vt_ref[vs, key_rows(j, nk)]), p, preferred_element_type=F32))
        m_ref[h, :, cs] = m_new

    for h in range(MLA_HEADS):
        m_ref[h] = jnp.full((1, tq), -jnp.inf, F32)
        acc_ref[h] = jnp.zeros((V_ROWS, tq), F32)

    for a in range(LOOKAHEAD):
        st_ref[a] = scores(0, items[a], False)

    def sweep(j, diag):
        pending = [st_ref[a] for a in range(LOOKAHEAD)]
        for i, item in enumerate(items):
            ahead = i + LOOKAHEAD
            if ahead < len(items):
                pending.append(scores(j, items[ahead], diag))
            elif not diag:
                pending.append(scores(j + 1, items[ahead - len(items)], False))
            consume(j, item, pending.pop(0), diag)
        for a, st in enumerate(pending):
            st_ref[a] = st

    def body(jj, carry):
        for u in range(KV_UNROLL):
            sweep(jj * KV_UNROLL + u, False)
        return carry

    n_full = qi // KV_UNROLL
    lax.fori_loop(0, n_full, body, 0)
    lax.fori_loop(n_full * KV_UNROLL, qi, lambda j, c: (sweep(j, False), c)[1], 0)
    sweep(qi, True)

    def head_out(h):
        acc = acc_ref[h]
        o = acc[:MLA_V] / acc[MLA_V:MLA_V + 1, :]
        return jnp.concatenate([o, jnp.zeros((LANES - MLA_V, tq), F32)], axis=0).T

    for p in range(MLA_HEADS // 2):
        odd = pltpu.roll(head_out(2 * p + 1), MLA_V, axis=1)
        o_ref[:, p * LANES:(p + 1) * LANES] = jnp.where(lane < MLA_V, head_out(2 * p), odd).astype(BF16)


def _attention(q, k, vt, *, batch, seq, tq, tk, tn):
    nq = seq // tq
    assert tq == tk and tq % tn == 0
    return pl.pallas_call(
        functools.partial(_attn_kernel, tk=tk, tn=tn),
        out_shape=jax.ShapeDtypeStruct((batch * seq, MLA_WIDTH), BF16),
        grid=(batch, nq),
        in_specs=[
            pl.BlockSpec((tq // 2, HEAD_PAD), lambda b, i: (b * nq + i, 0)),
            pl.BlockSpec((seq // 2, HEAD_PAD), lambda b, i: (b, 0)),
            pl.BlockSpec((MLA_HEADS * V_ROWS // 2, seq), lambda b, i: (0, b)),
        ],
        out_specs=pl.BlockSpec((tq, MLA_WIDTH), lambda b, i: (b * nq + i, 0)),
        scratch_shapes=[pltpu.VMEM((MLA_HEADS, 1, tq), F32), pltpu.VMEM((MLA_HEADS, V_ROWS, tq), F32),
                        pltpu.VMEM((LOOKAHEAD, tk, tn), F32)],
        compiler_params=_cparams("parallel", "arbitrary"),
        name="mla_attention",
    )(q, k, vt)


def _head_rows(x, n_heads, head_lanes):
    lane_head = lax.broadcasted_iota(jnp.int32, x.shape, 1) // head_lanes
    return jnp.concatenate([jnp.where(lane_head == h, x, 0.0).astype(BF16) for h in range(n_heads)], axis=0)


def _linmix_kernel(gq_ref, gk_ref, b_ref, gv_ref, sg_ref, rq_ref, rk_ref, rv_ref, sr_ref,
                   gng_ref, rng_ref, lgam_ref, ones_ref, o_ref, sg_state, sr_state, og_ref, or_ref):
    tc = gq_ref.shape[0]
    L = CHUNK

    @pl.when(pl.program_id(1) == 0)
    def _():
        sg_state[...] = jnp.zeros(sg_state.shape, F32)
        sr_state[...] = jnp.zeros(sr_state.shape, F32)

    ri = lax.broadcasted_iota(jnp.int32, (L, GLA_HEADS * L), 0)
    cj = lax.broadcasted_iota(jnp.int32, (L, GLA_HEADS * L), 1) % L
    lower = cj <= ri
    lgam = lgam_ref[...]
    lgam_cols = jnp.concatenate(
        [jnp.broadcast_to(lgam[:, h * RET_DK:h * RET_DK + 1], (1, L)) for h in range(RET_HEADS)], axis=1)
    ret_decay = jnp.exp(lgam_cols * jnp.abs(ri - cj).astype(F32))
    pos = lax.broadcasted_iota(jnp.int32, (L, QK_WIDTH), 0).astype(F32)
    q_dec = jnp.exp(lgam * (pos + 1.0))
    k_dec = jnp.exp(lgam * (L - 1.0 - pos))
    chunk_dec = jnp.exp(lgam * float(L))
    state_head = lax.broadcasted_iota(jnp.int32, (GLA_WIDTH, QK_WIDTH), 0) // GLA_DV
    state_lane_head = lax.broadcasted_iota(jnp.int32, (GLA_WIDTH, QK_WIDTH), 1) // GLA_DK
    state_diag = state_head == state_lane_head
    nt = (((1,), (1,)), ((), ()))

    for c in range(tc // L):
        rs = slice(c * L, (c + 1) * L)
        b = b_ref[rs, :]
        b_last = b[L - 1:L, :]
        eb, enb = jnp.exp(b), jnp.exp(-b)
        q = gq_ref[rs, :].astype(F32)
        k = gk_ref[rs, :].astype(F32)
        v = gv_ref[rs, :].astype(F32)
        q_fwd = (q * eb).astype(BF16)
        a_low = lax.dot_general(q_fwd, _head_rows(k * enb, GLA_HEADS, GLA_DK), nt, preferred_element_type=F32)
        a_up = lax.dot_general((q * enb).astype(BF16), _head_rows(k * eb, GLA_HEADS, GLA_DK), nt,
                               preferred_element_type=F32)
        attn = jnp.where(lower, a_low, a_up).astype(BF16)
        o_intra = jnp.dot(attn, _head_rows(v, GLA_HEADS, GLA_DV), preferred_element_type=F32)
        s_t = sg_state[...]
        o_inter = lax.dot_general(q_fwd, s_t.astype(BF16), nt, preferred_element_type=F32)
        og_ref[rs, :] = o_intra + o_inter
        k_state = (k * jnp.exp(b_last - b)).astype(BF16)
        d_state = jnp.dot(v.T.astype(BF16), k_state, preferred_element_type=F32)
        sg_state[...] = s_t * jnp.exp(b_last) + jnp.where(state_diag, d_state, 0.0)

        q = rq_ref[rs, :].astype(F32)
        k = rk_ref[rs, :].astype(F32)
        v = rv_ref[rs, :].astype(F32)
        scores = lax.dot_general(q.astype(BF16), _head_rows(k, RET_HEADS, RET_DK), nt,
                                 preferred_element_type=F32) * ret_decay
        o_intra = jnp.dot(scores.astype(BF16), _head_rows(v, RET_HEADS, RET_DV), preferred_element_type=F32)
        s_t = sr_state[...]
        o_inter = lax.dot_general((q * q_dec).astype(BF16), s_t.astype(BF16), nt, preferred_element_type=F32)
        or_ref[rs, :] = o_intra + o_inter
        d_state = jnp.dot(v.T.astype(BF16), (k * k_dec).astype(BF16), preferred_element_type=F32)
        sr_state[...] = s_t * chunk_dec + jnp.where(state_diag, d_state, 0.0)

    def head_norm(o, g, gate):
        o2 = o * o
        hi = o2.astype(BF16)
        lo = (o2 - hi.astype(F32)).astype(BF16)
        ss = (jnp.dot(hi, ones_ref[...], preferred_element_type=F32)
              + jnp.dot(lo, ones_ref[...], preferred_element_type=F32))
        return o * lax.rsqrt(ss * (1.0 / GLA_DV) + EPS) * g * gate

    o_ref[:, :GLA_WIDTH] = head_norm(og_ref[...], gng_ref[...], sg_ref[...].astype(F32)).astype(BF16)
    o_ref[:, GLA_WIDTH:] = head_norm(or_ref[...], rng_ref[...], sr_ref[...].astype(F32)).astype(BF16)


def _linmix(gq, gk, b, gv, sg, rq, rk, rv, sr, layer, gng, rng, lgam, ones_bd, *, batch, seq, tc):
    nt = seq // tc
    row = lambda w: pl.BlockSpec((tc, w), lambda bb, i: (bb * nt + i, 0))
    return pl.pallas_call(
        _linmix_kernel,
        out_shape=jax.ShapeDtypeStruct((batch * seq, GLA_WIDTH + RET_WIDTH), BF16),
        grid=(batch, nt),
        in_specs=[row(QK_WIDTH), row(QK_WIDTH), row(QK_WIDTH), row(GLA_WIDTH), row(GLA_WIDTH),
                  row(QK_WIDTH), row(QK_WIDTH), row(RET_WIDTH), row(RET_WIDTH),
                  _layer_spec(gng, layer), _layer_spec(rng, layer), _const_spec(lgam.shape),
                  _const_spec(ones_bd.shape)],
        out_specs=row(GLA_WIDTH + RET_WIDTH),
        scratch_shapes=[pltpu.VMEM((GLA_WIDTH, QK_WIDTH), F32), pltpu.VMEM((RET_WIDTH, QK_WIDTH), F32),
                        pltpu.VMEM((tc, GLA_WIDTH), F32), pltpu.VMEM((tc, RET_WIDTH), F32)],
        compiler_params=_cparams("parallel", "arbitrary"),
        name="gla_retention",
    )(gq, gk, b, gv, sg, rq, rk, rv, sr, gng, rng, lgam, ones_bd)


def _prep_mixer_layer(p):
    w_in = p["w_in"]
    splits = (MLA_Q_RANK, MLA_KV_RANK, ROPE_DIM, QK_WIDTH, QK_WIDTH, GLA_WIDTH, GLA_GATE_RANK, GLA_WIDTH,
              QK_WIDTH, QK_WIDTH, RET_WIDTH, RET_WIDTH)
    offs = [0]
    for s in splits:
        offs.append(offs[-1] + s)
    mq, mkv, mkr, gq, gk, gv, gg, gr, rq, rk, rv, rg = (w_in[:, offs[i]:offs[i + 1]] for i in range(12))
    d = w_in.shape[0]
    zeros = lambda n: jnp.zeros((d, n), F32)
    kpe_grp = jnp.concatenate([gg, zeros(MLA_NOPE - GLA_GATE_RANK), mkr, zeros(LANES - MLA_NOPE - ROPE_DIM)], 1)
    w_in_ext = jnp.concatenate([mq, mkv, kpe_grp, gq, gk, gv, gr, rq, rk, rv, rg], axis=1).astype(BF16)

    wq = p["mla_w_q_up"].reshape(MLA_Q_RANK, MLA_HEADS, MLA_NOPE + ROPE_DIM)
    pad_q = jnp.zeros((MLA_Q_RANK, MLA_HEADS, LANES - MLA_NOPE - ROPE_DIM), F32)
    wq_main = jnp.concatenate([wq, pad_q], axis=-1).reshape(MLA_Q_RANK, HEAD_PAD)
    wkv = p["mla_w_kv_up"].reshape(MLA_KV_RANK, MLA_HEADS, MLA_NOPE + MLA_V)
    pad_kv = jnp.zeros((MLA_KV_RANK, MLA_HEADS, LANES - MLA_NOPE), F32)
    wk_pad = jnp.concatenate([wkv[..., :MLA_NOPE], pad_kv], axis=-1).reshape(MLA_KV_RANK, HEAD_PAD)
    pad_v = jnp.zeros((MLA_KV_RANK, MLA_HEADS, V_ROWS - MLA_V), F32)
    wv_pad = jnp.concatenate([wkv[..., MLA_NOPE:], pad_v], axis=-1).reshape(MLA_KV_RANK, MLA_HEADS * V_ROWS)
    w_gate = jnp.concatenate([p["gla_w_gate_up"], jnp.zeros((LANES - GLA_GATE_RANK, QK_WIDTH), F32)], axis=0)
    row = lambda a: a.reshape(1, -1)
    w_out = p["w_out"].astype(BF16)
    return {
        "mix_pre_g": row(p["mix_pre_g"]), "mix_post_g": row(p["mix_post_g"]),
        "w_in": w_in_ext,
        "q_norm_g": row(p["mla_q_norm_g"]),
        "wq": wq_main.astype(BF16),
        "kv_norm_g": row(p["mla_kv_norm_g"]),
        "wk": wk_pad.astype(BF16), "wvt": wv_pad.T.astype(BF16),
        "w_gate": w_gate.astype(BF16), "b_gate": row(p["gla_b_gate"]),
        "gla_norm_g": row(jnp.tile(p["gla_norm_g"], GLA_HEADS)),
        "ret_norm_g": row(jnp.tile(p["ret_norm_g"], RET_HEADS)),
        "w_out_mla": w_out[:MLA_WIDTH], "w_out_lin": w_out[MLA_WIDTH:],
    }


def _ffn_params(p, prefix):
    gain = lambda g: g.reshape(g.shape[0], 1, g.shape[1])
    return (gain(p[prefix + "_pre_g"]), gain(p[prefix + "_post_g"]),
            p[prefix + "_w_gate"], p[prefix + "_w_up"], p[prefix + "_w_down"])


def _tile(n, pref):
    while n % pref:
        pref //= 2
    return pref


def kernel(x, positions, ffn1_pre_g, ffn1_post_g, ffn1_w_gate, ffn1_w_up, ffn1_w_down, mix_pre_g, mix_post_g, w_in, mla_q_norm_g, mla_w_q_up, mla_kv_norm_g, mla_w_kv_up, gla_w_gate_up, gla_b_gate, gla_norm_g, ret_norm_g, w_out, ffn2_pre_g, ffn2_post_g, ffn2_w_gate, ffn2_w_up, ffn2_w_down):
    p = dict(ffn1_pre_g=ffn1_pre_g, ffn1_post_g=ffn1_post_g, ffn1_w_gate=ffn1_w_gate, ffn1_w_up=ffn1_w_up,
             ffn1_w_down=ffn1_w_down, ffn2_pre_g=ffn2_pre_g, ffn2_post_g=ffn2_post_g, ffn2_w_gate=ffn2_w_gate,
             ffn2_w_up=ffn2_w_up, ffn2_w_down=ffn2_w_down)
    mixer = dict(mix_pre_g=mix_pre_g, mix_post_g=mix_post_g, w_in=w_in, mla_q_norm_g=mla_q_norm_g,
                 mla_w_q_up=mla_w_q_up, mla_kv_norm_g=mla_kv_norm_g, mla_w_kv_up=mla_w_kv_up,
                 gla_w_gate_up=gla_w_gate_up, gla_b_gate=gla_b_gate, gla_norm_g=gla_norm_g,
                 ret_norm_g=ret_norm_g, w_out=w_out)
    batch, seq, d = x.shape
    t = batch * seq
    depth = w_in.shape[0]
    tm = _tile(seq, 512)
    tq = _tile(seq, 512)
    tc = _tile(seq, 512)

    inv = ROPE_BASE ** (-jnp.arange(0, ROPE_DIM, 2, dtype=F32) / ROPE_DIM)
    inv_lane = jnp.tile(inv, LANES // HALF_ROPE).reshape(1, LANES)
    tb = _tile(tm, 256)
    idx = jnp.arange(tb)
    tri = ((idx[:, None] >= idx[None, :]) & (idx[:, None] // CHUNK == idx[None, :] // CHUNK)).astype(BF16)
    hd = jnp.arange(GLA_WIDTH) // GLA_DV
    ones_bd = (hd[:, None] == hd[None, :]).astype(BF16)
    log_gamma = jnp.log1p(-jnp.exp2(-5.0 - jnp.arange(RET_HEADS, dtype=F32)))
    lgam = jnp.repeat(log_gamma, RET_DK).reshape(1, QK_WIDTH)

    fw1 = _ffn_params(p, "ffn1")
    fw2 = _ffn_params(p, "ffn2")
    mw = jax.vmap(_prep_mixer_layer)(mixer)
    cosb, sinb = _rope_tables(positions.reshape(t, 1), inv_lane, tm=tm)

    xf = x.reshape(t, d)
    for l in range(depth):
        xf = _ffn(xf, l, fw1, tm=tm, tf=256)
        (q, k, gq, gk, b, gv, sg, rq, rk, rv, sr, vt) = _proj(xf, cosb, sinb, l, mw, tri, tm=tm)
        o_mla = _attention(q, k, vt, batch=batch, seq=seq, tq=tq, tk=tq, tn=_tile(tq, 256))
        o_lin = _linmix(gq, gk, b, gv, sg, rq, rk, rv, sr, l, mw["gla_norm_g"], mw["ret_norm_g"], lgam, ones_bd,
                        batch=batch, seq=seq, tc=tc)
        xf = _ffn(xf, l, fw2, tm=tm, tf=256,
                  mixer_out=(o_mla, o_lin, mw["w_out_mla"], mw["w_out_lin"], mw["mix_post_g"]))
    return xf.reshape(batch, seq, d)
```

```python
import functools
import math

import jax
import jax.numpy as jnp
from jax import lax
from jax.experimental import pallas as pl
from jax.experimental.pallas import tpu as pltpu

F32 = jnp.float32
BF16 = jnp.bfloat16

D_MODEL = 1024
D_FF = 2816
CHUNK = 64
EPS = 1e-6
ROPE_BASE = 10000.0
ROPE_DIM = 32
HALF_ROPE = ROPE_DIM // 2

MLA_HEADS = 8
MLA_Q_RANK = 256
MLA_KV_RANK = 128
MLA_NOPE = 64
MLA_V = 64
GLA_HEADS = 4
GLA_DK = 32
GLA_DV = 64
GLA_GATE_RANK = 16
GLA_TAU = 16.0
RET_HEADS = 4
RET_DK = 32
RET_DV = 64

LANES = 128
MLA_WIDTH = MLA_HEADS * MLA_V
GLA_WIDTH = GLA_HEADS * GLA_DV
RET_WIDTH = RET_HEADS * RET_DV
QK_WIDTH = GLA_HEADS * GLA_DK
HEAD_PAD = MLA_HEADS * LANES

C_MQ = 0
C_MKV = C_MQ + MLA_Q_RANK
C_KPE = C_MKV + MLA_KV_RANK
C_GQ = C_KPE + LANES
C_GK = C_GQ + QK_WIDTH
C_GV = C_GK + QK_WIDTH
C_GR = C_GV + GLA_WIDTH
C_RQ = C_GR + GLA_WIDTH
C_RK = C_RQ + QK_WIDTH
C_RV = C_RK + QK_WIDTH
C_RG = C_RV + RET_WIDTH
D_IN_EXT = C_RG + RET_WIDTH

LOG2_E = math.log2(math.e)
LOOKAHEAD = 4
KV_UNROLL = 2
V_ROWS = 80
VMEM_LIMIT = 56 * 1024 * 1024


def _rms(x, g):
    return x * lax.rsqrt(jnp.mean(x * x, axis=-1, keepdims=True) + EPS) * g


def _pack_rows(x):
    return pltpu.bitcast(x, jnp.uint32)


def _unpack_rows(x):
    return pltpu.bitcast(x, BF16)


def _silu(x):
    return x * jax.nn.sigmoid(x)


def _cparams(*sem):
    return pltpu.CompilerParams(dimension_semantics=sem, vmem_limit_bytes=VMEM_LIMIT)


def _const_spec(shape):
    nd = len(shape)
    return pl.BlockSpec(shape, lambda *_: (0,) * nd, pipeline_mode=pl.Buffered(1))


def _layer_spec(stacked, layer):
    nd = stacked.ndim
    return pl.BlockSpec((None,) + stacked.shape[1:], lambda *_: (layer,) + (0,) * (nd - 1),
                        pipeline_mode=pl.Buffered(1))


def _ffn_kernel(*refs, tf, with_mixer_out):
    if with_mixer_out:
        x_ref, om_ref, ol_ref, wa_ref, wb_ref, mpost_ref = refs[:6]
        refs = refs[6:]
        m = (jnp.dot(om_ref[...], wa_ref[...], preferred_element_type=F32)
             + jnp.dot(ol_ref[...], wb_ref[...], preferred_element_type=F32))
        x = x_ref[...] + _rms(m, mpost_ref[...])
    else:
        x_ref = refs[0]
        refs = refs[1:]
        x = x_ref[...]
    pre_ref, post_ref, wg_ref, wu_ref, wd_ref, o_ref, h_ref = refs
    xn = _rms(x, pre_ref[...]).astype(BF16)
    for c in range(D_FF // tf):
        sl = slice(c * tf, (c + 1) * tf)
        g = jnp.dot(xn, wg_ref[:, sl].astype(BF16), preferred_element_type=F32)
        u = jnp.dot(xn, wu_ref[:, sl].astype(BF16), preferred_element_type=F32)
        h_ref[:, sl] = (_silu(g) * u).astype(BF16)
    f = jnp.dot(h_ref[...], wd_ref[...].astype(BF16), preferred_element_type=F32)
    o_ref[...] = x + 0.5 * _rms(f, post_ref[...])


def _ffn(x, layer, fw, *, tm, tf, mixer_out=None):
    t = x.shape[0]
    row = lambda w: pl.BlockSpec((tm, w), lambda i: (i, 0))
    args, specs = [x], [row(D_MODEL)]
    if mixer_out is not None:
        o_mla, o_lin, wa, wb, mpost = mixer_out
        args += [o_mla, o_lin, wa, wb, mpost]
        specs += [row(MLA_WIDTH), row(GLA_WIDTH + RET_WIDTH)] + [_layer_spec(a, layer) for a in (wa, wb, mpost)]
    args += list(fw)
    specs += [_layer_spec(a, layer) for a in fw]
    return pl.pallas_call(
        functools.partial(_ffn_kernel, tf=tf, with_mixer_out=mixer_out is not None),
        out_shape=jax.ShapeDtypeStruct(x.shape, F32),
        grid=(t // tm,),
        in_specs=specs,
        out_specs=row(D_MODEL),
        scratch_shapes=[pltpu.VMEM((tm, D_FF), BF16)],
        compiler_params=_cparams("parallel"),
        name="ffn_mixer_out" if mixer_out is not None else "ffn",
    )(*args)


def _rope_kernel(pos_ref, inv_ref, cos_ref, sin_ref):
    ang = pos_ref[...].astype(F32) * inv_ref[...]
    cos_ref[...] = jnp.cos(ang)
    sin_ref[...] = jnp.sin(ang)


def _rope_tables(pos_col, inv_lane, *, tm):
    t = pos_col.shape[0]
    return pl.pallas_call(
        _rope_kernel,
        out_shape=(jax.ShapeDtypeStruct((t, LANES), F32),) * 2,
        grid=(t // tm,),
        in_specs=[pl.BlockSpec((tm, 1), lambda i: (i, 0)), _const_spec((1, LANES))],
        out_specs=(pl.BlockSpec((tm, LANES), lambda i: (i, 0)),) * 2,
        compiler_params=_cparams("parallel"),
        name="rope_tables",
    )(pos_col, inv_lane)


def _proj_kernel(x_ref, cos_ref, sin_ref, pre_ref, win_ref, qg_ref, wq_ref, kvg_ref, wk_ref, wvt_ref,
                 wgate_ref, bgate_ref, tri_ref,
                 q_ref, k_ref, gq_ref, gk_ref, b_ref, gv_ref, sg_ref,
                 rq_ref, rk_ref, rv_ref, sr_ref, vt_ref):
    tm = x_ref.shape[0]
    xn = _rms(x_ref[...], pre_ref[...]).astype(BF16)

    chunk_starts = (0, C_GQ, C_RQ, D_IN_EXT)
    chunks = [jnp.dot(xn, win_ref[:, a:b], preferred_element_type=F32)
              for a, b in zip(chunk_starts[:-1], chunk_starts[1:])]

    def proj(c0, width):
        for a, b, chunk in zip(chunk_starts[:-1], chunk_starts[1:], chunks):
            if a <= c0 and c0 + width <= b:
                return chunk[:, c0 - a:c0 - a + width]
        raise ValueError("column group straddles two chunks")

    cosb = cos_ref[...]
    lane = lax.broadcasted_iota(jnp.int32, (tm, LANES), 1)
    first_half = (lane % ROPE_DIM) < HALF_ROPE
    sins = jnp.where(first_half, -sin_ref[...], sin_ref[...])

    def rot_half(v):
        return jnp.where(first_half, pltpu.roll(v, LANES - HALF_ROPE, axis=1), pltpu.roll(v, HALF_ROPE, axis=1))

    hq = _rms(proj(C_MQ, MLA_Q_RANK), qg_ref[...]).astype(BF16)
    in_rope = (lane >= MLA_NOPE) & (lane < MLA_NOPE + ROPE_DIM)
    qcos = jnp.where(in_rope, cosb, 1.0)
    qsin = jnp.where(in_rope, sins, 0.0)
    q_scale = (MLA_NOPE + ROPE_DIM) ** -0.5 * LOG2_E
    qa = jnp.dot(hq, wq_ref[...], preferred_element_type=F32)
    for h in range(MLA_HEADS):
        sl = slice(h * LANES, (h + 1) * LANES)
        q_ref[:, sl] = _pack_rows(((qa[:, sl] * qcos + rot_half(qa[:, sl]) * qsin) * q_scale).astype(BF16))

    hkv = _rms(proj(C_MKV, MLA_KV_RANK), kvg_ref[...]).astype(BF16)
    g1 = proj(C_KPE, LANES)
    kpe = jnp.where(in_rope, g1 * cosb + rot_half(g1) * sins, 0.0)
    ones_row = jnp.where(lax.broadcasted_iota(jnp.int32, (V_ROWS, tm), 0) == MLA_V, 1.0, 0.0)
    kn = jnp.dot(hkv, wk_ref[...], preferred_element_type=F32)
    vt = lax.dot_general(wvt_ref[...], hkv, (((1,), (1,)), ((), ())), preferred_element_type=F32)
    for h in range(MLA_HEADS):
        sl = slice(h * LANES, (h + 1) * LANES)
        vs = slice(h * V_ROWS, (h + 1) * V_ROWS)
        k_ref[:, sl] = _pack_rows((kn[:, sl] + kpe).astype(BF16))
        vt_ref[h * V_ROWS // 2:(h + 1) * V_ROWS // 2, :] = _pack_rows((vt[vs, :] + ones_row).astype(BF16))

    z = jnp.dot(g1.astype(BF16), wgate_ref[...], preferred_element_type=F32) + bgate_ref[...]
    log_a = (jnp.minimum(z, 0.0) - jnp.log1p(jnp.exp(-jnp.abs(z)))) / GLA_TAU
    la_hi = log_a.astype(BF16)
    la_lo = (log_a - la_hi.astype(F32)).astype(BF16)
    la_split = jnp.concatenate([la_hi, la_lo], axis=1)
    tri = tri_ref[...]
    tb = tri.shape[0]
    for r in range(tm // tb):
        rs = slice(r * tb, (r + 1) * tb)
        cum = jnp.dot(tri, la_split[rs], preferred_element_type=F32)
        b_ref[rs, :] = cum[:, :QK_WIDTH] + cum[:, QK_WIDTH:]
    gq_ref[...] = (proj(C_GQ, QK_WIDTH) * (GLA_DK ** -0.5)).astype(BF16)
    gk_ref[...] = proj(C_GK, QK_WIDTH).astype(BF16)
    gv_ref[...] = proj(C_GV, GLA_WIDTH).astype(BF16)
    sg_ref[...] = _silu(proj(C_GR, GLA_WIDTH)).astype(BF16)

    rq = proj(C_RQ, QK_WIDTH)
    rk = proj(C_RK, QK_WIDTH)
    rq_ref[...] = (rq * cosb + rot_half(rq) * sins).astype(BF16)
    rk_ref[...] = ((rk * cosb + rot_half(rk) * sins) * (RET_DK ** -0.5)).astype(BF16)
    rv_ref[...] = proj(C_RV, RET_WIDTH).astype(BF16)
    sr_ref[...] = _silu(proj(C_RG, RET_WIDTH)).astype(BF16)


def _proj(x, cosb, sinb, layer, mw, tri, *, tm):
    t = x.shape[0]
    row = lambda w: pl.BlockSpec((tm, w), lambda i: (i, 0))
    out_widths = [(QK_WIDTH, BF16), (QK_WIDTH, BF16), (QK_WIDTH, F32), (GLA_WIDTH, BF16), (GLA_WIDTH, BF16),
                  (QK_WIDTH, BF16), (QK_WIDTH, BF16), (RET_WIDTH, BF16), (RET_WIDTH, BF16)]
    stacked = [mw["mix_pre_g"], mw["w_in"], mw["q_norm_g"], mw["wq"], mw["kv_norm_g"], mw["wk"], mw["wvt"],
               mw["w_gate"], mw["b_gate"]]
    return pl.pallas_call(
        _proj_kernel,
        out_shape=(jax.ShapeDtypeStruct((t // 2, HEAD_PAD), jnp.uint32),) * 2
        + tuple(jax.ShapeDtypeStruct((t, w), dt) for w, dt in out_widths)
        + (jax.ShapeDtypeStruct((MLA_HEADS * V_ROWS // 2, t), jnp.uint32),),
        grid=(t // tm,),
        in_specs=[row(D_MODEL), row(LANES), row(LANES)] + [_layer_spec(a, layer) for a in stacked]
        + [_const_spec(tri.shape)],
        out_specs=(pl.BlockSpec((tm // 2, HEAD_PAD), lambda i: (i, 0)),) * 2 + tuple(row(w) for w, _ in out_widths)
        + (pl.BlockSpec((MLA_HEADS * V_ROWS // 2, tm), lambda i: (0, i)),),
        compiler_params=_cparams("parallel"),
        name="mixer_proj",
    )(x, cosb, sinb, *stacked, tri)


def _attn_kernel(q_ref, k_ref, vt_ref, o_ref, m_ref, acc_ref, st_ref, *, tk, tn):
    tq = 2 * q_ref.shape[0]
    qi = pl.program_id(1)
    lane = lax.broadcasted_iota(jnp.int32, (tq, LANES), 1)
    nt = (((1,), (1,)), ((), ()))
    items = [(h, c) for h in range(MLA_HEADS) for c in range(tq // tn)]

    def key_rows(j, n):
        return pl.ds(pl.multiple_of(j * tk, tk), n)

    def packed_key_rows(j, n):
        return pl.ds(pl.multiple_of(j * (tk // 2), tk // 2), n // 2)

    def n_keys(item, diag):
        return (item[1] + 1) * tn if diag else tk

    def scores(j, item, diag):
        h, c = item
        hs = slice(h * LANES, (h + 1) * LANES)
        keys = _unpack_rows(k_ref[packed_key_rows(j, n_keys(item, diag)), hs])
        queries = _unpack_rows(q_ref[c * tn // 2:(c + 1) * tn // 2, hs])
        return lax.dot_general(keys, queries, nt, preferred_element_type=F32).astype(BF16)

    def chunk_mask(c):
        nk = (c + 1) * tn
        key_chunk = lax.broadcasted_iota(jnp.int32, (nk, tn), 0) // CHUNK
        qry_chunk = (lax.broadcasted_iota(jnp.int32, (nk, tn), 1) + c * tn) // CHUNK
        return jnp.where(key_chunk <= qry_chunk, 0.0, -jnp.inf).astype(BF16)

    diag_masks = [chunk_mask(c) for c in range(tq // tn)]

    def consume(j, item, st, diag):
        h, c = item
        vs = slice(h * V_ROWS // 2, (h + 1) * V_ROWS // 2)
        cs = slice(c * tn, (c + 1) * tn)
        nk = n_keys(item, diag)
        st = st[:nk]
        if diag:
            st = st + diag_masks[c]
        m_old = m_ref[h, :, cs]
        m_new = jnp.maximum(m_old, jnp.max(st, axis=0, keepdims=True).astype(F32))
        p = jnp.exp2(st - m_new.astype(BF16))
        acc_ref[h, :, cs] = (acc_ref[h, :, cs] * jnp.exp2(m_old - m_new)
                             + jnp.dot(_unpack_rows(vt_ref[vs, key_rows(j, nk)]), p, preferred_element_type=F32))
        m_ref[h, :, cs] = m_new

    for h in range(MLA_HEADS):
        m_ref[h] = jnp.full((1, tq), -jnp.inf, F32)
        acc_ref[h] = jnp.zeros((V_ROWS, tq), F32)

    for a in range(LOOKAHEAD):
        st_ref[a] = scores(0, items[a], False)

    def sweep(j, diag):
        pending = [st_ref[a] for a in range(LOOKAHEAD)]
        for i, item in enumerate(items):
            ahead = i + LOOKAHEAD
            if ahead < len(items):
                pending.append(scores(j, items[ahead], diag))
            elif not diag:
                pending.append(scores(j + 1, items[ahead - len(items)], False))
            consume(j, item, pending.pop(0), diag)
        for a, st in enumerate(pending):
            st_ref[a] = st

    def body(jj, carry):
        for u in range(KV_UNROLL):
            sweep(jj * KV_UNROLL + u, False)
        return carry

    n_full = qi // KV_UNROLL
    lax.fori_loop(0, n_full, body, 0)
    lax.fori_loop(n_full * KV_UNROLL, qi, lambda j, c: (sweep(j, False), c)[1], 0)
    sweep(qi, True)

    def head_out(h):
        acc = acc_ref[h]
        o = acc[:MLA_V] / acc[MLA_V:MLA_V + 1, :]
        return jnp.concatenate([o, jnp.zeros((LANES - MLA_V, tq), F32)], axis=0).T

    for p in range(MLA_HEADS // 2):
        odd = pltpu.roll(head_out(2 * p + 1), MLA_V, axis=1)
        o_ref[:, p * LANES:(p + 1) * LANES] = jnp.where(lane < MLA_V, head_out(2 * p), odd).astype(BF16)


def _attention(q, k, vt, *, batch, seq, tq, tk, tn):
    nq = seq // tq
    assert tq == tk and tq % tn == 0
    return pl.pallas_call(
        functools.partial(_attn_kernel, tk=tk, tn=tn),
        out_shape=jax.ShapeDtypeStruct((batch * seq, MLA_WIDTH), BF16),
        grid=(batch, nq),
        in_specs=[
            pl.BlockSpec((tq // 2, HEAD_PAD), lambda b, i: (b * nq + i, 0)),
            pl.BlockSpec((seq // 2, HEAD_PAD), lambda b, i: (b, 0)),
            pl.BlockSpec((MLA_HEADS * V_ROWS // 2, seq), lambda b, i: (0, b)),
        ],
        out_specs=pl.BlockSpec((tq, MLA_WIDTH), lambda b, i: (b * nq + i, 0)),
        scratch_shapes=[pltpu.VMEM((MLA_HEADS, 1, tq), F32), pltpu.VMEM((MLA_HEADS, V_ROWS, tq), F32),
                        pltpu.VMEM((LOOKAHEAD, tk, tn), BF16)],
        compiler_params=_cparams("parallel", "arbitrary"),
        name="mla_attention",
    )(q, k, vt)


def _head_rows(x, n_heads, head_lanes):
    lane_head = lax.broadcasted_iota(jnp.int32, x.shape, 1) // head_lanes
    return jnp.concatenate([jnp.where(lane_head == h, x, 0.0).astype(BF16) for h in range(n_heads)], axis=0)


def _linmix_kernel(gq_ref, gk_ref, b_ref, gv_ref, sg_ref, rq_ref, rk_ref, rv_ref, sr_ref,
                   gng_ref, rng_ref, lgam_ref, ones_ref, o_ref, sg_state, sr_state, og_ref, or_ref):
    tc = gq_ref.shape[0]
    L = CHUNK

    @pl.when(pl.program_id(1) == 0)
    def _():
        sg_state[...] = jnp.zeros(sg_state.shape, F32)
        sr_state[...] = jnp.zeros(sr_state.shape, F32)

    ri = lax.broadcasted_iota(jnp.int32, (L, GLA_HEADS * L), 0)
    cj = lax.broadcasted_iota(jnp.int32, (L, GLA_HEADS * L), 1) % L
    lower = cj <= ri
    lgam = lgam_ref[...]
    lgam_cols = jnp.concatenate(
        [jnp.broadcast_to(lgam[:, h * RET_DK:h * RET_DK + 1], (1, L)) for h in range(RET_HEADS)], axis=1)
    ret_decay = jnp.exp(lgam_cols * jnp.abs(ri - cj).astype(F32))
    pos = lax.broadcasted_iota(jnp.int32, (L, QK_WIDTH), 0).astype(F32)
    q_dec = jnp.exp(lgam * (pos + 1.0))
    k_dec = jnp.exp(lgam * (L - 1.0 - pos))
    chunk_dec = jnp.exp(lgam * float(L))
    state_head = lax.broadcasted_iota(jnp.int32, (GLA_WIDTH, QK_WIDTH), 0) // GLA_DV
    state_lane_head = lax.broadcasted_iota(jnp.int32, (GLA_WIDTH, QK_WIDTH), 1) // GLA_DK
    state_diag = state_head == state_lane_head
    nt = (((1,), (1,)), ((), ()))

    for c in range(tc // L):
        rs = slice(c * L, (c + 1) * L)
        b = b_ref[rs, :]
        b_last = b[L - 1:L, :]
        eb, enb = jnp.exp(b), jnp.exp(-b)
        q = gq_ref[rs, :].astype(F32)
        k = gk_ref[rs, :].astype(F32)
        v = gv_ref[rs, :].astype(F32)
        q_fwd = (q * eb).astype(BF16)
        a_low = lax.dot_general(q_fwd, _head_rows(k * enb, GLA_HEADS, GLA_DK), nt, preferred_element_type=F32)
        a_up = lax.dot_general((q * enb).astype(BF16), _head_rows(k * eb, GLA_HEADS, GLA_DK), nt,
                               preferred_element_type=F32)
        attn = jnp.where(lower, a_low, a_up).astype(BF16)
        o_intra = jnp.dot(attn, _head_rows(v, GLA_HEADS, GLA_DV), preferred_element_type=F32)
        s_t = sg_state[...]
        o_inter = lax.dot_general(q_fwd, s_t.astype(BF16), nt, preferred_element_type=F32)
        og_ref[rs, :] = o_intra + o_inter
        k_state = (k * jnp.exp(b_last - b)).astype(BF16)
        d_state = jnp.dot(v.T.astype(BF16), k_state, preferred_element_type=F32)
        sg_state[...] = s_t * jnp.exp(b_last) + jnp.where(state_diag, d_state, 0.0)

        q = rq_ref[rs, :].astype(F32)
        k = rk_ref[rs, :].astype(F32)
        v = rv_ref[rs, :].astype(F32)
        scores = lax.dot_general(q.astype(BF16), _head_rows(k, RET_HEADS, RET_DK), nt,
                                 preferred_element_type=F32) * ret_decay
        o_intra = jnp.dot(scores.astype(BF16), _head_rows(v, RET_HEADS, RET_DV), preferred_element_type=F32)
        s_t = sr_state[...]
        o_inter = lax.dot_general((q * q_dec).astype(BF16), s_t.astype(BF16), nt, preferred_element_type=F32)
        or_ref[rs, :] = o_intra + o_inter
        d_state = jnp.dot(v.T.astype(BF16), (k * k_dec).astype(BF16), preferred_element_type=F32)
        sr_state[...] = s_t * chunk_dec + jnp.where(state_diag, d_state, 0.0)

    def head_norm(o, g, gate):
        o2 = o * o
        hi = o2.astype(BF16)
        lo = (o2 - hi.astype(F32)).astype(BF16)
        ss = (jnp.dot(hi, ones_ref[...], preferred_element_type=F32)
              + jnp.dot(lo, ones_ref[...], preferred_element_type=F32))
        return o * lax.rsqrt(ss * (1.0 / GLA_DV) + EPS) * g * gate

    o_ref[:, :GLA_WIDTH] = head_norm(og_ref[...], gng_ref[...], sg_ref[...].astype(F32)).astype(BF16)
    o_ref[:, GLA_WIDTH:] = head_norm(or_ref[...], rng_ref[...], sr_ref[...].astype(F32)).astype(BF16)


def _linmix(gq, gk, b, gv, sg, rq, rk, rv, sr, layer, gng, rng, lgam, ones_bd, *, batch, seq, tc):
    nt = seq // tc
    row = lambda w: pl.BlockSpec((tc, w), lambda bb, i: (bb * nt + i, 0))
    return pl.pallas_call(
        _linmix_kernel,
        out_shape=jax.ShapeDtypeStruct((batch * seq, GLA_WIDTH + RET_WIDTH), BF16),
        grid=(batch, nt),
        in_specs=[row(QK_WIDTH), row(QK_WIDTH), row(QK_WIDTH), row(GLA_WIDTH), row(GLA_WIDTH),
                  row(QK_WIDTH), row(QK_WIDTH), row(RET_WIDTH), row(RET_WIDTH),
                  _layer_spec(gng, layer), _layer_spec(rng, layer), _const_spec(lgam.shape),
                  _const_spec(ones_bd.shape)],
        out_specs=row(GLA_WIDTH + RET_WIDTH),
        scratch_shapes=[pltpu.VMEM((GLA_WIDTH, QK_WIDTH), F32), pltpu.VMEM((RET_WIDTH, QK_WIDTH), F32),
                        pltpu.VMEM((tc, GLA_WIDTH), F32), pltpu.VMEM((tc, RET_WIDTH), F32)],
        compiler_params=_cparams("parallel", "arbitrary"),
        name="gla_retention",
    )(gq, gk, b, gv, sg, rq, rk, rv, sr, gng, rng, lgam, ones_bd)


def _prep_mixer_layer(p):
    w_in = p["w_in"]
    splits = (MLA_Q_RANK, MLA_KV_RANK, ROPE_DIM, QK_WIDTH, QK_WIDTH, GLA_WIDTH, GLA_GATE_RANK, GLA_WIDTH,
              QK_WIDTH, QK_WIDTH, RET_WIDTH, RET_WIDTH)
    offs = [0]
    for s in splits:
        offs.append(offs[-1] + s)
    mq, mkv, mkr, gq, gk, gv, gg, gr, rq, rk, rv, rg = (w_in[:, offs[i]:offs[i + 1]] for i in range(12))
    d = w_in.shape[0]
    zeros = lambda n: jnp.zeros((d, n), F32)
    kpe_grp = jnp.concatenate([gg, zeros(MLA_NOPE - GLA_GATE_RANK), mkr, zeros(LANES - MLA_NOPE - ROPE_DIM)], 1)
    w_in_ext = jnp.concatenate([mq, mkv, kpe_grp, gq, gk, gv, gr, rq, rk, rv, rg], axis=1).astype(BF16)

    wq = p["mla_w_q_up"].reshape(MLA_Q_RANK, MLA_HEADS, MLA_NOPE + ROPE_DIM)
    pad_q = jnp.zeros((MLA_Q_RANK, MLA_HEADS, LANES - MLA_NOPE - ROPE_DIM), F32)
    wq_main = jnp.concatenate([wq, pad_q], axis=-1).reshape(MLA_Q_RANK, HEAD_PAD)
    wkv = p["mla_w_kv_up"].reshape(MLA_KV_RANK, MLA_HEADS, MLA_NOPE + MLA_V)
    pad_kv = jnp.zeros((MLA_KV_RANK, MLA_HEADS, LANES - MLA_NOPE), F32)
    wk_pad = jnp.concatenate([wkv[..., :MLA_NOPE], pad_kv], axis=-1).reshape(MLA_KV_RANK, HEAD_PAD)
    pad_v = jnp.zeros((MLA_KV_RANK, MLA_HEADS, V_ROWS - MLA_V), F32)
    wv_pad = jnp.concatenate([wkv[..., MLA_NOPE:], pad_v], axis=-1).reshape(MLA_KV_RANK, MLA_HEADS * V_ROWS)
    w_gate = jnp.concatenate([p["gla_w_gate_up"], jnp.zeros((LANES - GLA_GATE_RANK, QK_WIDTH), F32)], axis=0)
    row = lambda a: a.reshape(1, -1)
    w_out = p["w_out"].astype(BF16)
    return {
        "mix_pre_g": row(p["mix_pre_g"]), "mix_post_g": row(p["mix_post_g"]),
        "w_in": w_in_ext,
        "q_norm_g": row(p["mla_q_norm_g"]),
        "wq": wq_main.astype(BF16),
        "kv_norm_g": row(p["mla_kv_norm_g"]),
        "wk": wk_pad.astype(BF16), "wvt": wv_pad.T.astype(BF16),
        "w_gate": w_gate.astype(BF16), "b_gate": row(p["gla_b_gate"]),
        "gla_norm_g": row(jnp.tile(p["gla_norm_g"], GLA_HEADS)),
        "ret_norm_g": row(jnp.tile(p["ret_norm_g"], RET_HEADS)),
        "w_out_mla": w_out[:MLA_WIDTH], "w_out_lin": w_out[MLA_WIDTH:],
    }


def _ffn_params(p, prefix):
    gain = lambda g: g.reshape(g.shape[0], 1, g.shape[1])
    return (gain(p[prefix + "_pre_g"]), gain(p[prefix + "_post_g"]),
            p[prefix + "_w_gate"], p[prefix + "_w_up"], p[prefix + "_w_down"])


def _tile(n, pref):
    while n % pref:
        pref //= 2
    return pref


def kernel(x, positions, ffn1_pre_g, ffn1_post_g, ffn1_w_gate, ffn1_w_up, ffn1_w_down, mix_pre_g, mix_post_g, w_in, mla_q_norm_g, mla_w_q_up, mla_kv_norm_g, mla_w_kv_up, gla_w_gate_up, gla_b_gate, gla_norm_g, ret_norm_g, w_out, ffn2_pre_g, ffn2_post_g, ffn2_w_gate, ffn2_w_up, ffn2_w_down):
    p = dict(ffn1_pre_g=ffn1_pre_g, ffn1_post_g=ffn1_post_g, ffn1_w_gate=ffn1_w_gate, ffn1_w_up=ffn1_w_up,
             ffn1_w_down=ffn1_w_down, ffn2_pre_g=ffn2_pre_g, ffn2_post_g=ffn2_post_g, ffn2_w_gate=ffn2_w_gate,
             ffn2_w_up=ffn2_w_up, ffn2_w_down=ffn2_w_down)
    mixer = dict(mix_pre_g=mix_pre_g, mix_post_g=mix_post_g, w_in=w_in, mla_q_norm_g=mla_q_norm_g,
                 mla_w_q_up=mla_w_q_up, mla_kv_norm_g=mla_kv_norm_g, mla_w_kv_up=mla_w_kv_up,
                 gla_w_gate_up=gla_w_gate_up, gla_b_gate=gla_b_gate, gla_norm_g=gla_norm_g,
                 ret_norm_g=ret_norm_g, w_out=w_out)
    batch, seq, d = x.shape
    t = batch * seq
    depth = w_in.shape[0]
    tm = _tile(seq, 512)
    tq = _tile(seq, 512)
    tc = _tile(seq, 512)

    inv = ROPE_BASE ** (-jnp.arange(0, ROPE_DIM, 2, dtype=F32) / ROPE_DIM)
    inv_lane = jnp.tile(inv, LANES // HALF_ROPE).reshape(1, LANES)
    tb = _tile(tm, 256)
    idx = jnp.arange(tb)
    tri = ((idx[:, None] >= idx[None, :]) & (idx[:, None] // CHUNK == idx[None, :] // CHUNK)).astype(BF16)
    hd = jnp.arange(GLA_WIDTH) // GLA_DV
    ones_bd = (hd[:, None] == hd[None, :]).astype(BF16)
    log_gamma = jnp.log1p(-jnp.exp2(-5.0 - jnp.arange(RET_HEADS, dtype=F32)))
    lgam = jnp.repeat(log_gamma, RET_DK).reshape(1, QK_WIDTH)

    fw1 = _ffn_params(p, "ffn1")
    fw2 = _ffn_params(p, "ffn2")
    mw = jax.vmap(_prep_mixer_layer)(mixer)
    cosb, sinb = _rope_tables(positions.reshape(t, 1), inv_lane, tm=tm)

    xf = x.reshape(t, d)
    for l in range(depth):
        xf = _ffn(xf, l, fw1, tm=tm, tf=256)
        (q, k, gq, gk, b, gv, sg, rq, rk, rv, sr, vt) = _proj(xf, cosb, sinb, l, mw, tri, tm=tm)
        o_mla = _attention(q, k, vt, batch=batch, seq=seq, tq=tq, tk=tq, tn=_tile(tq, 256))
        o_lin = _linmix(gq, gk, b, gv, sg, rq, rk, rv, sr, l, mw["gla_norm_g"], mw["ret_norm_g"], lgam, ones_bd,
                        batch=batch, seq=seq, tc=tc)
        xf = _ffn(xf, l, fw2, tm=tm, tf=256,
                  mixer_out=(o_mla, o_lin, mw["w_out_mla"], mw["w_out_lin"], mw["mix_post_g"]))
    return xf.reshape(batch, seq, d)
```

```python
import functools
import math

import jax
import jax.numpy as jnp
from jax import lax
from jax.experimental import pallas as pl
from jax.experimental.pallas import tpu as pltpu

F32 = jnp.float32
BF16 = jnp.bfloat16

D_MODEL = 1024
D_FF = 2816
CHUNK = 64
EPS = 1e-6
ROPE_BASE = 10000.0
ROPE_DIM = 32
HALF_ROPE = ROPE_DIM // 2

MLA_HEADS = 8
MLA_Q_RANK = 256
MLA_KV_RANK = 128
MLA_NOPE = 64
MLA_V = 64
GLA_HEADS = 4
GLA_DK = 32
GLA_DV = 64
GLA_GATE_RANK = 16
GLA_TAU = 16.0
RET_HEADS = 4
RET_DK = 32
RET_DV = 64

LANES = 128
MLA_WIDTH = MLA_HEADS * MLA_V
GLA_WIDTH = GLA_HEADS * GLA_DV
RET_WIDTH = RET_HEADS * RET_DV
QK_WIDTH = GLA_HEADS * GLA_DK
HEAD_PAD = MLA_HEADS * LANES

C_MQ = 0
C_MKV = C_MQ + MLA_Q_RANK
C_KPE = C_MKV + MLA_KV_RANK
C_GQ = C_KPE + LANES
C_GK = C_GQ + QK_WIDTH
C_GV = C_GK + QK_WIDTH
C_GR = C_GV + GLA_WIDTH
C_RQ = C_GR + GLA_WIDTH
C_RK = C_RQ + QK_WIDTH
C_RV = C_RK + QK_WIDTH
C_RG = C_RV + RET_WIDTH
D_IN_EXT = C_RG + RET_WIDTH

LOG2_E = math.log2(math.e)
LOOKAHEAD = 3
KV_UNROLL = 2
V_ROWS = 80
VMEM_LIMIT = 56 * 1024 * 1024


def _rms(x, g):
    return x * lax.rsqrt(jnp.mean(x * x, axis=-1, keepdims=True) + EPS) * g


def _pack_rows(x):
    return pltpu.bitcast(x, jnp.uint32)


def _unpack_rows(x):
    return pltpu.bitcast(x, BF16)


def _silu(x):
    return x * jax.nn.sigmoid(x)


def _cparams(*sem):
    return pltpu.CompilerParams(dimension_semantics=sem, vmem_limit_bytes=VMEM_LIMIT)


def _const_spec(shape):
    nd = len(shape)
    return pl.BlockSpec(shape, lambda *_: (0,) * nd, pipeline_mode=pl.Buffered(1))


def _layer_spec(stacked, layer):
    nd = stacked.ndim
    return pl.BlockSpec((None,) + stacked.shape[1:], lambda *_: (layer,) + (0,) * (nd - 1),
                        pipeline_mode=pl.Buffered(1))


def _ffn_kernel(*refs, tf, with_mixer_out):
    if with_mixer_out:
        x_ref, om_ref, ol_ref, wa_ref, wb_ref, mpost_ref = refs[:6]
        refs = refs[6:]
        m = (lax.dot_general(om_ref[...], wa_ref[...], (((0,), (0,)), ((), ())), preferred_element_type=F32)
             + jnp.dot(ol_ref[...], wb_ref[...], preferred_element_type=F32))
        x = x_ref[...] + _rms(m, mpost_ref[...])
    else:
        x_ref = refs[0]
        refs = refs[1:]
        x = x_ref[...]
    pre_ref, post_ref, wg_ref, wu_ref, wd_ref, o_ref, h_ref = refs
    xn = _rms(x, pre_ref[...]).astype(BF16)
    for c in range(D_FF // tf):
        sl = slice(c * tf, (c + 1) * tf)
        g = jnp.dot(xn, wg_ref[:, sl].astype(BF16), preferred_element_type=F32)
        u = jnp.dot(xn, wu_ref[:, sl].astype(BF16), preferred_element_type=F32)
        h_ref[:, sl] = (_silu(g) * u).astype(BF16)
    f = jnp.dot(h_ref[...], wd_ref[...].astype(BF16), preferred_element_type=F32)
    o_ref[...] = x + 0.5 * _rms(f, post_ref[...])


def _ffn(x, layer, fw, *, tm, tf, mixer_out=None):
    t = x.shape[0]
    row = lambda w: pl.BlockSpec((tm, w), lambda i: (i, 0))
    args, specs = [x], [row(D_MODEL)]
    if mixer_out is not None:
        o_mla, o_lin, wa, wb, mpost = mixer_out
        args += [o_mla, o_lin, wa, wb, mpost]
        specs += [pl.BlockSpec((MLA_WIDTH, tm), lambda i: (0, i)), row(GLA_WIDTH + RET_WIDTH)]
        specs += [_layer_spec(a, layer) for a in (wa, wb, mpost)]
    args += list(fw)
    specs += [_layer_spec(a, layer) for a in fw]
    return pl.pallas_call(
        functools.partial(_ffn_kernel, tf=tf, with_mixer_out=mixer_out is not None),
        out_shape=jax.ShapeDtypeStruct(x.shape, F32),
        grid=(t // tm,),
        in_specs=specs,
        out_specs=row(D_MODEL),
        scratch_shapes=[pltpu.VMEM((tm, D_FF), BF16)],
        compiler_params=_cparams("parallel"),
        name="ffn_mixer_out" if mixer_out is not None else "ffn",
    )(*args)


def _rope_kernel(pos_ref, inv_ref, cos_ref, sin_ref):
    ang = pos_ref[...].astype(F32) * inv_ref[...]
    cos_ref[...] = jnp.cos(ang)
    sin_ref[...] = jnp.sin(ang)


def _rope_tables(pos_col, inv_lane, *, tm):
    t = pos_col.shape[0]
    return pl.pallas_call(
        _rope_kernel,
        out_shape=(jax.ShapeDtypeStruct((t, LANES), F32),) * 2,
        grid=(t // tm,),
        in_specs=[pl.BlockSpec((tm, 1), lambda i: (i, 0)), _const_spec((1, LANES))],
        out_specs=(pl.BlockSpec((tm, LANES), lambda i: (i, 0)),) * 2,
        compiler_params=_cparams("parallel"),
        name="rope_tables",
    )(pos_col, inv_lane)


def _proj_kernel(x_ref, cos_ref, sin_ref, pre_ref, win_ref, qg_ref, wq_ref, kvg_ref, wk_ref, wvt_ref,
                 wgate_ref, bgate_ref, tri_ref,
                 q_ref, k_ref, gq_ref, gk_ref, b_ref, gv_ref, sg_ref,
                 rq_ref, rk_ref, rv_ref, sr_ref, vt_ref):
    tm = x_ref.shape[0]
    xn = _rms(x_ref[...], pre_ref[...]).astype(BF16)

    chunk_starts = (0, C_GQ, C_RQ, D_IN_EXT)
    chunks = [jnp.dot(xn, win_ref[:, a:b], preferred_element_type=F32)
              for a, b in zip(chunk_starts[:-1], chunk_starts[1:])]

    def proj(c0, width):
        for a, b, chunk in zip(chunk_starts[:-1], chunk_starts[1:], chunks):
            if a <= c0 and c0 + width <= b:
                return chunk[:, c0 - a:c0 - a + width]
        raise ValueError("column group straddles two chunks")

    cosb = cos_ref[...]
    lane = lax.broadcasted_iota(jnp.int32, (tm, LANES), 1)
    first_half = (lane % ROPE_DIM) < HALF_ROPE
    sins = jnp.where(first_half, -sin_ref[...], sin_ref[...])

    def rot_half(v):
        return jnp.where(first_half, pltpu.roll(v, LANES - HALF_ROPE, axis=1), pltpu.roll(v, HALF_ROPE, axis=1))

    hq = _rms(proj(C_MQ, MLA_Q_RANK), qg_ref[...]).astype(BF16)
    in_rope = (lane >= MLA_NOPE) & (lane < MLA_NOPE + ROPE_DIM)
    qcos = jnp.where(in_rope, cosb, 1.0)
    qsin = jnp.where(in_rope, sins, 0.0)
    q_scale = (MLA_NOPE + ROPE_DIM) ** -0.5 * LOG2_E
    qa = jnp.dot(hq, wq_ref[...], preferred_element_type=F32)
    for h in range(MLA_HEADS):
        sl = slice(h * LANES, (h + 1) * LANES)
        q_ref[:, sl] = _pack_rows(((qa[:, sl] * qcos + rot_half(qa[:, sl]) * qsin) * q_scale).astype(BF16))

    hkv = _rms(proj(C_MKV, MLA_KV_RANK), kvg_ref[...]).astype(BF16)
    g1 = proj(C_KPE, LANES)
    kpe = jnp.where(in_rope, g1 * cosb + rot_half(g1) * sins, 0.0)
    ones_row = jnp.where(lax.broadcasted_iota(jnp.int32, (V_ROWS, tm), 0) == MLA_V, 1.0, 0.0)
    kn = jnp.dot(hkv, wk_ref[...], preferred_element_type=F32)
    vt = lax.dot_general(wvt_ref[...], hkv, (((1,), (1,)), ((), ())), preferred_element_type=F32)
    for h in range(MLA_HEADS):
        sl = slice(h * LANES, (h + 1) * LANES)
        vs = slice(h * V_ROWS, (h + 1) * V_ROWS)
        k_ref[:, sl] = _pack_rows((kn[:, sl] + kpe).astype(BF16))
        vt_ref[h * V_ROWS // 2:(h + 1) * V_ROWS // 2, :] = _pack_rows((vt[vs, :] + ones_row).astype(BF16))

    z = jnp.dot(g1.astype(BF16), wgate_ref[...], preferred_element_type=F32) + bgate_ref[...]
    log_a = (jnp.minimum(z, 0.0) - jnp.log1p(jnp.exp(-jnp.abs(z)))) / GLA_TAU
    la_hi = log_a.astype(BF16)
    la_lo = (log_a - la_hi.astype(F32)).astype(BF16)
    la_split = jnp.concatenate([la_hi, la_lo], axis=1)
    tri = tri_ref[...]
    tb = tri.shape[0]
    for r in range(tm // tb):
        rs = slice(r * tb, (r + 1) * tb)
        cum = jnp.dot(tri, la_split[rs], preferred_element_type=F32)
        b_ref[rs, :] = cum[:, :QK_WIDTH] + cum[:, QK_WIDTH:]
    gq_ref[...] = (proj(C_GQ, QK_WIDTH) * (GLA_DK ** -0.5)).astype(BF16)
    gk_ref[...] = proj(C_GK, QK_WIDTH).astype(BF16)
    gv_ref[...] = proj(C_GV, GLA_WIDTH).astype(BF16)
    sg_ref[...] = _silu(proj(C_GR, GLA_WIDTH)).astype(BF16)

    rq = proj(C_RQ, QK_WIDTH)
    rk = proj(C_RK, QK_WIDTH)
    rq_ref[...] = (rq * cosb + rot_half(rq) * sins).astype(BF16)
    rk_ref[...] = ((rk * cosb + rot_half(rk) * sins) * (RET_DK ** -0.5)).astype(BF16)
    rv_ref[...] = proj(C_RV, RET_WIDTH).astype(BF16)
    sr_ref[...] = _silu(proj(C_RG, RET_WIDTH)).astype(BF16)


def _proj(x, cosb, sinb, layer, mw, tri, *, tm):
    t = x.shape[0]
    row = lambda w: pl.BlockSpec((tm, w), lambda i: (i, 0))
    out_widths = [(QK_WIDTH, BF16), (QK_WIDTH, BF16), (QK_WIDTH, F32), (GLA_WIDTH, BF16), (GLA_WIDTH, BF16),
                  (QK_WIDTH, BF16), (QK_WIDTH, BF16), (RET_WIDTH, BF16), (RET_WIDTH, BF16)]
    stacked = [mw["mix_pre_g"], mw["w_in"], mw["q_norm_g"], mw["wq"], mw["kv_norm_g"], mw["wk"], mw["wvt"],
               mw["w_gate"], mw["b_gate"]]
    return pl.pallas_call(
        _proj_kernel,
        out_shape=(jax.ShapeDtypeStruct((t // 2, HEAD_PAD), jnp.uint32),) * 2
        + tuple(jax.ShapeDtypeStruct((t, w), dt) for w, dt in out_widths)
        + (jax.ShapeDtypeStruct((MLA_HEADS * V_ROWS // 2, t), jnp.uint32),),
        grid=(t // tm,),
        in_specs=[row(D_MODEL), row(LANES), row(LANES)] + [_layer_spec(a, layer) for a in stacked]
        + [_const_spec(tri.shape)],
        out_specs=(pl.BlockSpec((tm // 2, HEAD_PAD), lambda i: (i, 0)),) * 2 + tuple(row(w) for w, _ in out_widths)
        + (pl.BlockSpec((MLA_HEADS * V_ROWS // 2, tm), lambda i: (0, i)),),
        compiler_params=_cparams("parallel"),
        name="mixer_proj",
    )(x, cosb, sinb, *stacked, tri)


def _attn_kernel(q_ref, k_ref, vt_ref, o_ref, m_ref, acc_ref, st_ref, *, tk, tn):
    tq = 2 * q_ref.shape[0]
    qi = pl.program_id(1)
    nt = (((1,), (1,)), ((), ()))
    items = [(h, c) for h in range(MLA_HEADS) for c in range(tq // tn)]

    def key_rows(j, n):
        return pl.ds(pl.multiple_of(j * tk, tk), n)

    def packed_key_rows(j, n):
        return pl.ds(pl.multiple_of(j * (tk // 2), tk // 2), n // 2)

    def n_keys(item, diag):
        return (item[1] + 1) * tn if diag else tk

    def scores(j, item, diag):
        h, c = item
        hs = slice(h * LANES, (h + 1) * LANES)
        keys = _unpack_rows(k_ref[packed_key_rows(j, n_keys(item, diag)), hs])
        queries = _unpack_rows(q_ref[c * tn // 2:(c + 1) * tn // 2, hs])
        return lax.dot_general(keys, queries, nt, preferred_element_type=F32).astype(BF16)

    def chunk_mask(c):
        nk = (c + 1) * tn
        key_chunk = lax.broadcasted_iota(jnp.int32, (nk, tn), 0) // CHUNK
        qry_chunk = (lax.broadcasted_iota(jnp.int32, (nk, tn), 1) + c * tn) // CHUNK
        return jnp.where(key_chunk <= qry_chunk, 0.0, -jnp.inf).astype(BF16)

    diag_masks = [chunk_mask(c) for c in range(tq // tn)]

    def consume(j, item, st, diag):
        h, c = item
        vs = slice(h * V_ROWS // 2, (h + 1) * V_ROWS // 2)
        cs = slice(c * tn, (c + 1) * tn)
        nk = n_keys(item, diag)
        st = st[:nk]
        if diag:
            st = st + diag_masks[c]
        m_old = m_ref[h, :, cs]
        m_new = jnp.maximum(m_old, jnp.max(st, axis=0, keepdims=True).astype(F32))
        p = jnp.exp2(st - m_new.astype(BF16))
        acc_ref[h, :, cs] = (acc_ref[h, :, cs] * jnp.exp2(m_old - m_new)
                             + jnp.dot(_unpack_rows(vt_ref[vs, key_rows(j, nk)]), p, preferred_element_type=F32))
        m_ref[h, :, cs] = m_new

    for h in range(MLA_HEADS):
        m_ref[h] = jnp.full((1, tq), -jnp.inf, F32)
        acc_ref[h] = jnp.zeros((V_ROWS, tq), F32)

    for a in range(LOOKAHEAD):
        st_ref[a] = scores(0, items[a], False)

    def sweep(j, diag):
        pending = [st_ref[a] for a in range(LOOKAHEAD)]
        for i, item in enumerate(items):
            ahead = i + LOOKAHEAD
            if ahead < len(items):
                pending.append(scores(j, items[ahead], diag))
            elif not diag:
                pending.append(scores(j + 1, items[ahead - len(items)], False))
            consume(j, item, pending.pop(0), diag)
        for a, st in enumerate(pending):
            st_ref[a] = st

    def body(jj, carry):
        for u in range(KV_UNROLL):
            sweep(jj * KV_UNROLL + u, False)
        return carry

    n_full = qi // KV_UNROLL
    lax.fori_loop(0, n_full, body, 0)
    lax.fori_loop(n_full * KV_UNROLL, qi, lambda j, c: (sweep(j, False), c)[1], 0)
    sweep(qi, True)

    for h in range(MLA_HEADS):
        acc = acc_ref[h]
        o_ref[h * MLA_V:(h + 1) * MLA_V, :] = (acc[:MLA_V] / acc[MLA_V:MLA_V + 1, :]).astype(BF16)


def _attention(q, k, vt, *, batch, seq, tq, tk, tn):
    nq = seq // tq
    assert tq == tk and tq % tn == 0
    return pl.pallas_call(
        functools.partial(_attn_kernel, tk=tk, tn=tn),
        out_shape=jax.ShapeDtypeStruct((MLA_WIDTH, batch * seq), BF16),
        grid=(batch, nq),
        in_specs=[
            pl.BlockSpec((tq // 2, HEAD_PAD), lambda b, i: (b * nq + i, 0)),
            pl.BlockSpec((seq // 2, HEAD_PAD), lambda b, i: (b, 0)),
            pl.BlockSpec((MLA_HEADS * V_ROWS // 2, seq), lambda b, i: (0, b)),
        ],
        out_specs=pl.BlockSpec((MLA_WIDTH, tq), lambda b, i: (0, b * nq + i)),
        scratch_shapes=[pltpu.VMEM((MLA_HEADS, 1, tq), F32), pltpu.VMEM((MLA_HEADS, V_ROWS, tq), F32),
                        pltpu.VMEM((LOOKAHEAD, tk, tn), BF16)],
        compiler_params=_cparams("parallel", "arbitrary"),
        name="mla_attention",
    )(q, k, vt)


def _head_rows(x, n_heads, head_lanes):
    lane_head = lax.broadcasted_iota(jnp.int32, x.shape, 1) // head_lanes
    return jnp.concatenate([jnp.where(lane_head == h, x, 0.0).astype(BF16) for h in range(n_heads)], axis=0)


def _linmix_kernel(gq_ref, gk_ref, b_ref, gv_ref, sg_ref, rq_ref, rk_ref, rv_ref, sr_ref,
                   gng_ref, rng_ref, lgam_ref, ones_ref, o_ref, sg_state, sr_state, og_ref, or_ref):
    tc = gq_ref.shape[0]
    L = CHUNK

    @pl.when(pl.program_id(1) == 0)
    def _():
        sg_state[...] = jnp.zeros(sg_state.shape, F32)
        sr_state[...] = jnp.zeros(sr_state.shape, F32)

    ri = lax.broadcasted_iota(jnp.int32, (L, GLA_HEADS * L), 0)
    cj = lax.broadcasted_iota(jnp.int32, (L, GLA_HEADS * L), 1) % L
    lower = cj <= ri
    lgam = lgam_ref[...]
    lgam_cols = jnp.concatenate(
        [jnp.broadcast_to(lgam[:, h * RET_DK:h * RET_DK + 1], (1, L)) for h in range(RET_HEADS)], axis=1)
    ret_decay = jnp.exp(lgam_cols * jnp.abs(ri - cj).astype(F32))
    pos = lax.broadcasted_iota(jnp.int32, (L, QK_WIDTH), 0).astype(F32)
    q_dec = jnp.exp(lgam * (pos + 1.0))
    k_dec = jnp.exp(lgam * (L - 1.0 - pos))
    chunk_dec = jnp.exp(lgam * float(L))
    state_head = lax.broadcasted_iota(jnp.int32, (GLA_WIDTH, QK_WIDTH), 0) // GLA_DV
    state_lane_head = lax.broadcasted_iota(jnp.int32, (GLA_WIDTH, QK_WIDTH), 1) // GLA_DK
    state_diag = state_head == state_lane_head
    nt = (((1,), (1,)), ((), ()))

    for c in range(tc // L):
        rs = slice(c * L, (c + 1) * L)
        b = b_ref[rs, :]
        b_last = b[L - 1:L, :]
        eb, enb = jnp.exp(b), jnp.exp(-b)
        q = gq_ref[rs, :].astype(F32)
        k = gk_ref[rs, :].astype(F32)
        v = gv_ref[rs, :].astype(F32)
        q_fwd = (q * eb).astype(BF16)
        a_low = lax.dot_general(q_fwd, _head_rows(k * enb, GLA_HEADS, GLA_DK), nt, preferred_element_type=F32)
        a_up = lax.dot_general((q * enb).astype(BF16), _head_rows(k * eb, GLA_HEADS, GLA_DK), nt,
                               preferred_element_type=F32)
        attn = jnp.where(lower, a_low, a_up).astype(BF16)
        o_intra = jnp.dot(attn, _head_rows(v, GLA_HEADS, GLA_DV), preferred_element_type=F32)
        s_t = sg_state[...]
        o_inter = lax.dot_general(q_fwd, s_t.astype(BF16), nt, preferred_element_type=F32)
        og_ref[rs, :] = o_intra + o_inter
        k_state = (k * jnp.exp(b_last - b)).astype(BF16)
        d_state = jnp.dot(v.T.astype(BF16), k_state, preferred_element_type=F32)
        sg_state[...] = s_t * jnp.exp(b_last) + jnp.where(state_diag, d_state, 0.0)

        q = rq_ref[rs, :].astype(F32)
        k = rk_ref[rs, :].astype(F32)
        v = rv_ref[rs, :].astype(F32)
        scores = lax.dot_general(q.astype(BF16), _head_rows(k, RET_HEADS, RET_DK), nt,
                                 preferred_element_type=F32) * ret_decay
        o_intra = jnp.dot(scores.astype(BF16), _head_rows(v, RET_HEADS, RET_DV), preferred_element_type=F32)
        s_t = sr_state[...]
        o_inter = lax.dot_general((q * q_dec).astype(BF16), s_t.astype(BF16), nt, preferred_element_type=F32)
        or_ref[rs, :] = o_intra + o_inter
        d_state = jnp.dot(v.T.astype(BF16), (k * k_dec).astype(BF16), preferred_element_type=F32)
        sr_state[...] = s_t * chunk_dec + jnp.where(state_diag, d_state, 0.0)

    def head_norm(o, g, gate):
        o2 = o * o
        hi = o2.astype(BF16)
        lo = (o2 - hi.astype(F32)).astype(BF16)
        ss = (jnp.dot(hi, ones_ref[...], preferred_element_type=F32)
              + jnp.dot(lo, ones_ref[...], preferred_element_type=F32))
        return o * lax.rsqrt(ss * (1.0 / GLA_DV) + EPS) * g * gate

    o_ref[:, :GLA_WIDTH] = head_norm(og_ref[...], gng_ref[...], sg_ref[...].astype(F32)).astype(BF16)
    o_ref[:, GLA_WIDTH:] = head_norm(or_ref[...], rng_ref[...], sr_ref[...].astype(F32)).astype(BF16)


def _linmix(gq, gk, b, gv, sg, rq, rk, rv, sr, layer, gng, rng, lgam, ones_bd, *, batch, seq, tc):
    nt = seq // tc
    row = lambda w: pl.BlockSpec((tc, w), lambda bb, i: (bb * nt + i, 0))
    return pl.pallas_call(
        _linmix_kernel,
        out_shape=jax.ShapeDtypeStruct((batch * seq, GLA_WIDTH + RET_WIDTH), BF16),
        grid=(batch, nt),
        in_specs=[row(QK_WIDTH), row(QK_WIDTH), row(QK_WIDTH), row(GLA_WIDTH), row(GLA_WIDTH),
                  row(QK_WIDTH), row(QK_WIDTH), row(RET_WIDTH), row(RET_WIDTH),
                  _layer_spec(gng, layer), _layer_spec(rng, layer), _const_spec(lgam.shape),
                  _const_spec(ones_bd.shape)],
        out_specs=row(GLA_WIDTH + RET_WIDTH),
        scratch_shapes=[pltpu.VMEM((GLA_WIDTH, QK_WIDTH), F32), pltpu.VMEM((RET_WIDTH, QK_WIDTH), F32),
                        pltpu.VMEM((tc, GLA_WIDTH), F32), pltpu.VMEM((tc, RET_WIDTH), F32)],
        compiler_params=_cparams("parallel", "arbitrary"),
        name="gla_retention",
    )(gq, gk, b, gv, sg, rq, rk, rv, sr, gng, rng, lgam, ones_bd)


def _prep_mixer_layer(p):
    w_in = p["w_in"]
    splits = (MLA_Q_RANK, MLA_KV_RANK, ROPE_DIM, QK_WIDTH, QK_WIDTH, GLA_WIDTH, GLA_GATE_RANK, GLA_WIDTH,
              QK_WIDTH, QK_WIDTH, RET_WIDTH, RET_WIDTH)
    offs = [0]
    for s in splits:
        offs.append(offs[-1] + s)
    mq, mkv, mkr, gq, gk, gv, gg, gr, rq, rk, rv, rg = (w_in[:, offs[i]:offs[i + 1]] for i in range(12))
    d = w_in.shape[0]
    zeros = lambda n: jnp.zeros((d, n), F32)
    kpe_grp = jnp.concatenate([gg, zeros(MLA_NOPE - GLA_GATE_RANK), mkr, zeros(LANES - MLA_NOPE - ROPE_DIM)], 1)
    w_in_ext = jnp.concatenate([mq, mkv, kpe_grp, gq, gk, gv, gr, rq, rk, rv, rg], axis=1).astype(BF16)

    wq = p["mla_w_q_up"].reshape(MLA_Q_RANK, MLA_HEADS, MLA_NOPE + ROPE_DIM)
    pad_q = jnp.zeros((MLA_Q_RANK, MLA_HEADS, LANES - MLA_NOPE - ROPE_DIM), F32)
    wq_main = jnp.concatenate([wq, pad_q], axis=-1).reshape(MLA_Q_RANK, HEAD_PAD)
    wkv = p["mla_w_kv_up"].reshape(MLA_KV_RANK, MLA_HEADS, MLA_NOPE + MLA_V)
    pad_kv = jnp.zeros((MLA_KV_RANK, MLA_HEADS, LANES - MLA_NOPE), F32)
    wk_pad = jnp.concatenate([wkv[..., :MLA_NOPE], pad_kv], axis=-1).reshape(MLA_KV_RANK, HEAD_PAD)
    pad_v = jnp.zeros((MLA_KV_RANK, MLA_HEADS, V_ROWS - MLA_V), F32)
    wv_pad = jnp.concatenate([wkv[..., MLA_NOPE:], pad_v], axis=-1).reshape(MLA_KV_RANK, MLA_HEADS * V_ROWS)
    w_gate = jnp.concatenate([p["gla_w_gate_up"], jnp.zeros((LANES - GLA_GATE_RANK, QK_WIDTH), F32)], axis=0)
    row = lambda a: a.reshape(1, -1)
    w_out = p["w_out"].astype(BF16)
    return {
        "mix_pre_g": row(p["mix_pre_g"]), "mix_post_g": row(p["mix_post_g"]),
        "w_in": w_in_ext,
        "q_norm_g": row(p["mla_q_norm_g"]),
        "wq": wq_main.astype(BF16),
        "kv_norm_g": row(p["mla_kv_norm_g"]),
        "wk": wk_pad.astype(BF16), "wvt": wv_pad.T.astype(BF16),
        "w_gate": w_gate.astype(BF16), "b_gate": row(p["gla_b_gate"]),
        "gla_norm_g": row(jnp.tile(p["gla_norm_g"], GLA_HEADS)),
        "ret_norm_g": row(jnp.tile(p["ret_norm_g"], RET_HEADS)),
        "w_out_mla": w_out[:MLA_WIDTH], "w_out_lin": w_out[MLA_WIDTH:],
    }


def _ffn_params(p, prefix):
    gain = lambda g: g.reshape(g.shape[0], 1, g.shape[1])
    return (gain(p[prefix + "_pre_g"]), gain(p[prefix + "_post_g"]),
            p[prefix + "_w_gate"], p[prefix + "_w_up"], p[prefix + "_w_down"])


def _tile(n, pref):
    while n % pref:
        pref //= 2
    return pref


def kernel(x, positions, ffn1_pre_g, ffn1_post_g, ffn1_w_gate, ffn1_w_up, ffn1_w_down, mix_pre_g, mix_post_g, w_in, mla_q_norm_g, mla_w_q_up, mla_kv_norm_g, mla_w_kv_up, gla_w_gate_up, gla_b_gate, gla_norm_g, ret_norm_g, w_out, ffn2_pre_g, ffn2_post_g, ffn2_w_gate, ffn2_w_up, ffn2_w_down):
    p = dict(ffn1_pre_g=ffn1_pre_g, ffn1_post_g=ffn1_post_g, ffn1_w_gate=ffn1_w_gate, ffn1_w_up=ffn1_w_up,
             ffn1_w_down=ffn1_w_down, ffn2_pre_g=ffn2_pre_g, ffn2_post_g=ffn2_post_g, ffn2_w_gate=ffn2_w_gate,
             ffn2_w_up=ffn2_w_up, ffn2_w_down=ffn2_w_down)
    mixer = dict(mix_pre_g=mix_pre_g, mix_post_g=mix_post_g, w_in=w_in, mla_q_norm_g=mla_q_norm_g,
                 mla_w_q_up=mla_w_q_up, mla_kv_norm_g=mla_kv_norm_g, mla_w_kv_up=mla_w_kv_up,
                 gla_w_gate_up=gla_w_gate_up, gla_b_gate=gla_b_gate, gla_norm_g=gla_norm_g,
                 ret_norm_g=ret_norm_g, w_out=w_out)
    batch, seq, d = x.shape
    t = batch * seq
    depth = w_in.shape[0]
    tm = _tile(seq, 512)
    tq = _tile(seq, 512)
    tc = _tile(seq, 512)

    inv = ROPE_BASE ** (-jnp.arange(0, ROPE_DIM, 2, dtype=F32) / ROPE_DIM)
    inv_lane = jnp.tile(inv, LANES // HALF_ROPE).reshape(1, LANES)
    tb = _tile(tm, 256)
    idx = jnp.arange(tb)
    tri = ((idx[:, None] >= idx[None, :]) & (idx[:, None] // CHUNK == idx[None, :] // CHUNK)).astype(BF16)
    hd = jnp.arange(GLA_WIDTH) // GLA_DV
    ones_bd = (hd[:, None] == hd[None, :]).astype(BF16)
    log_gamma = jnp.log1p(-jnp.exp2(-5.0 - jnp.arange(RET_HEADS, dtype=F32)))
    lgam = jnp.repeat(log_gamma, RET_DK).reshape(1, QK_WIDTH)

    fw1 = _ffn_params(p, "ffn1")
    fw2 = _ffn_params(p, "ffn2")
    mw = jax.vmap(_prep_mixer_layer)(mixer)
    cosb, sinb = _rope_tables(positions.reshape(t, 1), inv_lane, tm=tm)

    xf = x.reshape(t, d)
    for l in range(depth):
        xf = _ffn(xf, l, fw1, tm=tm, tf=256)
        (q, k, gq, gk, b, gv, sg, rq, rk, rv, sr, vt) = _proj(xf, cosb, sinb, l, mw, tri, tm=tm)
        o_mla = _attention(q, k, vt, batch=batch, seq=seq, tq=tq, tk=tq, tn=_tile(tq, 256))
        o_lin = _linmix(gq, gk, b, gv, sg, rq, rk, rv, sr, l, mw["gla_norm_g"], mw["ret_norm_g"], lgam, ones_bd,
                        batch=batch, seq=seq, tc=tc)
        xf = _ffn(xf, l, fw2, tm=tm, tf=256,
                  mixer_out=(o_mla, o_lin, mw["w_out_mla"], mw["w_out_lin"], mw["mix_post_g"]))
    return xf.reshape(batch, seq, d)
```

```python
import functools
import math

import jax
import jax.numpy as jnp
from jax import lax
from jax.experimental import pallas as pl
from jax.experimental.pallas import tpu as pltpu

F32 = jnp.float32
BF16 = jnp.bfloat16

D_MODEL = 1024
D_FF = 2816
CHUNK = 64
EPS = 1e-6
ROPE_BASE = 10000.0
ROPE_DIM = 32
HALF_ROPE = ROPE_DIM // 2

MLA_HEADS = 8
MLA_Q_RANK = 256
MLA_KV_RANK = 128
MLA_NOPE = 64
MLA_V = 64
GLA_HEADS = 4
GLA_DK = 32
GLA_DV = 64
GLA_GATE_RANK = 16
GLA_TAU = 16.0
RET_HEADS = 4
RET_DK = 32
RET_DV = 64

LANES = 128
MLA_WIDTH = MLA_HEADS * MLA_V
GLA_WIDTH = GLA_HEADS * GLA_DV
RET_WIDTH = RET_HEADS * RET_DV
QK_WIDTH = GLA_HEADS * GLA_DK
HEAD_PAD = MLA_HEADS * LANES

C_MQ = 0
C_MKV = C_MQ + MLA_Q_RANK
C_KPE = C_MKV + MLA_KV_RANK
C_GQ = C_KPE + LANES
C_GK = C_GQ + QK_WIDTH
C_GV = C_GK + QK_WIDTH
C_GR = C_GV + GLA_WIDTH
C_RQ = C_GR + GLA_WIDTH
C_RK = C_RQ + QK_WIDTH
C_RV = C_RK + QK_WIDTH
C_RG = C_RV + RET_WIDTH
D_IN_EXT = C_RG + RET_WIDTH

LOG2_E = math.log2(math.e)
LOOKAHEAD = 3
KV_UNROLL = 1
V_ROWS = 80
VMEM_LIMIT = 56 * 1024 * 1024


def _rms(x, g):
    return x * lax.rsqrt(jnp.mean(x * x, axis=-1, keepdims=True) + EPS) * g


def _pack_rows(x):
    return pltpu.bitcast(x, jnp.uint32)


def _unpack_rows(x):
    return pltpu.bitcast(x, BF16)


def _silu(x):
    return x * jax.nn.sigmoid(x)


def _cparams(*sem):
    return pltpu.CompilerParams(dimension_semantics=sem, vmem_limit_bytes=VMEM_LIMIT)


def _const_spec(shape):
    nd = len(shape)
    return pl.BlockSpec(shape, lambda *_: (0,) * nd, pipeline_mode=pl.Buffered(1))


def _layer_spec(stacked, layer):
    nd = stacked.ndim
    return pl.BlockSpec((None,) + stacked.shape[1:], lambda *_: (layer,) + (0,) * (nd - 1),
                        pipeline_mode=pl.Buffered(1))


def _ffn_kernel(*refs, tf, with_mixer_out):
    if with_mixer_out:
        x_ref, om_ref, ol_ref, wa_ref, wb_ref, mpost_ref = refs[:6]
        refs = refs[6:]
        m = (lax.dot_general(om_ref[...], wa_ref[...], (((0,), (0,)), ((), ())), preferred_element_type=F32)
             + jnp.dot(ol_ref[...], wb_ref[...], preferred_element_type=F32))
        x = x_ref[...] + _rms(m, mpost_ref[...])
    else:
        x_ref = refs[0]
        refs = refs[1:]
        x = x_ref[...]
    pre_ref, post_ref, wg_ref, wu_ref, wd_ref, o_ref, h_ref = refs
    xn = _rms(x, pre_ref[...]).astype(BF16)
    for c in range(D_FF // tf):
        sl = slice(c * tf, (c + 1) * tf)
        g = jnp.dot(xn, wg_ref[:, sl].astype(BF16), preferred_element_type=F32)
        u = jnp.dot(xn, wu_ref[:, sl].astype(BF16), preferred_element_type=F32)
        h_ref[:, sl] = (_silu(g) * u).astype(BF16)
    f = jnp.dot(h_ref[...], wd_ref[...].astype(BF16), preferred_element_type=F32)
    o_ref[...] = x + 0.5 * _rms(f, post_ref[...])


def _ffn(x, layer, fw, *, tm, tf, mixer_out=None):
    t = x.shape[0]
    row = lambda w: pl.BlockSpec((tm, w), lambda i: (i, 0))
    args, specs = [x], [row(D_MODEL)]
    if mixer_out is not None:
        o_mla, o_lin, wa, wb, mpost = mixer_out
        args += [o_mla, o_lin, wa, wb, mpost]
        specs += [pl.BlockSpec((MLA_WIDTH, tm), lambda i: (0, i)), row(GLA_WIDTH + RET_WIDTH)]
        specs += [_layer_spec(a, layer) for a in (wa, wb, mpost)]
    args += list(fw)
    specs += [_layer_spec(a, layer) for a in fw]
    return pl.pallas_call(
        functools.partial(_ffn_kernel, tf=tf, with_mixer_out=mixer_out is not None),
        out_shape=jax.ShapeDtypeStruct(x.shape, F32),
        grid=(t // tm,),
        in_specs=specs,
        out_specs=row(D_MODEL),
        scratch_shapes=[pltpu.VMEM((tm, D_FF), BF16)],
        compiler_params=_cparams("parallel"),
        name="ffn_mixer_out" if mixer_out is not None else "ffn",
    )(*args)


def _rope_kernel(pos_ref, inv_ref, cos_ref, sin_ref):
    ang = pos_ref[...].astype(F32) * inv_ref[...]
    cos_ref[...] = jnp.cos(ang)
    sin_ref[...] = jnp.sin(ang)


def _rope_tables(pos_lanes, inv_lane, *, tm):
    t = pos_lanes.shape[0]
    return pl.pallas_call(
        _rope_kernel,
        out_shape=(jax.ShapeDtypeStruct((t, LANES), F32),) * 2,
        grid=(t // tm,),
        in_specs=[pl.BlockSpec((tm, LANES), lambda i: (i, 0)), _const_spec((1, LANES))],
        out_specs=(pl.BlockSpec((tm, LANES), lambda i: (i, 0)),) * 2,
        compiler_params=_cparams("parallel"),
        name="rope_tables",
    )(pos_lanes, inv_lane)


def _proj_kernel(x_ref, cos_ref, sin_ref, pre_ref, win_ref, qg_ref, wq_ref, kvg_ref, wk_ref, wvt_ref,
                 wgate_ref, bgate_ref, tri_ref,
                 q_ref, k_ref, gq_ref, gk_ref, b_ref, gv_ref, sg_ref,
                 rq_ref, rk_ref, rv_ref, sr_ref, vt_ref):
    tm = x_ref.shape[0]
    xn = _rms(x_ref[...], pre_ref[...]).astype(BF16)

    chunk_starts = (0, C_GQ, C_RQ, D_IN_EXT)
    chunks = [jnp.dot(xn, win_ref[:, a:b], preferred_element_type=F32)
              for a, b in zip(chunk_starts[:-1], chunk_starts[1:])]

    def proj(c0, width):
        for a, b, chunk in zip(chunk_starts[:-1], chunk_starts[1:], chunks):
            if a <= c0 and c0 + width <= b:
                return chunk[:, c0 - a:c0 - a + width]
        raise ValueError("column group straddles two chunks")

    cosb = cos_ref[...]
    lane = lax.broadcasted_iota(jnp.int32, (tm, LANES), 1)
    first_half = (lane % ROPE_DIM) < HALF_ROPE
    sins = jnp.where(first_half, -sin_ref[...], sin_ref[...])

    def rot_half(v):
        return jnp.where(first_half, pltpu.roll(v, LANES - HALF_ROPE, axis=1), pltpu.roll(v, HALF_ROPE, axis=1))

    hq = _rms(proj(C_MQ, MLA_Q_RANK), qg_ref[...]).astype(BF16)
    in_rope = (lane >= MLA_NOPE) & (lane < MLA_NOPE + ROPE_DIM)
    qcos = jnp.where(in_rope, cosb, 1.0)
    qsin = jnp.where(in_rope, sins, 0.0)
    q_scale = (MLA_NOPE + ROPE_DIM) ** -0.5 * LOG2_E
    qa = jnp.dot(hq, wq_ref[...], preferred_element_type=F32)
    for h in range(MLA_HEADS):
        sl = slice(h * LANES, (h + 1) * LANES)
        q_ref[:, sl] = _pack_rows(((qa[:, sl] * qcos + rot_half(qa[:, sl]) * qsin) * q_scale).astype(BF16))

    hkv = _rms(proj(C_MKV, MLA_KV_RANK), kvg_ref[...]).astype(BF16)
    g1 = proj(C_KPE, LANES)
    kpe = jnp.where(in_rope, g1 * cosb + rot_half(g1) * sins, 0.0)
    ones_row = jnp.where(lax.broadcasted_iota(jnp.int32, (V_ROWS, tm), 0) == MLA_V, 1.0, 0.0)
    kn = jnp.dot(hkv, wk_ref[...], preferred_element_type=F32)
    vt = lax.dot_general(wvt_ref[...], hkv, (((1,), (1,)), ((), ())), preferred_element_type=F32)
    for h in range(MLA_HEADS):
        sl = slice(h * LANES, (h + 1) * LANES)
        vs = slice(h * V_ROWS, (h + 1) * V_ROWS)
        k_ref[:, sl] = _pack_rows((kn[:, sl] + kpe).astype(BF16))
        vt_ref[h * V_ROWS // 2:(h + 1) * V_ROWS // 2, :] = _pack_rows((vt[vs, :] + ones_row).astype(BF16))

    z = jnp.dot(g1.astype(BF16), wgate_ref[...], preferred_element_type=F32) + bgate_ref[...]
    log_a = (jnp.minimum(z, 0.0) - jnp.log1p(jnp.exp(-jnp.abs(z)))) / GLA_TAU
    la_hi = log_a.astype(BF16)
    la_lo = (log_a - la_hi.astype(F32)).astype(BF16)
    la_split = jnp.concatenate([la_hi, la_lo], axis=1)
    tri = tri_ref[...]
    tb = tri.shape[0]
    for r in range(tm // tb):
        rs = slice(r * tb, (r + 1) * tb)
        cum = jnp.dot(tri, la_split[rs], preferred_element_type=F32)
        b_ref[rs, :] = cum[:, :QK_WIDTH] + cum[:, QK_WIDTH:]
    gq_ref[...] = (proj(C_GQ, QK_WIDTH) * (GLA_DK ** -0.5)).astype(BF16)
    gk_ref[...] = proj(C_GK, QK_WIDTH).astype(BF16)
    gv_ref[...] = proj(C_GV, GLA_WIDTH).astype(BF16)
    sg_ref[...] = _silu(proj(C_GR, GLA_WIDTH)).astype(BF16)

    rq = proj(C_RQ, QK_WIDTH)
    rk = proj(C_RK, QK_WIDTH)
    rq_ref[...] = (rq * cosb + rot_half(rq) * sins).astype(BF16)
    rk_ref[...] = ((rk * cosb + rot_half(rk) * sins) * (RET_DK ** -0.5)).astype(BF16)
    rv_ref[...] = proj(C_RV, RET_WIDTH).astype(BF16)
    sr_ref[...] = _silu(proj(C_RG, RET_WIDTH)).astype(BF16)


def _proj(x, cosb, sinb, layer, mw, tri, *, tm):
    t = x.shape[0]
    row = lambda w: pl.BlockSpec((tm, w), lambda i: (i, 0))
    out_widths = [(QK_WIDTH, BF16), (QK_WIDTH, BF16), (QK_WIDTH, F32), (GLA_WIDTH, BF16), (GLA_WIDTH, BF16),
                  (QK_WIDTH, BF16), (QK_WIDTH, BF16), (RET_WIDTH, BF16), (RET_WIDTH, BF16)]
    stacked = [mw["mix_pre_g"], mw["w_in"], mw["q_norm_g"], mw["wq"], mw["kv_norm_g"], mw["wk"], mw["wvt"],
               mw["w_gate"], mw["b_gate"]]
    return pl.pallas_call(
        _proj_kernel,
        out_shape=(jax.ShapeDtypeStruct((t // 2, HEAD_PAD), jnp.uint32),) * 2
        + tuple(jax.ShapeDtypeStruct((t, w), dt) for w, dt in out_widths)
        + (jax.ShapeDtypeStruct((MLA_HEADS * V_ROWS // 2, t), jnp.uint32),),
        grid=(t // tm,),
        in_specs=[row(D_MODEL), row(LANES), row(LANES)] + [_layer_spec(a, layer) for a in stacked]
        + [_const_spec(tri.shape)],
        out_specs=(pl.BlockSpec((tm // 2, HEAD_PAD), lambda i: (i, 0)),) * 2 + tuple(row(w) for w, _ in out_widths)
        + (pl.BlockSpec((MLA_HEADS * V_ROWS // 2, tm), lambda i: (0, i)),),
        compiler_params=_cparams("parallel"),
        name="mixer_proj",
    )(x, cosb, sinb, *stacked, tri)


def _attn_kernel(q_ref, k_ref, vt_ref, o_ref, m_ref, acc_ref, st_ref, *, tk, tn):
    tq = 2 * q_ref.shape[0]
    qi = pl.program_id(1)
    nt = (((1,), (1,)), ((), ()))
    items = [(h, c) for h in range(MLA_HEADS) for c in range(tq // tn)]

    def key_rows(j, n):
        return pl.ds(pl.multiple_of(j * tk, tk), n)

    def packed_key_rows(j, n):
        return pl.ds(pl.multiple_of(j * (tk // 2), tk // 2), n // 2)

    def n_keys(item, diag):
        return (item[1] + 1) * tn if diag else tk

    def scores(j, item, diag):
        h, c = item
        hs = slice(h * LANES, (h + 1) * LANES)
        keys = _unpack_rows(k_ref[packed_key_rows(j, n_keys(item, diag)), hs])
        queries = _unpack_rows(q_ref[c * tn // 2:(c + 1) * tn // 2, hs])
        return lax.dot_general(keys, queries, nt, preferred_element_type=F32).astype(BF16)

    key_chunk = lax.broadcasted_iota(jnp.int32, (tn, tn), 0) // CHUNK
    qry_chunk = lax.broadcasted_iota(jnp.int32, (tn, tn), 1) // CHUNK
    diag_mask = jnp.where(key_chunk <= qry_chunk, 0.0, -jnp.inf).astype(BF16)

    def consume(j, item, st, diag):
        h, c = item
        vs = slice(h * V_ROWS // 2, (h + 1) * V_ROWS // 2)
        cs = slice(c * tn, (c + 1) * tn)
        nk = n_keys(item, diag)
        st = st[:nk]
        if diag:
            own = st[nk - tn:] + diag_mask
            st = own if nk == tn else jnp.concatenate([st[:nk - tn], own], axis=0)
        m_old = m_ref[h, :, cs]
        m_new = jnp.maximum(m_old, jnp.max(st, axis=0, keepdims=True).astype(F32))
        p = jnp.exp2(st - m_new.astype(BF16))
        acc_ref[h, :, cs] = (acc_ref[h, :, cs] * jnp.exp2(m_old - m_new)
                             + jnp.dot(_unpack_rows(vt_ref[vs, key_rows(j, nk)]), p, preferred_element_type=F32))
        m_ref[h, :, cs] = m_new

    for h in range(MLA_HEADS):
        m_ref[h] = jnp.full((1, tq), -jnp.inf, F32)
        acc_ref[h] = jnp.zeros((V_ROWS, tq), F32)

    for a in range(LOOKAHEAD):
        st_ref[a] = scores(0, items[a], False)

    def sweep(j, diag):
        pending = [st_ref[a] for a in range(LOOKAHEAD)]
        for i, item in enumerate(items):
            ahead = i + LOOKAHEAD
            if ahead < len(items):
                pending.append(scores(j, items[ahead], diag))
            elif not diag:
                pending.append(scores(j + 1, items[ahead - len(items)], False))
            consume(j, item, pending.pop(0), diag)
        for a, st in enumerate(pending):
            st_ref[a] = st

    def body(jj, carry):
        for u in range(KV_UNROLL):
            sweep(jj * KV_UNROLL + u, False)
        return carry

    n_full = qi // KV_UNROLL
    lax.fori_loop(0, n_full, body, 0)
    lax.fori_loop(n_full * KV_UNROLL, qi, lambda j, c: (sweep(j, False), c)[1], 0)
    sweep(qi, True)

    for h in range(MLA_HEADS):
        acc = acc_ref[h]
        o_ref[h * MLA_V:(h + 1) * MLA_V, :] = (acc[:MLA_V] / acc[MLA_V:MLA_V + 1, :]).astype(BF16)


def _attention(q, k, vt, *, batch, seq, tq, tk, tn):
    nq = seq // tq
    assert tq == tk and tq % tn == 0
    return pl.pallas_call(
        functools.partial(_attn_kernel, tk=tk, tn=tn),
        out_shape=jax.ShapeDtypeStruct((MLA_WIDTH, batch * seq), BF16),
        grid=(batch, nq),
        in_specs=[
            pl.BlockSpec((tq // 2, HEAD_PAD), lambda b, i: (b * nq + i, 0)),
            pl.BlockSpec((seq // 2, HEAD_PAD), lambda b, i: (b, 0)),
            pl.BlockSpec((MLA_HEADS * V_ROWS // 2, seq), lambda b, i: (0, b)),
        ],
        out_specs=pl.BlockSpec((MLA_WIDTH, tq), lambda b, i: (0, b * nq + i)),
        scratch_shapes=[pltpu.VMEM((MLA_HEADS, 1, tq), F32), pltpu.VMEM((MLA_HEADS, V_ROWS, tq), F32),
                        pltpu.VMEM((LOOKAHEAD, tk, tn), BF16)],
        compiler_params=_cparams("parallel", "arbitrary"),
        name="mla_attention",
    )(q, k, vt)


def _head_rows(x, n_heads, head_lanes):
    lane_head = lax.broadcasted_iota(jnp.int32, x.shape, 1) // head_lanes
    return jnp.concatenate([jnp.where(lane_head == h, x, 0.0).astype(BF16) for h in range(n_heads)], axis=0)


def _linmix_kernel(gq_ref, gk_ref, b_ref, gv_ref, sg_ref, rq_ref, rk_ref, rv_ref, sr_ref,
                   gng_ref, rng_ref, lgam_ref, ones_ref, o_ref, sg_state, sr_state, og_ref, or_ref):
    tc = gq_ref.shape[0]
    L = CHUNK

    @pl.when(pl.program_id(1) == 0)
    def _():
        sg_state[...] = jnp.zeros(sg_state.shape, F32)
        sr_state[...] = jnp.zeros(sr_state.shape, F32)

    ri = lax.broadcasted_iota(jnp.int32, (L, GLA_HEADS * L), 0)
    cj = lax.broadcasted_iota(jnp.int32, (L, GLA_HEADS * L), 1) % L
    lower = cj <= ri
    lgam = lgam_ref[...]
    lgam_cols = jnp.concatenate(
        [jnp.broadcast_to(lgam[:, h * RET_DK:h * RET_DK + 1], (1, L)) for h in range(RET_HEADS)], axis=1)
    ret_decay = jnp.exp(lgam_cols * jnp.abs(ri - cj).astype(F32))
    pos = lax.broadcasted_iota(jnp.int32, (L, QK_WIDTH), 0).astype(F32)
    q_dec = jnp.exp(lgam * (pos + 1.0))
    k_dec = jnp.exp(lgam * (L - 1.0 - pos))
    chunk_dec = jnp.exp(lgam * float(L))
    state_head = lax.broadcasted_iota(jnp.int32, (GLA_WIDTH, QK_WIDTH), 0) // GLA_DV
    state_lane_head = lax.broadcasted_iota(jnp.int32, (GLA_WIDTH, QK_WIDTH), 1) // GLA_DK
    state_diag = state_head == state_lane_head
    nt = (((1,), (1,)), ((), ()))

    for c in range(tc // L):
        rs = slice(c * L, (c + 1) * L)
        b = b_ref[rs, :]
        b_last = b[L - 1:L, :]
        eb, enb = jnp.exp(b), jnp.exp(-b)
        q = gq_ref[rs, :].astype(F32)
        k = gk_ref[rs, :].astype(F32)
        v = gv_ref[rs, :].astype(F32)
        q_fwd = (q * eb).astype(BF16)
        a_low = lax.dot_general(q_fwd, _head_rows(k * enb, GLA_HEADS, GLA_DK), nt, preferred_element_type=F32)
        a_up = lax.dot_general((q * enb).astype(BF16), _head_rows(k * eb, GLA_HEADS, GLA_DK), nt,
                               preferred_element_type=F32)
        attn = jnp.where(lower, a_low, a_up).astype(BF16)
        o_intra = jnp.dot(attn, _head_rows(v, GLA_HEADS, GLA_DV), preferred_element_type=F32)
        s_t = sg_state[...]
        o_inter = lax.dot_general(q_fwd, s_t.astype(BF16), nt, preferred_element_type=F32)
        og_ref[rs, :] = o_intra + o_inter
        k_state = (k * jnp.exp(b_last - b)).astype(BF16)
        d_state = jnp.dot(v.T.astype(BF16), k_state, preferred_element_type=F32)
        sg_state[...] = s_t * jnp.exp(b_last) + jnp.where(state_diag, d_state, 0.0)

        q = rq_ref[rs, :].astype(F32)
        k = rk_ref[rs, :].astype(F32)
        v = rv_ref[rs, :].astype(F32)
        scores = lax.dot_general(q.astype(BF16), _head_rows(k, RET_HEADS, RET_DK), nt,
                                 preferred_element_type=F32) * ret_decay
        o_intra = jnp.dot(scores.astype(BF16), _head_rows(v, RET_HEADS, RET_DV), preferred_element_type=F32)
        s_t = sr_state[...]
        o_inter = lax.dot_general((q * q_dec).astype(BF16), s_t.astype(BF16), nt, preferred_element_type=F32)
        or_ref[rs, :] = o_intra + o_inter
        d_state = jnp.dot(v.T.astype(BF16), (k * k_dec).astype(BF16), preferred_element_type=F32)
        sr_state[...] = s_t * chunk_dec + jnp.where(state_diag, d_state, 0.0)

    def head_norm(o, g, gate):
        o2 = o * o
        hi = o2.astype(BF16)
        lo = (o2 - hi.astype(F32)).astype(BF16)
        ss = (jnp.dot(hi, ones_ref[...], preferred_element_type=F32)
              + jnp.dot(lo, ones_ref[...], preferred_element_type=F32))
        return o * lax.rsqrt(ss * (1.0 / GLA_DV) + EPS) * g * gate

    o_ref[:, :GLA_WIDTH] = head_norm(og_ref[...], gng_ref[...], sg_ref[...].astype(F32)).astype(BF16)
    o_ref[:, GLA_WIDTH:] = head_norm(or_ref[...], rng_ref[...], sr_ref[...].astype(F32)).astype(BF16)


def _linmix(gq, gk, b, gv, sg, rq, rk, rv, sr, layer, gng, rng, lgam, ones_bd, *, batch, seq, tc):
    nt = seq // tc
    row = lambda w: pl.BlockSpec((tc, w), lambda bb, i: (bb * nt + i, 0))
    return pl.pallas_call(
        _linmix_kernel,
        out_shape=jax.ShapeDtypeStruct((batch * seq, GLA_WIDTH + RET_WIDTH), BF16),
        grid=(batch, nt),
        in_specs=[row(QK_WIDTH), row(QK_WIDTH), row(QK_WIDTH), row(GLA_WIDTH), row(GLA_WIDTH),
                  row(QK_WIDTH), row(QK_WIDTH), row(RET_WIDTH), row(RET_WIDTH),
                  _layer_spec(gng, layer), _layer_spec(rng, layer), _const_spec(lgam.shape),
                  _const_spec(ones_bd.shape)],
        out_specs=row(GLA_WIDTH + RET_WIDTH),
        scratch_shapes=[pltpu.VMEM((GLA_WIDTH, QK_WIDTH), F32), pltpu.VMEM((RET_WIDTH, QK_WIDTH), F32),
                        pltpu.VMEM((tc, GLA_WIDTH), F32), pltpu.VMEM((tc, RET_WIDTH), F32)],
        compiler_params=_cparams("parallel", "arbitrary"),
        name="gla_retention",
    )(gq, gk, b, gv, sg, rq, rk, rv, sr, gng, rng, lgam, ones_bd)


def _prep_mixer_layer(p):
    w_in = p["w_in"]
    splits = (MLA_Q_RANK, MLA_KV_RANK, ROPE_DIM, QK_WIDTH, QK_WIDTH, GLA_WIDTH, GLA_GATE_RANK, GLA_WIDTH,
              QK_WIDTH, QK_WIDTH, RET_WIDTH, RET_WIDTH)
    offs = [0]
    for s in splits:
        offs.append(offs[-1] + s)
    mq, mkv, mkr, gq, gk, gv, gg, gr, rq, rk, rv, rg = (w_in[:, offs[i]:offs[i + 1]] for i in range(12))
    d = w_in.shape[0]
    zeros = lambda n: jnp.zeros((d, n), F32)
    kpe_grp = jnp.concatenate([gg, zeros(MLA_NOPE - GLA_GATE_RANK), mkr, zeros(LANES - MLA_NOPE - ROPE_DIM)], 1)
    w_in_ext = jnp.concatenate([mq, mkv, kpe_grp, gq, gk, gv, gr, rq, rk, rv, rg], axis=1).astype(BF16)

    wq = p["mla_w_q_up"].reshape(MLA_Q_RANK, MLA_HEADS, MLA_NOPE + ROPE_DIM)
    pad_q = jnp.zeros((MLA_Q_RANK, MLA_HEADS, LANES - MLA_NOPE - ROPE_DIM), F32)
    wq_main = jnp.concatenate([wq, pad_q], axis=-1).reshape(MLA_Q_RANK, HEAD_PAD)
    wkv = p["mla_w_kv_up"].reshape(MLA_KV_RANK, MLA_HEADS, MLA_NOPE + MLA_V)
    pad_kv = jnp.zeros((MLA_KV_RANK, MLA_HEADS, LANES - MLA_NOPE), F32)
    wk_pad = jnp.concatenate([wkv[..., :MLA_NOPE], pad_kv], axis=-1).reshape(MLA_KV_RANK, HEAD_PAD)
    pad_v = jnp.zeros((MLA_KV_RANK, MLA_HEADS, V_ROWS - MLA_V), F32)
    wv_pad = jnp.concatenate([wkv[..., MLA_NOPE:], pad_v], axis=-1).reshape(MLA_KV_RANK, MLA_HEADS * V_ROWS)
    w_gate = jnp.concatenate([p["gla_w_gate_up"], jnp.zeros((LANES - GLA_GATE_RANK, QK_WIDTH), F32)], axis=0)
    row = lambda a: a.reshape(1, -1)
    w_out = p["w_out"].astype(BF16)
    return {
        "mix_pre_g": row(p["mix_pre_g"]), "mix_post_g": row(p["mix_post_g"]),
        "w_in": w_in_ext,
        "q_norm_g": row(p["mla_q_norm_g"]),
        "wq": wq_main.astype(BF16),
        "kv_norm_g": row(p["mla_kv_norm_g"]),
        "wk": wk_pad.astype(BF16), "wvt": wv_pad.T.astype(BF16),
        "w_gate": w_gate.astype(BF16), "b_gate": row(p["gla_b_gate"]),
        "gla_norm_g": row(jnp.tile(p["gla_norm_g"], GLA_HEADS)),
        "ret_norm_g": row(jnp.tile(p["ret_norm_g"], RET_HEADS)),
        "w_out_mla": w_out[:MLA_WIDTH], "w_out_lin": w_out[MLA_WIDTH:],
    }


def _ffn_params(p, prefix):
    gain = lambda g: g.reshape(g.shape[0], 1, g.shape[1])
    return (gain(p[prefix + "_pre_g"]), gain(p[prefix + "_post_g"]),
            p[prefix + "_w_gate"], p[prefix + "_w_up"], p[prefix + "_w_down"])


def _tile(n, pref):
    while n % pref:
        pref //= 2
    return pref


def kernel(x, positions, ffn1_pre_g, ffn1_post_g, ffn1_w_gate, ffn1_w_up, ffn1_w_down, mix_pre_g, mix_post_g, w_in, mla_q_norm_g, mla_w_q_up, mla_kv_norm_g, mla_w_kv_up, gla_w_gate_up, gla_b_gate, gla_norm_g, ret_norm_g, w_out, ffn2_pre_g, ffn2_post_g, ffn2_w_gate, ffn2_w_up, ffn2_w_down):
    p = dict(ffn1_pre_g=ffn1_pre_g, ffn1_post_g=ffn1_post_g, ffn1_w_gate=ffn1_w_gate, ffn1_w_up=ffn1_w_up,
             ffn1_w_down=ffn1_w_down, ffn2_pre_g=ffn2_pre_g, ffn2_post_g=ffn2_post_g, ffn2_w_gate=ffn2_w_gate,
             ffn2_w_up=ffn2_w_up, ffn2_w_down=ffn2_w_down)
    mixer = dict(mix_pre_g=mix_pre_g, mix_post_g=mix_post_g, w_in=w_in, mla_q_norm_g=mla_q_norm_g,
                 mla_w_q_up=mla_w_q_up, mla_kv_norm_g=mla_kv_norm_g, mla_w_kv_up=mla_w_kv_up,
                 gla_w_gate_up=gla_w_gate_up, gla_b_gate=gla_b_gate, gla_norm_g=gla_norm_g,
                 ret_norm_g=ret_norm_g, w_out=w_out)
    batch, seq, d = x.shape
    t = batch * seq
    depth = w_in.shape[0]
    tm = _tile(seq, 512)
    tq = _tile(seq, 1024)
    tc = _tile(seq, 512)

    inv = ROPE_BASE ** (-jnp.arange(0, ROPE_DIM, 2, dtype=F32) / ROPE_DIM)
    inv_lane = jnp.tile(inv, LANES // HALF_ROPE).reshape(1, LANES)
    tb = _tile(tm, 256)
    idx = jnp.arange(tb)
    tri = ((idx[:, None] >= idx[None, :]) & (idx[:, None] // CHUNK == idx[None, :] // CHUNK)).astype(BF16)
    hd = jnp.arange(GLA_WIDTH) // GLA_DV
    ones_bd = (hd[:, None] == hd[None, :]).astype(BF16)
    log_gamma = jnp.log1p(-jnp.exp2(-5.0 - jnp.arange(RET_HEADS, dtype=F32)))
    lgam = jnp.repeat(log_gamma, RET_DK).reshape(1, QK_WIDTH)

    fw1 = _ffn_params(p, "ffn1")
    fw2 = _ffn_params(p, "ffn2")
    mw = jax.vmap(_prep_mixer_layer)(mixer)
    tokens_per_row = LANES // HALF_ROPE
    pos_lanes = jnp.repeat(positions.reshape(t // tokens_per_row, tokens_per_row), HALF_ROPE, axis=1)
    cos_c, sin_c = _rope_tables(pos_lanes, inv_lane, tm=_tile(t // tokens_per_row, 512))
    cosb = jnp.tile(cos_c.reshape(t, HALF_ROPE), (1, tokens_per_row))
    sinb = jnp.tile(sin_c.reshape(t, HALF_ROPE), (1, tokens_per_row))

    xf = x.reshape(t, d)
    for l in range(depth):
        xf = _ffn(xf, l, fw1, tm=tm, tf=256)
        (q, k, gq, gk, b, gv, sg, rq, rk, rv, sr, vt) = _proj(xf, cosb, sinb, l, mw, tri, tm=tm)
        o_mla = _attention(q, k, vt, batch=batch, seq=seq, tq=tq, tk=tq, tn=_tile(tq, 256))
        o_lin = _linmix(gq, gk, b, gv, sg, rq, rk, rv, sr, l, mw["gla_norm_g"], mw["ret_norm_g"], lgam, ones_bd,
                        batch=batch, seq=seq, tc=tc)
        xf = _ffn(xf, l, fw2, tm=tm, tf=256,
                  mixer_out=(o_mla, o_lin, mw["w_out_mla"], mw["w_out_lin"], mw["mix_post_g"]))
    return xf.reshape(batch, seq, d)
```

```python
import functools
import math

import jax
import jax.numpy as jnp
from jax import lax
from jax.experimental import pallas as pl
from jax.experimental.pallas import tpu as pltpu

F32 = jnp.float32
BF16 = jnp.bfloat16

D_MODEL = 1024
D_FF = 2816
CHUNK = 64
EPS = 1e-6
ROPE_BASE = 10000.0
ROPE_DIM = 32
HALF_ROPE = ROPE_DIM // 2

MLA_HEADS = 8
MLA_Q_RANK = 256
MLA_KV_RANK = 128
MLA_NOPE = 64
MLA_V = 64
GLA_HEADS = 4
GLA_DK = 32
GLA_DV = 64
GLA_GATE_RANK = 16
GLA_TAU = 16.0
RET_HEADS = 4
RET_DK = 32
RET_DV = 64

LANES = 128
MLA_WIDTH = MLA_HEADS * MLA_V
GLA_WIDTH = GLA_HEADS * GLA_DV
RET_WIDTH = RET_HEADS * RET_DV
QK_WIDTH = GLA_HEADS * GLA_DK
HEAD_PAD = MLA_HEADS * LANES

C_MQ = 0
C_MKV = C_MQ + MLA_Q_RANK
C_KPE = C_MKV + MLA_KV_RANK
C_GQ = C_KPE + LANES
C_GK = C_GQ + QK_WIDTH
C_GV = C_GK + QK_WIDTH
C_GR = C_GV + GLA_WIDTH
C_RQ = C_GR + GLA_WIDTH
C_RK = C_RQ + QK_WIDTH
C_RV = C_RK + QK_WIDTH
C_RG = C_RV + RET_WIDTH
D_IN_EXT = C_RG + RET_WIDTH

LOG2_E = math.log2(math.e)
LOOKAHEAD = 3
KV_UNROLL = 1
V_ROWS = 80
VMEM_LIMIT = 56 * 1024 * 1024


def _rms(x, g):
    return x * lax.rsqrt(jnp.mean(x * x, axis=-1, keepdims=True) + EPS) * g


def _pack_rows(x):
    return pltpu.bitcast(x, jnp.uint32)


def _unpack_rows(x):
    return pltpu.bitcast(x, BF16)


def _silu(x):
    return x * jax.nn.sigmoid(x)


def _cparams(*sem):
    return pltpu.CompilerParams(dimension_semantics=sem, vmem_limit_bytes=VMEM_LIMIT)


def _const_spec(shape):
    nd = len(shape)
    return pl.BlockSpec(shape, lambda *_: (0,) * nd, pipeline_mode=pl.Buffered(1))


def _layer_spec(stacked, layer):
    nd = stacked.ndim
    return pl.BlockSpec((None,) + stacked.shape[1:], lambda *_: (layer,) + (0,) * (nd - 1),
                        pipeline_mode=pl.Buffered(1))


def _ffn_kernel(*refs, tf, with_mixer_out):
    if with_mixer_out:
        x_ref, om_ref, ol_ref, wa_ref, wb_ref, mpost_ref = refs[:6]
        refs = refs[6:]
        m = (lax.dot_general(om_ref[...], wa_ref[...], (((0,), (0,)), ((), ())), preferred_element_type=F32)
             + jnp.dot(ol_ref[...], wb_ref[...], preferred_element_type=F32))
        x = x_ref[...] + _rms(m, mpost_ref[...])
    else:
        x_ref = refs[0]
        refs = refs[1:]
        x = x_ref[...]
    pre_ref, post_ref, wg_ref, wu_ref, wd_ref, o_ref, h_ref = refs
    xn = _rms(x, pre_ref[...]).astype(BF16)
    for c in range(D_FF // tf):
        sl = slice(c * tf, (c + 1) * tf)
        g = jnp.dot(xn, wg_ref[:, sl].astype(BF16), preferred_element_type=F32)
        u = jnp.dot(xn, wu_ref[:, sl].astype(BF16), preferred_element_type=F32)
        h_ref[:, sl] = (_silu(g) * u).astype(BF16)
    f = jnp.dot(h_ref[...], wd_ref[...].astype(BF16), preferred_element_type=F32)
    o_ref[...] = x + 0.5 * _rms(f, post_ref[...])


def _ffn(x, layer, fw, *, tm, tf, mixer_out=None):
    t = x.shape[0]
    row = lambda w: pl.BlockSpec((tm, w), lambda i: (i, 0))
    args, specs = [x], [row(D_MODEL)]
    if mixer_out is not None:
        o_mla, o_lin, wa, wb, mpost = mixer_out
        args += [o_mla, o_lin, wa, wb, mpost]
        specs += [pl.BlockSpec((MLA_WIDTH, tm), lambda i: (0, i)), row(GLA_WIDTH + RET_WIDTH)]
        specs += [_layer_spec(a, layer) for a in (wa, wb, mpost)]
    args += list(fw)
    specs += [_layer_spec(a, layer) for a in fw]
    return pl.pallas_call(
        functools.partial(_ffn_kernel, tf=tf, with_mixer_out=mixer_out is not None),
        out_shape=jax.ShapeDtypeStruct(x.shape, F32),
        grid=(t // tm,),
        in_specs=specs,
        out_specs=row(D_MODEL),
        scratch_shapes=[pltpu.VMEM((tm, D_FF), BF16)],
        compiler_params=_cparams("parallel"),
        name="ffn_mixer_out" if mixer_out is not None else "ffn",
    )(*args)


def _rope_kernel(pos_ref, inv_ref, phase_ref, tab_ref):
    tab_ref[...] = jnp.cos(pos_ref[...].astype(F32) * inv_ref[...] - phase_ref[...])


def _rope_table(pos_col, inv_lane, phase_lane, *, tm):
    t = pos_col.shape[0]
    return pl.pallas_call(
        _rope_kernel,
        out_shape=jax.ShapeDtypeStruct((t, LANES), F32),
        grid=(t // tm,),
        in_specs=[pl.BlockSpec((tm, 1), lambda i: (i, 0)), _const_spec((1, LANES)), _const_spec((1, LANES))],
        out_specs=pl.BlockSpec((tm, LANES), lambda i: (i, 0)),
        compiler_params=_cparams("parallel"),
        name="rope_table",
    )(pos_col, inv_lane, phase_lane)


def _proj_kernel(x_ref, rope_ref, pre_ref, win_ref, qg_ref, wq_ref, kvg_ref, wk_ref, wvt_ref,
                 wgate_ref, bgate_ref, tri_ref,
                 q_ref, k_ref, gq_ref, gk_ref, b_ref, gv_ref, sg_ref,
                 rq_ref, rk_ref, rv_ref, sr_ref, vt_ref):
    tm = x_ref.shape[0]
    xn = _rms(x_ref[...], pre_ref[...]).astype(BF16)

    chunk_starts = (0, C_GQ, C_RQ, D_IN_EXT)
    chunks = [jnp.dot(xn, win_ref[:, a:b], preferred_element_type=F32)
              for a, b in zip(chunk_starts[:-1], chunk_starts[1:])]

    def proj(c0, width):
        for a, b, chunk in zip(chunk_starts[:-1], chunk_starts[1:], chunks):
            if a <= c0 and c0 + width <= b:
                return chunk[:, c0 - a:c0 - a + width]
        raise ValueError("column group straddles two chunks")

    lane = lax.broadcasted_iota(jnp.int32, (tm, LANES), 1)
    tab = rope_ref[...]
    swapped = pltpu.roll(tab, LANES // 2, axis=1)
    cosb = jnp.where(lane < LANES // 2, tab, swapped)
    sinb = jnp.where(lane < LANES // 2, swapped, tab)
    first_half = (lane % ROPE_DIM) < HALF_ROPE
    sins = jnp.where(first_half, -sinb, sinb)

    def rot_half(v):
        return jnp.where(first_half, pltpu.roll(v, LANES - HALF_ROPE, axis=1), pltpu.roll(v, HALF_ROPE, axis=1))

    hq = _rms(proj(C_MQ, MLA_Q_RANK), qg_ref[...]).astype(BF16)
    in_rope = (lane >= MLA_NOPE) & (lane < MLA_NOPE + ROPE_DIM)
    qcos = jnp.where(in_rope, cosb, 1.0)
    qsin = jnp.where(in_rope, sins, 0.0)
    q_scale = (MLA_NOPE + ROPE_DIM) ** -0.5 * LOG2_E
    qa = jnp.dot(hq, wq_ref[...], preferred_element_type=F32)
    for h in range(MLA_HEADS):
        sl = slice(h * LANES, (h + 1) * LANES)
        q_ref[:, sl] = _pack_rows(((qa[:, sl] * qcos + rot_half(qa[:, sl]) * qsin) * q_scale).astype(BF16))

    hkv = _rms(proj(C_MKV, MLA_KV_RANK), kvg_ref[...]).astype(BF16)
    g1 = proj(C_KPE, LANES)
    kpe = jnp.where(in_rope, g1 * cosb + rot_half(g1) * sins, 0.0)
    ones_row = jnp.where(lax.broadcasted_iota(jnp.int32, (V_ROWS, tm), 0) == MLA_V, 1.0, 0.0)
    kn = jnp.dot(hkv, wk_ref[...], preferred_element_type=F32)
    vt = lax.dot_general(wvt_ref[...], hkv, (((1,), (1,)), ((), ())), preferred_element_type=F32)
    for h in range(MLA_HEADS):
        sl = slice(h * LANES, (h + 1) * LANES)
        vs = slice(h * V_ROWS, (h + 1) * V_ROWS)
        k_ref[:, sl] = _pack_rows((kn[:, sl] + kpe).astype(BF16))
        vt_ref[h * V_ROWS // 2:(h + 1) * V_ROWS // 2, :] = _pack_rows((vt[vs, :] + ones_row).astype(BF16))

    z = jnp.dot(g1.astype(BF16), wgate_ref[...], preferred_element_type=F32) + bgate_ref[...]
    log_a = (jnp.minimum(z, 0.0) - jnp.log1p(jnp.exp(-jnp.abs(z)))) / GLA_TAU
    la_hi = log_a.astype(BF16)
    la_lo = (log_a - la_hi.astype(F32)).astype(BF16)
    la_split = jnp.concatenate([la_hi, la_lo], axis=1)
    tri = tri_ref[...]
    tb = tri.shape[0]
    for r in range(tm // tb):
        rs = slice(r * tb, (r + 1) * tb)
        cum = jnp.dot(tri, la_split[rs], preferred_element_type=F32)
        b_ref[rs, :] = cum[:, :QK_WIDTH] + cum[:, QK_WIDTH:]
    gq_ref[...] = (proj(C_GQ, QK_WIDTH) * (GLA_DK ** -0.5)).astype(BF16)
    gk_ref[...] = proj(C_GK, QK_WIDTH).astype(BF16)
    gv_ref[...] = proj(C_GV, GLA_WIDTH).astype(BF16)
    sg_ref[...] = _silu(proj(C_GR, GLA_WIDTH)).astype(BF16)

    rq = proj(C_RQ, QK_WIDTH)
    rk = proj(C_RK, QK_WIDTH)
    rq_ref[...] = (rq * cosb + rot_half(rq) * sins).astype(BF16)
    rk_ref[...] = ((rk * cosb + rot_half(rk) * sins) * (RET_DK ** -0.5)).astype(BF16)
    rv_ref[...] = proj(C_RV, RET_WIDTH).astype(BF16)
    sr_ref[...] = _silu(proj(C_RG, RET_WIDTH)).astype(BF16)


def _proj(x, rope_tab, layer, mw, tri, *, tm):
    t = x.shape[0]
    row = lambda w: pl.BlockSpec((tm, w), lambda i: (i, 0))
    out_widths = [(QK_WIDTH, BF16), (QK_WIDTH, BF16), (QK_WIDTH, F32), (GLA_WIDTH, BF16), (GLA_WIDTH, BF16),
                  (QK_WIDTH, BF16), (QK_WIDTH, BF16), (RET_WIDTH, BF16), (RET_WIDTH, BF16)]
    stacked = [mw["mix_pre_g"], mw["w_in"], mw["q_norm_g"], mw["wq"], mw["kv_norm_g"], mw["wk"], mw["wvt"],
               mw["w_gate"], mw["b_gate"]]
    return pl.pallas_call(
        _proj_kernel,
        out_shape=(jax.ShapeDtypeStruct((t // 2, HEAD_PAD), jnp.uint32),) * 2
        + tuple(jax.ShapeDtypeStruct((t, w), dt) for w, dt in out_widths)
        + (jax.ShapeDtypeStruct((MLA_HEADS * V_ROWS // 2, t), jnp.uint32),),
        grid=(t // tm,),
        in_specs=[row(D_MODEL), row(LANES)] + [_layer_spec(a, layer) for a in stacked]
        + [_const_spec(tri.shape)],
        out_specs=(pl.BlockSpec((tm // 2, HEAD_PAD), lambda i: (i, 0)),) * 2 + tuple(row(w) for w, _ in out_widths)
        + (pl.BlockSpec((MLA_HEADS * V_ROWS // 2, tm), lambda i: (0, i)),),
        compiler_params=_cparams("parallel"),
        name="mixer_proj",
    )(x, rope_tab, *stacked, tri)


def _attn_kernel(q_ref, k_ref, vt_ref, o_ref, m_ref, acc_ref, st_ref, *, tk, tn):
    tq = 2 * q_ref.shape[0]
    qi = pl.program_id(1)
    nt = (((1,), (1,)), ((), ()))
    items = [(h, c) for h in range(MLA_HEADS) for c in range(tq // tn)]

    def key_rows(j, n):
        return pl.ds(pl.multiple_of(j * tk, tk), n)

    def packed_key_rows(j, n):
        return pl.ds(pl.multiple_of(j * (tk // 2), tk // 2), n // 2)

    def n_keys(item, diag):
        return (item[1] + 1) * tn if diag else tk

    def scores(j, item, diag):
        h, c = item
        hs = slice(h * LANES, (h + 1) * LANES)
        keys = _unpack_rows(k_ref[packed_key_rows(j, n_keys(item, diag)), hs])
        queries = _unpack_rows(q_ref[c * tn // 2:(c + 1) * tn // 2, hs])
        return lax.dot_general(keys, queries, nt, preferred_element_type=F32).astype(BF16)

    key_chunk = lax.broadcasted_iota(jnp.int32, (tn, tn), 0) // CHUNK
    qry_chunk = lax.broadcasted_iota(jnp.int32, (tn, tn), 1) // CHUNK
    diag_mask = jnp.where(key_chunk <= qry_chunk, 0.0, -jnp.inf).astype(BF16)

    def consume(j, item, st, diag):
        h, c = item
        vs = slice(h * V_ROWS // 2, (h + 1) * V_ROWS // 2)
        cs = slice(c * tn, (c + 1) * tn)
        nk = n_keys(item, diag)
        st = st[:nk]
        if diag:
            own = st[nk - tn:] + diag_mask
            st = own if nk == tn else jnp.concatenate([st[:nk - tn], own], axis=0)
        m_old = m_ref[h, :, cs]
        m_new = jnp.maximum(m_old, jnp.max(st, axis=0, keepdims=True).astype(F32))
        p = jnp.exp2(st - m_new.astype(BF16))
        acc_ref[h, :, cs] = (acc_ref[h, :, cs] * jnp.exp2(m_old - m_new)
                             + jnp.dot(_unpack_rows(vt_ref[vs, key_rows(j, nk)]), p, preferred_element_type=F32))
        m_ref[h, :, cs] = m_new

    for h in range(MLA_HEADS):
        m_ref[h] = jnp.full((1, tq), -jnp.inf, F32)
        acc_ref[h] = jnp.zeros((V_ROWS, tq), F32)

    for a in range(LOOKAHEAD):
        st_ref[a] = scores(0, items[a], False)

    def sweep(j, diag):
        pending = [st_ref[a] for a in range(LOOKAHEAD)]
        for i, item in enumerate(items):
            ahead = i + LOOKAHEAD
            if ahead < len(items):
                pending.append(scores(j, items[ahead], diag))
            elif not diag:
                pending.append(scores(j + 1, items[ahead - len(items)], False))
            consume(j, item, pending.pop(0), diag)
        for a, st in enumerate(pending):
            st_ref[a] = st

    def body(jj, carry):
        for u in range(KV_UNROLL):
            sweep(jj * KV_UNROLL + u, False)
        return carry

    n_full = qi // KV_UNROLL
    lax.fori_loop(0, n_full, body, 0)
    lax.fori_loop(n_full * KV_UNROLL, qi, lambda j, c: (sweep(j, False), c)[1], 0)
    sweep(qi, True)

    for h in range(MLA_HEADS):
        acc = acc_ref[h]
        o_ref[h * MLA_V:(h + 1) * MLA_V, :] = (acc[:MLA_V] / acc[MLA_V:MLA_V + 1, :]).astype(BF16)


def _attention(q, k, vt, *, batch, seq, tq, tk, tn):
    nq = seq // tq
    assert tq == tk and tq % tn == 0
    return pl.pallas_call(
        functools.partial(_attn_kernel, tk=tk, tn=tn),
        out_shape=jax.ShapeDtypeStruct((MLA_WIDTH, batch * seq), BF16),
        grid=(batch, nq),
        in_specs=[
            pl.BlockSpec((tq // 2, HEAD_PAD), lambda b, i: (b * nq + i, 0)),
            pl.BlockSpec((seq // 2, HEAD_PAD), lambda b, i: (b, 0)),
            pl.BlockSpec((MLA_HEADS * V_ROWS // 2, seq), lambda b, i: (0, b)),
        ],
        out_specs=pl.BlockSpec((MLA_WIDTH, tq), lambda b, i: (0, b * nq + i)),
        scratch_shapes=[pltpu.VMEM((MLA_HEADS, 1, tq), F32), pltpu.VMEM((MLA_HEADS, V_ROWS, tq), F32),
                        pltpu.VMEM((LOOKAHEAD, tk, tn), BF16)],
        compiler_params=_cparams("parallel", "arbitrary"),
        name="mla_attention",
    )(q, k, vt)


def _head_rows(x, n_heads, head_lanes):
    lane_head = lax.broadcasted_iota(jnp.int32, x.shape, 1) // head_lanes
    return jnp.concatenate([jnp.where(lane_head == h, x, 0.0).astype(BF16) for h in range(n_heads)], axis=0)


def _linmix_kernel(gq_ref, gk_ref, b_ref, gv_ref, sg_ref, rq_ref, rk_ref, rv_ref, sr_ref,
                   gng_ref, rng_ref, lgam_ref, ones_ref, o_ref, sg_state, sr_state, og_ref, or_ref):
    tc = gq_ref.shape[0]
    L = CHUNK

    @pl.when(pl.program_id(1) == 0)
    def _():
        sg_state[...] = jnp.zeros(sg_state.shape, F32)
        sr_state[...] = jnp.zeros(sr_state.shape, F32)

    ri = lax.broadcasted_iota(jnp.int32, (L, GLA_HEADS * L), 0)
    cj = lax.broadcasted_iota(jnp.int32, (L, GLA_HEADS * L), 1) % L
    lower = cj <= ri
    lgam = lgam_ref[...]
    lgam_cols = jnp.concatenate(
        [jnp.broadcast_to(lgam[:, h * RET_DK:h * RET_DK + 1], (1, L)) for h in range(RET_HEADS)], axis=1)
    ret_decay = jnp.exp(lgam_cols * jnp.abs(ri - cj).astype(F32))
    pos = lax.broadcasted_iota(jnp.int32, (L, QK_WIDTH), 0).astype(F32)
    q_dec = jnp.exp(lgam * (pos + 1.0))
    k_dec = jnp.exp(lgam * (L - 1.0 - pos))
    chunk_dec = jnp.exp(lgam * float(L))
    state_head = lax.broadcasted_iota(jnp.int32, (GLA_WIDTH, QK_WIDTH), 0) // GLA_DV
    state_lane_head = lax.broadcasted_iota(jnp.int32, (GLA_WIDTH, QK_WIDTH), 1) // GLA_DK
    state_diag = state_head == state_lane_head
    nt = (((1,), (1,)), ((), ()))

    for c in range(tc // L):
        rs = slice(c * L, (c + 1) * L)
        b = b_ref[rs, :]
        b_last = b[L - 1:L, :]
        eb, enb = jnp.exp(b), jnp.exp(-b)
        q = gq_ref[rs, :].astype(F32)
        k = gk_ref[rs, :].astype(F32)
        v = gv_ref[rs, :].astype(F32)
        q_fwd = (q * eb).astype(BF16)
        a_low = lax.dot_general(q_fwd, _head_rows(k * enb, GLA_HEADS, GLA_DK), nt, preferred_element_type=F32)
        a_up = lax.dot_general((q * enb).astype(BF16), _head_rows(k * eb, GLA_HEADS, GLA_DK), nt,
                               preferred_element_type=F32)
        attn = jnp.where(lower, a_low, a_up).astype(BF16)
        o_intra = jnp.dot(attn, _head_rows(v, GLA_HEADS, GLA_DV), preferred_element_type=F32)
        s_t = sg_state[...]
        o_inter = lax.dot_general(q_fwd, s_t.astype(BF16), nt, preferred_element_type=F32)
        og_ref[rs, :] = o_intra + o_inter
        k_state = (k * jnp.exp(b_last - b)).astype(BF16)
        d_state = jnp.dot(v.T.astype(BF16), k_state, preferred_element_type=F32)
        sg_state[...] = s_t * jnp.exp(b_last) + jnp.where(state_diag, d_state, 0.0)

        q = rq_ref[rs, :].astype(F32)
        k = rk_ref[rs, :].astype(F32)
        v = rv_ref[rs, :].astype(F32)
        scores = lax.dot_general(q.astype(BF16), _head_rows(k, RET_HEADS, RET_DK), nt,
                                 preferred_element_type=F32) * ret_decay
        o_intra = jnp.dot(scores.astype(BF16), _head_rows(v, RET_HEADS, RET_DV), preferred_element_type=F32)
        s_t = sr_state[...]
        o_inter = lax.dot_general((q * q_dec).astype(BF16), s_t.astype(BF16), nt, preferred_element_type=F32)
        or_ref[rs, :] = o_intra + o_inter
        d_state = jnp.dot(v.T.astype(BF16), (k * k_dec).astype(BF16), preferred_element_type=F32)
        sr_state[...] = s_t * chunk_dec + jnp.where(state_diag, d_state, 0.0)

    def head_norm(o, g, gate):
        o2 = o * o
        hi = o2.astype(BF16)
        lo = (o2 - hi.astype(F32)).astype(BF16)
        ss = (jnp.dot(hi, ones_ref[...], preferred_element_type=F32)
              + jnp.dot(lo, ones_ref[...], preferred_element_type=F32))
        return o * lax.rsqrt(ss * (1.0 / GLA_DV) + EPS) * g * gate

    o_ref[:, :GLA_WIDTH] = head_norm(og_ref[...], gng_ref[...], sg_ref[...].astype(F32)).astype(BF16)
    o_ref[:, GLA_WIDTH:] = head_norm(or_ref[...], rng_ref[...], sr_ref[...].astype(F32)).astype(BF16)


def _linmix(gq, gk, b, gv, sg, rq, rk, rv, sr, layer, gng, rng, lgam, ones_bd, *, batch, seq, tc):
    nt = seq // tc
    row = lambda w: pl.BlockSpec((tc, w), lambda bb, i: (bb * nt + i, 0))
    return pl.pallas_call(
        _linmix_kernel,
        out_shape=jax.ShapeDtypeStruct((batch * seq, GLA_WIDTH + RET_WIDTH), BF16),
        grid=(batch, nt),
        in_specs=[row(QK_WIDTH), row(QK_WIDTH), row(QK_WIDTH), row(GLA_WIDTH), row(GLA_WIDTH),
                  row(QK_WIDTH), row(QK_WIDTH), row(RET_WIDTH), row(RET_WIDTH),
                  _layer_spec(gng, layer), _layer_spec(rng, layer), _const_spec(lgam.shape),
                  _const_spec(ones_bd.shape)],
        out_specs=row(GLA_WIDTH + RET_WIDTH),
        scratch_shapes=[pltpu.VMEM((GLA_WIDTH, QK_WIDTH), F32), pltpu.VMEM((RET_WIDTH, QK_WIDTH), F32),
                        pltpu.VMEM((tc, GLA_WIDTH), F32), pltpu.VMEM((tc, RET_WIDTH), F32)],
        compiler_params=_cparams("parallel", "arbitrary"),
        name="gla_retention",
    )(gq, gk, b, gv, sg, rq, rk, rv, sr, gng, rng, lgam, ones_bd)


def _prep_mixer_layer(p):
    w_in = p["w_in"]
    splits = (MLA_Q_RANK, MLA_KV_RANK, ROPE_DIM, QK_WIDTH, QK_WIDTH, GLA_WIDTH, GLA_GATE_RANK, GLA_WIDTH,
              QK_WIDTH, QK_WIDTH, RET_WIDTH, RET_WIDTH)
    offs = [0]
    for s in splits:
        offs.append(offs[-1] + s)
    mq, mkv, mkr, gq, gk, gv, gg, gr, rq, rk, rv, rg = (w_in[:, offs[i]:offs[i + 1]] for i in range(12))
    d = w_in.shape[0]
    zeros = lambda n: jnp.zeros((d, n), F32)
    kpe_grp = jnp.concatenate([gg, zeros(MLA_NOPE - GLA_GATE_RANK), mkr, zeros(LANES - MLA_NOPE - ROPE_DIM)], 1)
    w_in_ext = jnp.concatenate([mq, mkv, kpe_grp, gq, gk, gv, gr, rq, rk, rv, rg], axis=1).astype(BF16)

    wq = p["mla_w_q_up"].reshape(MLA_Q_RANK, MLA_HEADS, MLA_NOPE + ROPE_DIM)
    pad_q = jnp.zeros((MLA_Q_RANK, MLA_HEADS, LANES - MLA_NOPE - ROPE_DIM), F32)
    wq_main = jnp.concatenate([wq, pad_q], axis=-1).reshape(MLA_Q_RANK, HEAD_PAD)
    wkv = p["mla_w_kv_up"].reshape(MLA_KV_RANK, MLA_HEADS, MLA_NOPE + MLA_V)
    pad_kv = jnp.zeros((MLA_KV_RANK, MLA_HEADS, LANES - MLA_NOPE), F32)
    wk_pad = jnp.concatenate([wkv[..., :MLA_NOPE], pad_kv], axis=-1).reshape(MLA_KV_RANK, HEAD_PAD)
    pad_v = jnp.zeros((MLA_KV_RANK, MLA_HEADS, V_ROWS - MLA_V), F32)
    wv_pad = jnp.concatenate([wkv[..., MLA_NOPE:], pad_v], axis=-1).reshape(MLA_KV_RANK, MLA_HEADS * V_ROWS)
    w_gate = jnp.concatenate([p["gla_w_gate_up"], jnp.zeros((LANES - GLA_GATE_RANK, QK_WIDTH), F32)], axis=0)
    row = lambda a: a.reshape(1, -1)
    w_out = p["w_out"].astype(BF16)
    return {
        "mix_pre_g": row(p["mix_pre_g"]), "mix_post_g": row(p["mix_post_g"]),
        "w_in": w_in_ext,
        "q_norm_g": row(p["mla_q_norm_g"]),
        "wq": wq_main.astype(BF16),
        "kv_norm_g": row(p["mla_kv_norm_g"]),
        "wk": wk_pad.astype(BF16), "wvt": wv_pad.T.astype(BF16),
        "w_gate": w_gate.astype(BF16), "b_gate": row(p["gla_b_gate"]),
        "gla_norm_g": row(jnp.tile(p["gla_norm_g"], GLA_HEADS)),
        "ret_norm_g": row(jnp.tile(p["ret_norm_g"], RET_HEADS)),
        "w_out_mla": w_out[:MLA_WIDTH], "w_out_lin": w_out[MLA_WIDTH:],
    }


def _ffn_params(p, prefix):
    gain = lambda g: g.reshape(g.shape[0], 1, g.shape[1])
    return (gain(p[prefix + "_pre_g"]), gain(p[prefix + "_post_g"]),
            p[prefix + "_w_gate"], p[prefix + "_w_up"], p[prefix + "_w_down"])


def _tile(n, pref):
    while n % pref:
        pref //= 2
    return pref


def kernel(x, positions, ffn1_pre_g, ffn1_post_g, ffn1_w_gate, ffn1_w_up, ffn1_w_down, mix_pre_g, mix_post_g, w_in, mla_q_norm_g, mla_w_q_up, mla_kv_norm_g, mla_w_kv_up, gla_w_gate_up, gla_b_gate, gla_norm_g, ret_norm_g, w_out, ffn2_pre_g, ffn2_post_g, ffn2_w_gate, ffn2_w_up, ffn2_w_down):
    p = dict(ffn1_pre_g=ffn1_pre_g, ffn1_post_g=ffn1_post_g, ffn1_w_gate=ffn1_w_gate, ffn1_w_up=ffn1_w_up,
             ffn1_w_down=ffn1_w_down, ffn2_pre_g=ffn2_pre_g, ffn2_post_g=ffn2_post_g, ffn2_w_gate=ffn2_w_gate,
             ffn2_w_up=ffn2_w_up, ffn2_w_down=ffn2_w_down)
    mixer = dict(mix_pre_g=mix_pre_g, mix_post_g=mix_post_g, w_in=w_in, mla_q_norm_g=mla_q_norm_g,
                 mla_w_q_up=mla_w_q_up, mla_kv_norm_g=mla_kv_norm_g, mla_w_kv_up=mla_w_kv_up,
                 gla_w_gate_up=gla_w_gate_up, gla_b_gate=gla_b_gate, gla_norm_g=gla_norm_g,
                 ret_norm_g=ret_norm_g, w_out=w_out)
    batch, seq, d = x.shape
    t = batch * seq
    depth = w_in.shape[0]
    tm = _tile(seq, 512)
    tq = _tile(seq, 1024)
    tc = _tile(seq, 512)

    inv = ROPE_BASE ** (-jnp.arange(0, ROPE_DIM, 2, dtype=F32) / ROPE_DIM)
    inv_lane = jnp.tile(inv, LANES // HALF_ROPE).reshape(1, LANES)
    tb = _tile(tm, 256)
    idx = jnp.arange(tb)
    tri = ((idx[:, None] >= idx[None, :]) & (idx[:, None] // CHUNK == idx[None, :] // CHUNK)).astype(BF16)
    hd = jnp.arange(GLA_WIDTH) // GLA_DV
    ones_bd = (hd[:, None] == hd[None, :]).astype(BF16)
    log_gamma = jnp.log1p(-jnp.exp2(-5.0 - jnp.arange(RET_HEADS, dtype=F32)))
    lgam = jnp.repeat(log_gamma, RET_DK).reshape(1, QK_WIDTH)

    fw1 = _ffn_params(p, "ffn1")
    fw2 = _ffn_params(p, "ffn2")
    mw = jax.vmap(_prep_mixer_layer)(mixer)
    phase_lane = jnp.where(jnp.arange(LANES) < LANES // 2, 0.0, math.pi / 2).astype(F32).reshape(1, LANES)
    rope_tab = _rope_table(positions.reshape(t, 1), inv_lane, phase_lane, tm=tm)

    xf = x.reshape(t, d)
    for l in range(depth):
        xf = _ffn(xf, l, fw1, tm=tm, tf=256)
        (q, k, gq, gk, b, gv, sg, rq, rk, rv, sr, vt) = _proj(xf, rope_tab, l, mw, tri, tm=_tile(seq, 2 * tm))
        o_mla = _attention(q, k, vt, batch=batch, seq=seq, tq=tq, tk=tq, tn=_tile(tq, 256))
        o_lin = _linmix(gq, gk, b, gv, sg, rq, rk, rv, sr, l, mw["gla_norm_g"], mw["ret_norm_g"], lgam, ones_bd,
                        batch=batch, seq=seq, tc=tc)
        xf = _ffn(xf, l, fw2, tm=tm, tf=256,
                  mixer_out=(o_mla, o_lin, mw["w_out_mla"], mw["w_out_lin"], mw["mix_post_g"]))
    return xf.reshape(batch, seq, d)
```

```python
import functools
import math

import jax
import jax.numpy as jnp
from jax import lax
from jax.experimental import pallas as pl
from jax.experimental.pallas import tpu as pltpu

F32 = jnp.float32
BF16 = jnp.bfloat16

D_MODEL = 1024
D_FF = 2816
CHUNK = 64
EPS = 1e-6
ROPE_BASE = 10000.0
ROPE_DIM = 32
HALF_ROPE = ROPE_DIM // 2

MLA_HEADS = 8
MLA_Q_RANK = 256
MLA_KV_RANK = 128
MLA_NOPE = 64
MLA_V = 64
GLA_HEADS = 4
GLA_DK = 32
GLA_DV = 64
GLA_GATE_RANK = 16
GLA_TAU = 16.0
RET_HEADS = 4
RET_DK = 32
RET_DV = 64

LANES = 128
MLA_WIDTH = MLA_HEADS * MLA_V
GLA_WIDTH = GLA_HEADS * GLA_DV
RET_WIDTH = RET_HEADS * RET_DV
QK_WIDTH = GLA_HEADS * GLA_DK
HEAD_PAD = MLA_HEADS * LANES

C_MQ = 0
C_MKV = C_MQ + MLA_Q_RANK
C_KPE = C_MKV + MLA_KV_RANK
C_GQ = C_KPE + LANES
C_GK = C_GQ + QK_WIDTH
C_GV = C_GK + QK_WIDTH
C_GR = C_GV + GLA_WIDTH
C_RQ = C_GR + GLA_WIDTH
C_RK = C_RQ + QK_WIDTH
C_RV = C_RK + QK_WIDTH
C_RG = C_RV + RET_WIDTH
D_IN_EXT = C_RG + RET_WIDTH

LOG2_E = math.log2(math.e)
LOOKAHEAD = 4
KV_UNROLL = 1
V_ROWS = 80
VMEM_LIMIT = 56 * 1024 * 1024


def _rms(x, g):
    return x * lax.rsqrt(jnp.mean(x * x, axis=-1, keepdims=True) + EPS) * g


def _pack_rows(x):
    return pltpu.bitcast(x, jnp.uint32)


def _unpack_rows(x):
    return pltpu.bitcast(x, BF16)


def _silu(x):
    return x * jax.nn.sigmoid(x)


def _cparams(*sem):
    return pltpu.CompilerParams(dimension_semantics=sem, vmem_limit_bytes=VMEM_LIMIT)


def _const_spec(shape):
    nd = len(shape)
    return pl.BlockSpec(shape, lambda *_: (0,) * nd, pipeline_mode=pl.Buffered(1))


def _layer_spec(stacked, layer):
    nd = stacked.ndim
    return pl.BlockSpec((None,) + stacked.shape[1:], lambda *_: (layer,) + (0,) * (nd - 1),
                        pipeline_mode=pl.Buffered(1))


def _ffn_kernel(*refs, tf, with_mixer_out):
    if with_mixer_out:
        x_ref, om_ref, ol_ref, wa_ref, wb_ref, mpost_ref = refs[:6]
        refs = refs[6:]
        m = (lax.dot_general(om_ref[...], wa_ref[...], (((0,), (0,)), ((), ())), preferred_element_type=F32)
             + jnp.dot(ol_ref[...], wb_ref[...], preferred_element_type=F32))
        x = x_ref[...] + _rms(m, mpost_ref[...])
    else:
        x_ref = refs[0]
        refs = refs[1:]
        x = x_ref[...]
    pre_ref, post_ref, wg_ref, wu_ref, wd_ref, o_ref, h_ref = refs
    xn = _rms(x, pre_ref[...]).astype(BF16)
    for c in range(D_FF // tf):
        sl = slice(c * tf, (c + 1) * tf)
        g = jnp.dot(xn, wg_ref[:, sl].astype(BF16), preferred_element_type=F32)
        u = jnp.dot(xn, wu_ref[:, sl].astype(BF16), preferred_element_type=F32)
        h_ref[:, sl] = (_silu(g) * u).astype(BF16)
    f = jnp.dot(h_ref[...], wd_ref[...].astype(BF16), preferred_element_type=F32)
    o_ref[...] = x + 0.5 * _rms(f, post_ref[...])


def _ffn(x, layer, fw, *, tm, tf, mixer_out=None):
    t = x.shape[0]
    row = lambda w: pl.BlockSpec((tm, w), lambda i: (i, 0))
    args, specs = [x], [row(D_MODEL)]
    if mixer_out is not None:
        o_mla, o_lin, wa, wb, mpost = mixer_out
        args += [o_mla, o_lin, wa, wb, mpost]
        specs += [pl.BlockSpec((MLA_WIDTH, tm), lambda i: (0, i)), row(GLA_WIDTH + RET_WIDTH)]
        specs += [_layer_spec(a, layer) for a in (wa, wb, mpost)]
    args += list(fw)
    specs += [_layer_spec(a, layer) for a in fw]
    return pl.pallas_call(
        functools.partial(_ffn_kernel, tf=tf, with_mixer_out=mixer_out is not None),
        out_shape=jax.ShapeDtypeStruct(x.shape, F32),
        grid=(t // tm,),
        in_specs=specs,
        out_specs=row(D_MODEL),
        scratch_shapes=[pltpu.VMEM((tm, D_FF), BF16)],
        compiler_params=_cparams("parallel"),
        name="ffn_mixer_out" if mixer_out is not None else "ffn",
    )(*args)


def _rope_kernel(pos_ref, inv_ref, phase_ref, tab_ref):
    tab_ref[...] = jnp.cos(pos_ref[...].astype(F32) * inv_ref[...] - phase_ref[...])


def _rope_table(pos_col, inv_lane, phase_lane, *, tm):
    t = pos_col.shape[0]
    return pl.pallas_call(
        _rope_kernel,
        out_shape=jax.ShapeDtypeStruct((t, LANES), F32),
        grid=(t // tm,),
        in_specs=[pl.BlockSpec((tm, 1), lambda i: (i, 0)), _const_spec((1, LANES)), _const_spec((1, LANES))],
        out_specs=pl.BlockSpec((tm, LANES), lambda i: (i, 0)),
        compiler_params=_cparams("parallel"),
        name="rope_table",
    )(pos_col, inv_lane, phase_lane)


def _proj_kernel(x_ref, rope_ref, pre_ref, win_ref, qg_ref, wq_ref, kvg_ref, wk_ref, wvt_ref,
                 wgate_ref, bgate_ref, tri_ref,
                 q_ref, k_ref, gq_ref, gk_ref, b_ref, gv_ref, sg_ref,
                 rq_ref, rk_ref, rv_ref, sr_ref, vt_ref):
    tm = x_ref.shape[0]
    xn = _rms(x_ref[...], pre_ref[...]).astype(BF16)

    chunk_starts = (0, C_GQ, C_RQ, D_IN_EXT)
    chunks = [jnp.dot(xn, win_ref[:, a:b], preferred_element_type=F32)
              for a, b in zip(chunk_starts[:-1], chunk_starts[1:])]

    def proj(c0, width):
        for a, b, chunk in zip(chunk_starts[:-1], chunk_starts[1:], chunks):
            if a <= c0 and c0 + width <= b:
                return chunk[:, c0 - a:c0 - a + width]
        raise ValueError("column group straddles two chunks")

    lane = lax.broadcasted_iota(jnp.int32, (tm, LANES), 1)
    tab = rope_ref[...]
    swapped = pltpu.roll(tab, LANES // 2, axis=1)
    cosb = jnp.where(lane < LANES // 2, tab, swapped)
    sinb = jnp.where(lane < LANES // 2, swapped, tab)
    first_half = (lane % ROPE_DIM) < HALF_ROPE
    sins = jnp.where(first_half, -sinb, sinb)

    def rot_half(v):
        return jnp.where(first_half, pltpu.roll(v, LANES - HALF_ROPE, axis=1), pltpu.roll(v, HALF_ROPE, axis=1))

    hq = _rms(proj(C_MQ, MLA_Q_RANK), qg_ref[...]).astype(BF16)
    in_rope = (lane >= MLA_NOPE) & (lane < MLA_NOPE + ROPE_DIM)
    qcos = jnp.where(in_rope, cosb, 1.0)
    qsin = jnp.where(in_rope, sins, 0.0)
    q_scale = (MLA_NOPE + ROPE_DIM) ** -0.5 * LOG2_E
    qa = jnp.dot(hq, wq_ref[...], preferred_element_type=F32)
    for h in range(MLA_HEADS):
        sl = slice(h * LANES, (h + 1) * LANES)
        q_ref[:, sl] = _pack_rows(((qa[:, sl] * qcos + rot_half(qa[:, sl]) * qsin) * q_scale).astype(BF16))

    hkv = _rms(proj(C_MKV, MLA_KV_RANK), kvg_ref[...]).astype(BF16)
    g1 = proj(C_KPE, LANES)
    kpe = jnp.where(in_rope, g1 * cosb + rot_half(g1) * sins, 0.0)
    ones_row = jnp.where(lax.broadcasted_iota(jnp.int32, (V_ROWS, tm), 0) == MLA_V, 1.0, 0.0)
    kn = jnp.dot(hkv, wk_ref[...], preferred_element_type=F32)
    vt = lax.dot_general(wvt_ref[...], hkv, (((1,), (1,)), ((), ())), preferred_element_type=F32)
    for h in range(MLA_HEADS):
        sl = slice(h * LANES, (h + 1) * LANES)
        vs = slice(h * V_ROWS, (h + 1) * V_ROWS)
        k_ref[:, sl] = _pack_rows((kn[:, sl] + kpe).astype(BF16))
        vt_ref[h * V_ROWS // 2:(h + 1) * V_ROWS // 2, :] = _pack_rows((vt[vs, :] + ones_row).astype(BF16))

    z = jnp.dot(g1.astype(BF16), wgate_ref[...], preferred_element_type=F32) + bgate_ref[...]
    log_a = (jnp.minimum(z, 0.0) - jnp.log1p(jnp.exp(-jnp.abs(z)))) / GLA_TAU
    la_hi = log_a.astype(BF16)
    la_lo = (log_a - la_hi.astype(F32)).astype(BF16)
    la_split = jnp.concatenate([la_hi, la_lo], axis=1)
    tri = tri_ref[...]
    tb = tri.shape[0]
    for r in range(tm // tb):
        rs = slice(r * tb, (r + 1) * tb)
        cum = jnp.dot(tri, la_split[rs], preferred_element_type=F32)
        b_ref[rs, :] = cum[:, :QK_WIDTH] + cum[:, QK_WIDTH:]
    gq_ref[...] = (proj(C_GQ, QK_WIDTH) * (GLA_DK ** -0.5)).astype(BF16)
    gk_ref[...] = proj(C_GK, QK_WIDTH).astype(BF16)
    gv_ref[...] = proj(C_GV, GLA_WIDTH).astype(BF16)
    sg_ref[...] = _silu(proj(C_GR, GLA_WIDTH)).astype(BF16)

    rq = proj(C_RQ, QK_WIDTH)
    rk = proj(C_RK, QK_WIDTH)
    rq_ref[...] = (rq * cosb + rot_half(rq) * sins).astype(BF16)
    rk_ref[...] = ((rk * cosb + rot_half(rk) * sins) * (RET_DK ** -0.5)).astype(BF16)
    rv_ref[...] = proj(C_RV, RET_WIDTH).astype(BF16)
    sr_ref[...] = _silu(proj(C_RG, RET_WIDTH)).astype(BF16)


def _proj(x, rope_tab, layer, mw, tri, *, tm):
    t = x.shape[0]
    row = lambda w: pl.BlockSpec((tm, w), lambda i: (i, 0))
    out_widths = [(QK_WIDTH, BF16), (QK_WIDTH, BF16), (QK_WIDTH, F32), (GLA_WIDTH, BF16), (GLA_WIDTH, BF16),
                  (QK_WIDTH, BF16), (QK_WIDTH, BF16), (RET_WIDTH, BF16), (RET_WIDTH, BF16)]
    stacked = [mw["mix_pre_g"], mw["w_in"], mw["q_norm_g"], mw["wq"], mw["kv_norm_g"], mw["wk"], mw["wvt"],
               mw["w_gate"], mw["b_gate"]]
    return pl.pallas_call(
        _proj_kernel,
        out_shape=(jax.ShapeDtypeStruct((t // 2, HEAD_PAD), jnp.uint32),) * 2
        + tuple(jax.ShapeDtypeStruct((t, w), dt) for w, dt in out_widths)
        + (jax.ShapeDtypeStruct((MLA_HEADS * V_ROWS // 2, t), jnp.uint32),),
        grid=(t // tm,),
        in_specs=[row(D_MODEL), row(LANES)] + [_layer_spec(a, layer) for a in stacked]
        + [_const_spec(tri.shape)],
        out_specs=(pl.BlockSpec((tm // 2, HEAD_PAD), lambda i: (i, 0)),) * 2 + tuple(row(w) for w, _ in out_widths)
        + (pl.BlockSpec((MLA_HEADS * V_ROWS // 2, tm), lambda i: (0, i)),),
        compiler_params=_cparams("parallel"),
        name="mixer_proj",
    )(x, rope_tab, *stacked, tri)


def _attn_kernel(q_ref, k_ref, vt_ref, o_ref, m_ref, acc_ref, st_ref, *, tk, tn):
    tq = 2 * q_ref.shape[0]
    qi = pl.program_id(1)
    nt = (((1,), (1,)), ((), ()))
    items = [(h, c) for h in range(MLA_HEADS) for c in range(tq // tn)]

    def key_rows(j, n):
        return pl.ds(pl.multiple_of(j * tk, tk), n)

    def packed_key_rows(j, n):
        return pl.ds(pl.multiple_of(j * (tk // 2), tk // 2), n // 2)

    def n_keys(item, diag):
        return (item[1] + 1) * tn if diag else tk

    def scores(j, item, diag):
        h, c = item
        hs = slice(h * LANES, (h + 1) * LANES)
        keys = _unpack_rows(k_ref[packed_key_rows(j, n_keys(item, diag)), hs])
        queries = _unpack_rows(q_ref[c * tn // 2:(c + 1) * tn // 2, hs])
        return lax.dot_general(keys, queries, nt, preferred_element_type=F32).astype(BF16)

    key_chunk = lax.broadcasted_iota(jnp.int32, (tn, tn), 0) // CHUNK
    qry_chunk = lax.broadcasted_iota(jnp.int32, (tn, tn), 1) // CHUNK
    diag_mask = jnp.where(key_chunk <= qry_chunk, 0.0, -jnp.inf).astype(BF16)

    def consume(j, item, st, diag):
        h, c = item
        vs = slice(h * V_ROWS // 2, (h + 1) * V_ROWS // 2)
        cs = slice(c * tn, (c + 1) * tn)
        nk = n_keys(item, diag)
        st = st[:nk]
        if diag:
            own = st[nk - tn:] + diag_mask
            st = own if nk == tn else jnp.concatenate([st[:nk - tn], own], axis=0)
        m_old = m_ref[h, :, cs]
        m_new = jnp.maximum(m_old, jnp.max(st, axis=0, keepdims=True).astype(F32))
        p = jnp.exp2(st - m_new.astype(BF16))
        acc_ref[h, :, cs] = (acc_ref[h, :, cs] * jnp.exp2(m_old - m_new)
                             + jnp.dot(_unpack_rows(vt_ref[vs, key_rows(j, nk)]), p, preferred_element_type=F32))
        m_ref[h, :, cs] = m_new

    for h in range(MLA_HEADS):
        m_ref[h] = jnp.full((1, tq), -jnp.inf, F32)
        acc_ref[h] = jnp.zeros((V_ROWS, tq), F32)

    for a in range(LOOKAHEAD):
        st_ref[a] = scores(0, items[a], False)

    def sweep(j, diag):
        pending = [st_ref[a] for a in range(LOOKAHEAD)]
        for i, item in enumerate(items):
            ahead = i + LOOKAHEAD
            if ahead < len(items):
                pending.append(scores(j, items[ahead], diag))
            elif not diag:
                pending.append(scores(j + 1, items[ahead - len(items)], False))
            consume(j, item, pending.pop(0), diag)
        for a, st in enumerate(pending):
            st_ref[a] = st

    def body(jj, carry):
        for u in range(KV_UNROLL):
            sweep(jj * KV_UNROLL + u, False)
        return carry

    n_full = qi // KV_UNROLL
    lax.fori_loop(0, n_full, body, 0)
    lax.fori_loop(n_full * KV_UNROLL, qi, lambda j, c: (sweep(j, False), c)[1], 0)
    sweep(qi, True)

    for h in range(MLA_HEADS):
        acc = acc_ref[h]
        o_ref[h * MLA_V:(h + 1) * MLA_V, :] = (acc[:MLA_V] / acc[MLA_V:MLA_V + 1, :]).astype(BF16)


def _attention(q, k, vt, *, batch, seq, tq, tk, tn):
    nq = seq // tq
    assert tq == tk and tq % tn == 0
    return pl.pallas_call(
        functools.partial(_attn_kernel, tk=tk, tn=tn),
        out_shape=jax.ShapeDtypeStruct((MLA_WIDTH, batch * seq), BF16),
        grid=(batch, nq),
        in_specs=[
            pl.BlockSpec((tq // 2, HEAD_PAD), lambda b, i: (b * nq + i, 0)),
            pl.BlockSpec((seq // 2, HEAD_PAD), lambda b, i: (b, 0)),
            pl.BlockSpec((MLA_HEADS * V_ROWS // 2, seq), lambda b, i: (0, b)),
        ],
        out_specs=pl.BlockSpec((MLA_WIDTH, tq), lambda b, i: (0, b * nq + i)),
        scratch_shapes=[pltpu.VMEM((MLA_HEADS, 1, tq), F32), pltpu.VMEM((MLA_HEADS, V_ROWS, tq), F32),
                        pltpu.VMEM((LOOKAHEAD, tk, tn), BF16)],
        compiler_params=_cparams("parallel", "arbitrary"),
        name="mla_attention",
    )(q, k, vt)


def _head_rows(x, n_heads, head_lanes):
    lane_head = lax.broadcasted_iota(jnp.int32, x.shape, 1) // head_lanes
    return jnp.concatenate([jnp.where(lane_head == h, x, 0.0).astype(BF16) for h in range(n_heads)], axis=0)


def _linmix_kernel(gq_ref, gk_ref, b_ref, gv_ref, sg_ref, rq_ref, rk_ref, rv_ref, sr_ref,
                   gng_ref, rng_ref, lgam_ref, ones_ref, o_ref, sg_state, sr_state, og_ref, or_ref):
    tc = gq_ref.shape[0]
    L = CHUNK

    @pl.when(pl.program_id(1) == 0)
    def _():
        sg_state[...] = jnp.zeros(sg_state.shape, F32)
        sr_state[...] = jnp.zeros(sr_state.shape, F32)

    ri = lax.broadcasted_iota(jnp.int32, (L, GLA_HEADS * L), 0)
    cj = lax.broadcasted_iota(jnp.int32, (L, GLA_HEADS * L), 1) % L
    lower = cj <= ri
    lgam = lgam_ref[...]
    lgam_cols = jnp.concatenate(
        [jnp.broadcast_to(lgam[:, h * RET_DK:h * RET_DK + 1], (1, L)) for h in range(RET_HEADS)], axis=1)
    ret_decay = jnp.exp(lgam_cols * jnp.abs(ri - cj).astype(F32))
    pos = lax.broadcasted_iota(jnp.int32, (L, QK_WIDTH), 0).astype(F32)
    q_dec = jnp.exp(lgam * (pos + 1.0))
    k_dec = jnp.exp(lgam * (L - 1.0 - pos))
    chunk_dec = jnp.exp(lgam * float(L))
    state_head = lax.broadcasted_iota(jnp.int32, (GLA_WIDTH, QK_WIDTH), 0) // GLA_DV
    state_lane_head = lax.broadcasted_iota(jnp.int32, (GLA_WIDTH, QK_WIDTH), 1) // GLA_DK
    state_diag = state_head == state_lane_head
    nt = (((1,), (1,)), ((), ()))

    for c in range(tc // L):
        rs = slice(c * L, (c + 1) * L)
        b = b_ref[rs, :]
        b_last = b[L - 1:L, :]
        eb, enb = jnp.exp(b), jnp.exp(-b)
        q = gq_ref[rs, :].astype(F32)
        k = gk_ref[rs, :].astype(F32)
        v = gv_ref[rs, :].astype(F32)
        q_fwd = (q * eb).astype(BF16)
        a_low = lax.dot_general(q_fwd, _head_rows(k * enb, GLA_HEADS, GLA_DK), nt, preferred_element_type=F32)
        a_up = lax.dot_general((q * enb).astype(BF16), _head_rows(k * eb, GLA_HEADS, GLA_DK), nt,
                               preferred_element_type=F32)
        attn = jnp.where(lower, a_low, a_up).astype(BF16)
        o_intra = jnp.dot(attn, _head_rows(v, GLA_HEADS, GLA_DV), preferred_element_type=F32)
        s_t = sg_state[...]
        o_inter = lax.dot_general(q_fwd, s_t.astype(BF16), nt, preferred_element_type=F32)
        og_ref[rs, :] = o_intra + o_inter
        k_state = (k * jnp.exp(b_last - b)).astype(BF16)
        d_state = jnp.dot(v.T.astype(BF16), k_state, preferred_element_type=F32)
        sg_state[...] = s_t * jnp.exp(b_last) + jnp.where(state_diag, d_state, 0.0)

        q = rq_ref[rs, :].astype(F32)
        k = rk_ref[rs, :].astype(F32)
        v = rv_ref[rs, :].astype(F32)
        scores = lax.dot_general(q.astype(BF16), _head_rows(k, RET_HEADS, RET_DK), nt,
                                 preferred_element_type=F32) * ret_decay
        o_intra = jnp.dot(scores.astype(BF16), _head_rows(v, RET_HEADS, RET_DV), preferred_element_type=F32)
        s_t = sr_state[...]
        o_inter = lax.dot_general((q * q_dec).astype(BF16), s_t.astype(BF16), nt, preferred_element_type=F32)
        or_ref[rs, :] = o_intra + o_inter
        d_state = jnp.dot(v.T.astype(BF16), (k * k_dec).astype(BF16), preferred_element_type=F32)
        sr_state[...] = s_t * chunk_dec + jnp.where(state_diag, d_state, 0.0)

    def head_norm(o, g, gate):
        o2 = o * o
        hi = o2.astype(BF16)
        lo = (o2 - hi.astype(F32)).astype(BF16)
        ss = (jnp.dot(hi, ones_ref[...], preferred_element_type=F32)
              + jnp.dot(lo, ones_ref[...], preferred_element_type=F32))
        return o * lax.rsqrt(ss * (1.0 / GLA_DV) + EPS) * g * gate

    o_ref[:, :GLA_WIDTH] = head_norm(og_ref[...], gng_ref[...], sg_ref[...].astype(F32)).astype(BF16)
    o_ref[:, GLA_WIDTH:] = head_norm(or_ref[...], rng_ref[...], sr_ref[...].astype(F32)).astype(BF16)


def _linmix(gq, gk, b, gv, sg, rq, rk, rv, sr, layer, gng, rng, lgam, ones_bd, *, batch, seq, tc):
    nt = seq // tc
    row = lambda w: pl.BlockSpec((tc, w), lambda bb, i: (bb * nt + i, 0))
    return pl.pallas_call(
        _linmix_kernel,
        out_shape=jax.ShapeDtypeStruct((batch * seq, GLA_WIDTH + RET_WIDTH), BF16),
        grid=(batch, nt),
        in_specs=[row(QK_WIDTH), row(QK_WIDTH), row(QK_WIDTH), row(GLA_WIDTH), row(GLA_WIDTH),
                  row(QK_WIDTH), row(QK_WIDTH), row(RET_WIDTH), row(RET_WIDTH),
                  _layer_spec(gng, layer), _layer_spec(rng, layer), _const_spec(lgam.shape),
                  _const_spec(ones_bd.shape)],
        out_specs=row(GLA_WIDTH + RET_WIDTH),
        scratch_shapes=[pltpu.VMEM((GLA_WIDTH, QK_WIDTH), F32), pltpu.VMEM((RET_WIDTH, QK_WIDTH), F32),
                        pltpu.VMEM((tc, GLA_WIDTH), F32), pltpu.VMEM((tc, RET_WIDTH), F32)],
        compiler_params=_cparams("parallel", "arbitrary"),
        name="gla_retention",
    )(gq, gk, b, gv, sg, rq, rk, rv, sr, gng, rng, lgam, ones_bd)


def _prep_mixer_layer(p):
    w_in = p["w_in"]
    splits = (MLA_Q_RANK, MLA_KV_RANK, ROPE_DIM, QK_WIDTH, QK_WIDTH, GLA_WIDTH, GLA_GATE_RANK, GLA_WIDTH,
              QK_WIDTH, QK_WIDTH, RET_WIDTH, RET_WIDTH)
    offs = [0]
    for s in splits:
        offs.append(offs[-1] + s)
    mq, mkv, mkr, gq, gk, gv, gg, gr, rq, rk, rv, rg = (w_in[:, offs[i]:offs[i + 1]] for i in range(12))
    d = w_in.shape[0]
    zeros = lambda n: jnp.zeros((d, n), F32)
    kpe_grp = jnp.concatenate([gg, zeros(MLA_NOPE - GLA_GATE_RANK), mkr, zeros(LANES - MLA_NOPE - ROPE_DIM)], 1)
    w_in_ext = jnp.concatenate([mq, mkv, kpe_grp, gq, gk, gv, gr, rq, rk, rv, rg], axis=1).astype(BF16)

    wq = p["mla_w_q_up"].reshape(MLA_Q_RANK, MLA_HEADS, MLA_NOPE + ROPE_DIM)
    pad_q = jnp.zeros((MLA_Q_RANK, MLA_HEADS, LANES - MLA_NOPE - ROPE_DIM), F32)
    wq_main = jnp.concatenate([wq, pad_q], axis=-1).reshape(MLA_Q_RANK, HEAD_PAD)
    wkv = p["mla_w_kv_up"].reshape(MLA_KV_RANK, MLA_HEADS, MLA_NOPE + MLA_V)
    pad_kv = jnp.zeros((MLA_KV_RANK, MLA_HEADS, LANES - MLA_NOPE), F32)
    wk_pad = jnp.concatenate([wkv[..., :MLA_NOPE], pad_kv], axis=-1).reshape(MLA_KV_RANK, HEAD_PAD)
    pad_v = jnp.zeros((MLA_KV_RANK, MLA_HEADS, V_ROWS - MLA_V), F32)
    wv_pad = jnp.concatenate([wkv[..., MLA_NOPE:], pad_v], axis=-1).reshape(MLA_KV_RANK, MLA_HEADS * V_ROWS)
    w_gate = jnp.concatenate([p["gla_w_gate_up"], jnp.zeros((LANES - GLA_GATE_RANK, QK_WIDTH), F32)], axis=0)
    row = lambda a: a.reshape(1, -1)
    w_out = p["w_out"].astype(BF16)
    return {
        "mix_pre_g": row(p["mix_pre_g"]), "mix_post_g": row(p["mix_post_g"]),
        "w_in": w_in_ext,
        "q_norm_g": row(p["mla_q_norm_g"]),
        "wq": wq_main.astype(BF16),
        "kv_norm_g": row(p["mla_kv_norm_g"]),
        "wk": wk_pad.astype(BF16), "wvt": wv_pad.T.astype(BF16),
        "w_gate": w_gate.astype(BF16), "b_gate": row(p["gla_b_gate"]),
        "gla_norm_g": row(jnp.tile(p["gla_norm_g"], GLA_HEADS)),
        "ret_norm_g": row(jnp.tile(p["ret_norm_g"], RET_HEADS)),
        "w_out_mla": w_out[:MLA_WIDTH], "w_out_lin": w_out[MLA_WIDTH:],
    }


def _ffn_params(p, prefix):
    gain = lambda g: g.reshape(g.shape[0], 1, g.shape[1])
    return (gain(p[prefix + "_pre_g"]), gain(p[prefix + "_post_g"]),
            p[prefix + "_w_gate"], p[prefix + "_w_up"], p[prefix + "_w_down"])


def _tile(n, pref):
    while n % pref:
        pref //= 2
    return pref


def kernel(x, positions, ffn1_pre_g, ffn1_post_g, ffn1_w_gate, ffn1_w_up, ffn1_w_down, mix_pre_g, mix_post_g, w_in, mla_q_norm_g, mla_w_q_up, mla_kv_norm_g, mla_w_kv_up, gla_w_gate_up, gla_b_gate, gla_norm_g, ret_norm_g, w_out, ffn2_pre_g, ffn2_post_g, ffn2_w_gate, ffn2_w_up, ffn2_w_down):
    p = dict(ffn1_pre_g=ffn1_pre_g, ffn1_post_g=ffn1_post_g, ffn1_w_gate=ffn1_w_gate, ffn1_w_up=ffn1_w_up,
             ffn1_w_down=ffn1_w_down, ffn2_pre_g=ffn2_pre_g, ffn2_post_g=ffn2_post_g, ffn2_w_gate=ffn2_w_gate,
             ffn2_w_up=ffn2_w_up, ffn2_w_down=ffn2_w_down)
    mixer = dict(mix_pre_g=mix_pre_g, mix_post_g=mix_post_g, w_in=w_in, mla_q_norm_g=mla_q_norm_g,
                 mla_w_q_up=mla_w_q_up, mla_kv_norm_g=mla_kv_norm_g, mla_w_kv_up=mla_w_kv_up,
                 gla_w_gate_up=gla_w_gate_up, gla_b_gate=gla_b_gate, gla_norm_g=gla_norm_g,
                 ret_norm_g=ret_norm_g, w_out=w_out)
    batch, seq, d = x.shape
    t = batch * seq
    depth = w_in.shape[0]
    tm = _tile(seq, 512)
    tq = _tile(seq, 1024)
    tc = _tile(seq, 2048)

    inv = ROPE_BASE ** (-jnp.arange(0, ROPE_DIM, 2, dtype=F32) / ROPE_DIM)
    inv_lane = jnp.tile(inv, LANES // HALF_ROPE).reshape(1, LANES)
    tb = _tile(tm, 256)
    idx = jnp.arange(tb)
    tri = ((idx[:, None] >= idx[None, :]) & (idx[:, None] // CHUNK == idx[None, :] // CHUNK)).astype(BF16)
    hd = jnp.arange(GLA_WIDTH) // GLA_DV
    ones_bd = (hd[:, None] == hd[None, :]).astype(BF16)
    log_gamma = jnp.log1p(-jnp.exp2(-5.0 - jnp.arange(RET_HEADS, dtype=F32)))
    lgam = jnp.repeat(log_gamma, RET_DK).reshape(1, QK_WIDTH)

    fw1 = _ffn_params(p, "ffn1")
    fw2 = _ffn_params(p, "ffn2")
    mw = jax.vmap(_prep_mixer_layer)(mixer)
    phase_lane = jnp.where(jnp.arange(LANES) < LANES // 2, 0.0, math.pi / 2).astype(F32).reshape(1, LANES)
    rope_tab = _rope_table(positions.reshape(t, 1), inv_lane, phase_lane, tm=tm)

    xf = x.reshape(t, d)
    for l in range(depth):
        xf = _ffn(xf, l, fw1, tm=tm, tf=256)
        (q, k, gq, gk, b, gv, sg, rq, rk, rv, sr, vt) = _proj(xf, rope_tab, l, mw, tri, tm=_tile(seq, 2 * tm))
        o_mla = _attention(q, k, vt, batch=batch, seq=seq, tq=tq, tk=tq, tn=_tile(tq, 256))
        o_lin = _linmix(gq, gk, b, gv, sg, rq, rk, rv, sr, l, mw["gla_norm_g"], mw["ret_norm_g"], lgam, ones_bd,
                        batch=batch, seq=seq, tc=tc)
        xf = _ffn(xf, l, fw2, tm=tm, tf=256,
                  mixer_out=(o_mla, o_lin, mw["w_out_mla"], mw["w_out_lin"], mw["mix_post_g"]))
    return xf.reshape(batch, seq, d)
```

```python
import functools
import math

import jax
import jax.numpy as jnp
from jax import lax
from jax.experimental import pallas as pl
from jax.experimental.pallas import tpu as pltpu

F32 = jnp.float32
BF16 = jnp.bfloat16

D_MODEL = 1024
D_FF = 2816
CHUNK = 64
EPS = 1e-6
ROPE_BASE = 10000.0
ROPE_DIM = 32
HALF_ROPE = ROPE_DIM // 2

MLA_HEADS = 8
MLA_Q_RANK = 256
MLA_KV_RANK = 128
MLA_NOPE = 64
MLA_V = 64
GLA_HEADS = 4
GLA_DK = 32
GLA_DV = 64
GLA_GATE_RANK = 16
GLA_TAU = 16.0
RET_HEADS = 4
RET_DK = 32
RET_DV = 64

LANES = 128
MLA_WIDTH = MLA_HEADS * MLA_V
GLA_WIDTH = GLA_HEADS * GLA_DV
RET_WIDTH = RET_HEADS * RET_DV
QK_WIDTH = GLA_HEADS * GLA_DK
HEAD_PAD = MLA_HEADS * LANES

C_MQ = 0
C_MKV = C_MQ + MLA_Q_RANK
C_KPE = C_MKV + MLA_KV_RANK
C_GQ = C_KPE + LANES
C_GK = C_GQ + QK_WIDTH
C_GV = C_GK + QK_WIDTH
C_GR = C_GV + GLA_WIDTH
C_RQ = C_GR + GLA_WIDTH
C_RK = C_RQ + QK_WIDTH
C_RV = C_RK + QK_WIDTH
C_RG = C_RV + RET_WIDTH
D_IN_EXT = C_RG + RET_WIDTH

LOG2_E = math.log2(math.e)
LOOKAHEAD = 4
KV_UNROLL = 1
V_ROWS = 80
VMEM_LIMIT = 56 * 1024 * 1024


def _rms(x, g):
    return x * lax.rsqrt(jnp.mean(x * x, axis=-1, keepdims=True) + EPS) * g


def _pack_rows(x):
    return pltpu.bitcast(x, jnp.uint32)


def _unpack_rows(x):
    return pltpu.bitcast(x, BF16)


def _silu(x):
    return x * jax.nn.sigmoid(x)


def _cparams(*sem):
    return pltpu.CompilerParams(dimension_semantics=sem, vmem_limit_bytes=VMEM_LIMIT)


def _const_spec(shape):
    nd = len(shape)
    return pl.BlockSpec(shape, lambda *_: (0,) * nd, pipeline_mode=pl.Buffered(1))


def _layer_spec(stacked, layer):
    nd = stacked.ndim
    return pl.BlockSpec((None,) + stacked.shape[1:], lambda *_: (layer,) + (0,) * (nd - 1),
                        pipeline_mode=pl.Buffered(1))


def _ffn_kernel(*refs, tf, with_mixer_out):
    if with_mixer_out:
        x_ref, om_ref, ol_ref, wa_ref, wb_ref, mpost_ref = refs[:6]
        refs = refs[6:]
        m = (lax.dot_general(om_ref[...], wa_ref[...], (((0,), (0,)), ((), ())), preferred_element_type=F32)
             + jnp.dot(ol_ref[...], wb_ref[...], preferred_element_type=F32))
        x = x_ref[...] + _rms(m, mpost_ref[...])
    else:
        x_ref = refs[0]
        refs = refs[1:]
        x = x_ref[...]
    pre_ref, post_ref, wg_ref, wu_ref, wd_ref, o_ref, h_ref = refs
    xn = _rms(x, pre_ref[...]).astype(BF16)
    for c in range(D_FF // tf):
        sl = slice(c * tf, (c + 1) * tf)
        g = jnp.dot(xn, wg_ref[:, sl].astype(BF16), preferred_element_type=F32)
        u = jnp.dot(xn, wu_ref[:, sl].astype(BF16), preferred_element_type=F32)
        h_ref[:, sl] = (_silu(g) * u).astype(BF16)
    f = jnp.dot(h_ref[...], wd_ref[...].astype(BF16), preferred_element_type=F32)
    o_ref[...] = x + 0.5 * _rms(f, post_ref[...])


def _ffn(x, layer, fw, *, tm, tf, mixer_out=None):
    t = x.shape[0]
    row = lambda w: pl.BlockSpec((tm, w), lambda i: (i, 0))
    args, specs = [x], [row(D_MODEL)]
    if mixer_out is not None:
        o_mla, o_lin, wa, wb, mpost = mixer_out
        args += [o_mla, o_lin, wa, wb, mpost]
        specs += [pl.BlockSpec((MLA_WIDTH, tm), lambda i: (0, i)), row(GLA_WIDTH + RET_WIDTH)]
        specs += [_layer_spec(a, layer) for a in (wa, wb, mpost)]
    args += list(fw)
    specs += [_layer_spec(a, layer) for a in fw]
    return pl.pallas_call(
        functools.partial(_ffn_kernel, tf=tf, with_mixer_out=mixer_out is not None),
        out_shape=jax.ShapeDtypeStruct(x.shape, F32),
        grid=(t // tm,),
        in_specs=specs,
        out_specs=row(D_MODEL),
        scratch_shapes=[pltpu.VMEM((tm, D_FF), BF16)],
        compiler_params=_cparams("parallel"),
        name="ffn_mixer_out" if mixer_out is not None else "ffn",
    )(*args)


def _rope_kernel(pos_ref, inv_ref, phase_ref, tab_ref, cos_step_ref, sin_step_ref):
    tm = pos_ref.shape[0]
    inv = inv_ref[...]
    phase = phase_ref[...]
    step = lax.broadcasted_iota(jnp.int32, (tm, 1), 0)

    @pl.when(pl.program_id(0) == 0)
    def _():
        ang = step.astype(F32) * inv
        cos_step_ref[...] = jnp.cos(ang)
        sin_step_ref[...] = jnp.sin(ang)

    pos = pos_ref[...]
    first = pos[0:1, :]
    consecutive = jnp.min(jnp.where(pos == first + step, 1, 0))

    @pl.when(consecutive == 1)
    def _():
        base = first.astype(F32) * inv - phase
        tab_ref[...] = jnp.cos(base) * cos_step_ref[...] - jnp.sin(base) * sin_step_ref[...]

    @pl.when(consecutive != 1)
    def _():
        tab_ref[...] = jnp.cos(pos.astype(F32) * inv - phase)


def _rope_table(pos_col, inv_lane, phase_lane, *, tm):
    t = pos_col.shape[0]
    return pl.pallas_call(
        _rope_kernel,
        out_shape=jax.ShapeDtypeStruct((t, LANES), F32),
        grid=(t // tm,),
        in_specs=[pl.BlockSpec((tm, 1), lambda i: (i, 0)), _const_spec((1, LANES)), _const_spec((1, LANES))],
        out_specs=pl.BlockSpec((tm, LANES), lambda i: (i, 0)),
        scratch_shapes=[pltpu.VMEM((tm, LANES), F32), pltpu.VMEM((tm, LANES), F32)],
        compiler_params=_cparams("arbitrary"),
        name="rope_table",
    )(pos_col, inv_lane, phase_lane)


def _proj_kernel(x_ref, rope_ref, pre_ref, win_ref, qg_ref, wq_ref, kvg_ref, wk_ref, wvt_ref,
                 wgate_ref, bgate_ref, tri_ref,
                 q_ref, k_ref, gq_ref, gk_ref, b_ref, gv_ref, sg_ref,
                 rq_ref, rk_ref, rv_ref, sr_ref, vt_ref):
    tm = x_ref.shape[0]
    xn = _rms(x_ref[...], pre_ref[...]).astype(BF16)

    chunk_starts = (0, C_GQ, C_RQ, D_IN_EXT)
    chunks = [jnp.dot(xn, win_ref[:, a:b], preferred_element_type=F32)
              for a, b in zip(chunk_starts[:-1], chunk_starts[1:])]

    def proj(c0, width):
        for a, b, chunk in zip(chunk_starts[:-1], chunk_starts[1:], chunks):
            if a <= c0 and c0 + width <= b:
                return chunk[:, c0 - a:c0 - a + width]
        raise ValueError("column group straddles two chunks")

    lane = lax.broadcasted_iota(jnp.int32, (tm, LANES), 1)
    tab = rope_ref[...]
    swapped = pltpu.roll(tab, LANES // 2, axis=1)
    cosb = jnp.where(lane < LANES // 2, tab, swapped)
    sinb = jnp.where(lane < LANES // 2, swapped, tab)
    first_half = (lane % ROPE_DIM) < HALF_ROPE
    sins = jnp.where(first_half, -sinb, sinb)

    def rot_half(v):
        return jnp.where(first_half, pltpu.roll(v, LANES - HALF_ROPE, axis=1), pltpu.roll(v, HALF_ROPE, axis=1))

    hq = _rms(proj(C_MQ, MLA_Q_RANK), qg_ref[...]).astype(BF16)
    in_rope = (lane >= MLA_NOPE) & (lane < MLA_NOPE + ROPE_DIM)
    qcos = jnp.where(in_rope, cosb, 1.0)
    qsin = jnp.where(in_rope, sins, 0.0)
    q_scale = (MLA_NOPE + ROPE_DIM) ** -0.5 * LOG2_E
    qa = jnp.dot(hq, wq_ref[...], preferred_element_type=F32)
    for h in range(MLA_HEADS):
        sl = slice(h * LANES, (h + 1) * LANES)
        q_ref[:, sl] = _pack_rows(((qa[:, sl] * qcos + rot_half(qa[:, sl]) * qsin) * q_scale).astype(BF16))

    hkv = _rms(proj(C_MKV, MLA_KV_RANK), kvg_ref[...]).astype(BF16)
    g1 = proj(C_KPE, LANES)
    kpe = jnp.where(in_rope, g1 * cosb + rot_half(g1) * sins, 0.0)
    ones_row = jnp.where(lax.broadcasted_iota(jnp.int32, (V_ROWS, tm), 0) == MLA_V, 1.0, 0.0)
    kn = jnp.dot(hkv, wk_ref[...], preferred_element_type=F32)
    vt = lax.dot_general(wvt_ref[...], hkv, (((1,), (1,)), ((), ())), preferred_element_type=F32)
    for h in range(MLA_HEADS):
        sl = slice(h * LANES, (h + 1) * LANES)
        vs = slice(h * V_ROWS, (h + 1) * V_ROWS)
        k_ref[:, sl] = _pack_rows((kn[:, sl] + kpe).astype(BF16))
        vt_ref[h * V_ROWS // 2:(h + 1) * V_ROWS // 2, :] = _pack_rows((vt[vs, :] + ones_row).astype(BF16))

    z = jnp.dot(g1.astype(BF16), wgate_ref[...], preferred_element_type=F32) + bgate_ref[...]
    log_a = (jnp.minimum(z, 0.0) - jnp.log1p(jnp.exp(-jnp.abs(z)))) / GLA_TAU
    la_hi = log_a.astype(BF16)
    la_lo = (log_a - la_hi.astype(F32)).astype(BF16)
    la_split = jnp.concatenate([la_hi, la_lo], axis=1)
    tri = tri_ref[...]
    tb = tri.shape[0]
    for r in range(tm // tb):
        rs = slice(r * tb, (r + 1) * tb)
        cum = jnp.dot(tri, la_split[rs], preferred_element_type=F32)
        b_ref[rs, :] = cum[:, :QK_WIDTH] + cum[:, QK_WIDTH:]
    gq_ref[...] = (proj(C_GQ, QK_WIDTH) * (GLA_DK ** -0.5)).astype(BF16)
    gk_ref[...] = proj(C_GK, QK_WIDTH).astype(BF16)
    gv_ref[...] = proj(C_GV, GLA_WIDTH).astype(BF16)
    sg_ref[...] = _silu(proj(C_GR, GLA_WIDTH)).astype(BF16)

    rq = proj(C_RQ, QK_WIDTH)
    rk = proj(C_RK, QK_WIDTH)
    rq_ref[...] = (rq * cosb + rot_half(rq) * sins).astype(BF16)
    rk_ref[...] = ((rk * cosb + rot_half(rk) * sins) * (RET_DK ** -0.5)).astype(BF16)
    rv_ref[...] = proj(C_RV, RET_WIDTH).astype(BF16)
    sr_ref[...] = _silu(proj(C_RG, RET_WIDTH)).astype(BF16)


def _proj(x, rope_tab, layer, mw, tri, *, tm):
    t = x.shape[0]
    row = lambda w: pl.BlockSpec((tm, w), lambda i: (i, 0))
    out_widths = [(QK_WIDTH, BF16), (QK_WIDTH, BF16), (QK_WIDTH, F32), (GLA_WIDTH, BF16), (GLA_WIDTH, BF16),
                  (QK_WIDTH, BF16), (QK_WIDTH, BF16), (RET_WIDTH, BF16), (RET_WIDTH, BF16)]
    stacked = [mw["mix_pre_g"], mw["w_in"], mw["q_norm_g"], mw["wq"], mw["kv_norm_g"], mw["wk"], mw["wvt"],
               mw["w_gate"], mw["b_gate"]]
    return pl.pallas_call(
        _proj_kernel,
        out_shape=(jax.ShapeDtypeStruct((t // 2, HEAD_PAD), jnp.uint32),) * 2
        + tuple(jax.ShapeDtypeStruct((t, w), dt) for w, dt in out_widths)
        + (jax.ShapeDtypeStruct((MLA_HEADS * V_ROWS // 2, t), jnp.uint32),),
        grid=(t // tm,),
        in_specs=[row(D_MODEL), row(LANES)] + [_layer_spec(a, layer) for a in stacked]
        + [_const_spec(tri.shape)],
        out_specs=(pl.BlockSpec((tm // 2, HEAD_PAD), lambda i: (i, 0)),) * 2 + tuple(row(w) for w, _ in out_widths)
        + (pl.BlockSpec((MLA_HEADS * V_ROWS // 2, tm), lambda i: (0, i)),),
        compiler_params=_cparams("parallel"),
        name="mixer_proj",
    )(x, rope_tab, *stacked, tri)


def _attn_kernel(q_ref, k_ref, vt_ref, o_ref, m_ref, acc_ref, st_ref, *, tk, tn):
    tq = 2 * q_ref.shape[0]
    qi = pl.program_id(1)
    nt = (((1,), (1,)), ((), ()))
    items = [(h, c) for h in range(MLA_HEADS) for c in range(tq // tn)]

    def key_rows(j, n):
        return pl.ds(pl.multiple_of(j * tk, tk), n)

    def packed_key_rows(j, n):
        return pl.ds(pl.multiple_of(j * (tk // 2), tk // 2), n // 2)

    def n_keys(item, diag):
        return (item[1] + 1) * tn if diag else tk

    def scores(j, item, diag):
        h, c = item
        hs = slice(h * LANES, (h + 1) * LANES)
        keys = _unpack_rows(k_ref[packed_key_rows(j, n_keys(item, diag)), hs])
        queries = _unpack_rows(q_ref[c * tn // 2:(c + 1) * tn // 2, hs])
        return lax.dot_general(keys, queries, nt, preferred_element_type=F32).astype(BF16)

    key_chunk = lax.broadcasted_iota(jnp.int32, (tn, tn), 0) // CHUNK
    qry_chunk = lax.broadcasted_iota(jnp.int32, (tn, tn), 1) // CHUNK
    diag_mask = jnp.where(key_chunk <= qry_chunk, 0.0, -jnp.inf).astype(BF16)

    def consume(j, item, st, diag):
        h, c = item
        vs = slice(h * V_ROWS // 2, (h + 1) * V_ROWS // 2)
        cs = slice(c * tn, (c + 1) * tn)
        nk = n_keys(item, diag)
        st = st[:nk]
        if diag:
            own = st[nk - tn:] + diag_mask
            st = own if nk == tn else jnp.concatenate([st[:nk - tn], own], axis=0)
        m_old = m_ref[h, :, cs]
        m_new = jnp.maximum(m_old, jnp.max(st, axis=0, keepdims=True).astype(F32))
        p = jnp.exp2(st - m_new.astype(BF16))
        acc_ref[h, :, cs] = (acc_ref[h, :, cs] * jnp.exp2(m_old - m_new)
                             + jnp.dot(_unpack_rows(vt_ref[vs, key_rows(j, nk)]), p, preferred_element_type=F32))
        m_ref[h, :, cs] = m_new

    for h in range(MLA_HEADS):
        m_ref[h] = jnp.full((1, tq), -jnp.inf, F32)
        acc_ref[h] = jnp.zeros((V_ROWS, tq), F32)

    for a in range(LOOKAHEAD):
        st_ref[a] = scores(0, items[a], False)

    def sweep(j, diag):
        pending = [st_ref[a] for a in range(LOOKAHEAD)]
        for i, item in enumerate(items):
            ahead = i + LOOKAHEAD
            if ahead < len(items):
                pending.append(scores(j, items[ahead], diag))
            elif not diag:
                pending.append(scores(j + 1, items[ahead - len(items)], False))
            consume(j, item, pending.pop(0), diag)
        for a, st in enumerate(pending):
            st_ref[a] = st

    def body(jj, carry):
        for u in range(KV_UNROLL):
            sweep(jj * KV_UNROLL + u, False)
        return carry

    n_full = qi // KV_UNROLL
    lax.fori_loop(0, n_full, body, 0)
    lax.fori_loop(n_full * KV_UNROLL, qi, lambda j, c: (sweep(j, False), c)[1], 0)
    sweep(qi, True)

    for h in range(MLA_HEADS):
        acc = acc_ref[h]
        o_ref[h * MLA_V:(h + 1) * MLA_V, :] = (acc[:MLA_V] / acc[MLA_V:MLA_V + 1, :]).astype(BF16)


def _attention(q, k, vt, *, batch, seq, tq, tk, tn):
    nq = seq // tq
    assert tq == tk and tq % tn == 0
    return pl.pallas_call(
        functools.partial(_attn_kernel, tk=tk, tn=tn),
        out_shape=jax.ShapeDtypeStruct((MLA_WIDTH, batch * seq), BF16),
        grid=(batch, nq),
        in_specs=[
            pl.BlockSpec((tq // 2, HEAD_PAD), lambda b, i: (b * nq + i, 0)),
            pl.BlockSpec((seq // 2, HEAD_PAD), lambda b, i: (b, 0)),
            pl.BlockSpec((MLA_HEADS * V_ROWS // 2, seq), lambda b, i: (0, b)),
        ],
        out_specs=pl.BlockSpec((MLA_WIDTH, tq), lambda b, i: (0, b * nq + i)),
        scratch_shapes=[pltpu.VMEM((MLA_HEADS, 1, tq), F32), pltpu.VMEM((MLA_HEADS, V_ROWS, tq), F32),
                        pltpu.VMEM((LOOKAHEAD, tk, tn), BF16)],
        compiler_params=_cparams("parallel", "arbitrary"),
        name="mla_attention",
    )(q, k, vt)


def _head_rows(x, n_heads, head_lanes):
    lane_head = lax.broadcasted_iota(jnp.int32, x.shape, 1) // head_lanes
    return jnp.concatenate([jnp.where(lane_head == h, x, 0.0).astype(BF16) for h in range(n_heads)], axis=0)


def _linmix_kernel(gq_ref, gk_ref, b_ref, gv_ref, sg_ref, rq_ref, rk_ref, rv_ref, sr_ref,
                   gng_ref, rng_ref, lgam_ref, ones_ref, o_ref, sg_state, sr_state, og_ref, or_ref):
    tc = gq_ref.shape[0]
    L = CHUNK

    @pl.when(pl.program_id(1) == 0)
    def _():
        sg_state[...] = jnp.zeros(sg_state.shape, F32)
        sr_state[...] = jnp.zeros(sr_state.shape, F32)

    ri = lax.broadcasted_iota(jnp.int32, (L, GLA_HEADS * L), 0)
    cj = lax.broadcasted_iota(jnp.int32, (L, GLA_HEADS * L), 1) % L
    lower = cj <= ri
    lgam = lgam_ref[...]
    lgam_cols = jnp.concatenate(
        [jnp.broadcast_to(lgam[:, h * RET_DK:h * RET_DK + 1], (1, L)) for h in range(RET_HEADS)], axis=1)
    ret_decay = jnp.exp(lgam_cols * jnp.abs(ri - cj).astype(F32))
    pos = lax.broadcasted_iota(jnp.int32, (L, QK_WIDTH), 0).astype(F32)
    q_dec = jnp.exp(lgam * (pos + 1.0))
    k_dec = jnp.exp(lgam * (L - 1.0 - pos))
    chunk_dec = jnp.exp(lgam * float(L))
    state_head = lax.broadcasted_iota(jnp.int32, (GLA_WIDTH, QK_WIDTH), 0) // GLA_DV
    state_lane_head = lax.broadcasted_iota(jnp.int32, (GLA_WIDTH, QK_WIDTH), 1) // GLA_DK
    state_diag = state_head == state_lane_head
    nt = (((1,), (1,)), ((), ()))

    for c in range(tc // L):
        rs = slice(c * L, (c + 1) * L)
        b = b_ref[rs, :]
        b_last = b[L - 1:L, :]
        eb, enb = jnp.exp(b), jnp.exp(-b)
        q = gq_ref[rs, :].astype(F32)
        k = gk_ref[rs, :].astype(F32)
        v = gv_ref[rs, :].astype(F32)
        q_fwd = (q * eb).astype(BF16)
        a_low = lax.dot_general(q_fwd, _head_rows(k * enb, GLA_HEADS, GLA_DK), nt, preferred_element_type=F32)
        a_up = lax.dot_general((q * enb).astype(BF16), _head_rows(k * eb, GLA_HEADS, GLA_DK), nt,
                               preferred_element_type=F32)
        attn = jnp.where(lower, a_low, a_up).astype(BF16)
        o_intra = jnp.dot(attn, _head_rows(v, GLA_HEADS, GLA_DV), preferred_element_type=F32)
        s_t = sg_state[...]
        o_inter = lax.dot_general(q_fwd, s_t.astype(BF16), nt, preferred_element_type=F32)
        og_ref[rs, :] = o_intra + o_inter
        k_state = (k * jnp.exp(b_last - b)).astype(BF16)
        d_state = jnp.dot(v.T.astype(BF16), k_state, preferred_element_type=F32)
        sg_state[...] = s_t * jnp.exp(b_last) + jnp.where(state_diag, d_state, 0.0)

        q = rq_ref[rs, :].astype(F32)
        k = rk_ref[rs, :].astype(F32)
        v = rv_ref[rs, :].astype(F32)
        scores = lax.dot_general(q.astype(BF16), _head_rows(k, RET_HEADS, RET_DK), nt,
                                 preferred_element_type=F32) * ret_decay
        o_intra = jnp.dot(scores.astype(BF16), _head_rows(v, RET_HEADS, RET_DV), preferred_element_type=F32)
        s_t = sr_state[...]
        o_inter = lax.dot_general((q * q_dec).astype(BF16), s_t.astype(BF16), nt, preferred_element_type=F32)
        or_ref[rs, :] = o_intra + o_inter
        d_state = jnp.dot(v.T.astype(BF16), (k * k_dec).astype(BF16), preferred_element_type=F32)
        sr_state[...] = s_t * chunk_dec + jnp.where(state_diag, d_state, 0.0)

    def head_norm(o, g, gate):
        o2 = o * o
        hi = o2.astype(BF16)
        lo = (o2 - hi.astype(F32)).astype(BF16)
        ss = (jnp.dot(hi, ones_ref[...], preferred_element_type=F32)
              + jnp.dot(lo, ones_ref[...], preferred_element_type=F32))
        return o * lax.rsqrt(ss * (1.0 / GLA_DV) + EPS) * g * gate

    o_ref[:, :GLA_WIDTH] = head_norm(og_ref[...], gng_ref[...], sg_ref[...].astype(F32)).astype(BF16)
    o_ref[:, GLA_WIDTH:] = head_norm(or_ref[...], rng_ref[...], sr_ref[...].astype(F32)).astype(BF16)


def _linmix(gq, gk, b, gv, sg, rq, rk, rv, sr, layer, gng, rng, lgam, ones_bd, *, batch, seq, tc):
    nt = seq // tc
    row = lambda w: pl.BlockSpec((tc, w), lambda bb, i: (bb * nt + i, 0))
    return pl.pallas_call(
        _linmix_kernel,
        out_shape=jax.ShapeDtypeStruct((batch * seq, GLA_WIDTH + RET_WIDTH), BF16),
        grid=(batch, nt),
        in_specs=[row(QK_WIDTH), row(QK_WIDTH), row(QK_WIDTH), row(GLA_WIDTH), row(GLA_WIDTH),
                  row(QK_WIDTH), row(QK_WIDTH), row(RET_WIDTH), row(RET_WIDTH),
                  _layer_spec(gng, layer), _layer_spec(rng, layer), _const_spec(lgam.shape),
                  _const_spec(ones_bd.shape)],
        out_specs=row(GLA_WIDTH + RET_WIDTH),
        scratch_shapes=[pltpu.VMEM((GLA_WIDTH, QK_WIDTH), F32), pltpu.VMEM((RET_WIDTH, QK_WIDTH), F32),
                        pltpu.VMEM((tc, GLA_WIDTH), F32), pltpu.VMEM((tc, RET_WIDTH), F32)],
        compiler_params=_cparams("parallel", "arbitrary"),
        name="gla_retention",
    )(gq, gk, b, gv, sg, rq, rk, rv, sr, gng, rng, lgam, ones_bd)


def _prep_mixer_layer(p):
    w_in = p["w_in"]
    splits = (MLA_Q_RANK, MLA_KV_RANK, ROPE_DIM, QK_WIDTH, QK_WIDTH, GLA_WIDTH, GLA_GATE_RANK, GLA_WIDTH,
              QK_WIDTH, QK_WIDTH, RET_WIDTH, RET_WIDTH)
    offs = [0]
    for s in splits:
        offs.append(offs[-1] + s)
    mq, mkv, mkr, gq, gk, gv, gg, gr, rq, rk, rv, rg = (w_in[:, offs[i]:offs[i + 1]] for i in range(12))
    d = w_in.shape[0]
    zeros = lambda n: jnp.zeros((d, n), F32)
    kpe_grp = jnp.concatenate([gg, zeros(MLA_NOPE - GLA_GATE_RANK), mkr, zeros(LANES - MLA_NOPE - ROPE_DIM)], 1)
    w_in_ext = jnp.concatenate([mq, mkv, kpe_grp, gq, gk, gv, gr, rq, rk, rv, rg], axis=1).astype(BF16)

    wq = p["mla_w_q_up"].reshape(MLA_Q_RANK, MLA_HEADS, MLA_NOPE + ROPE_DIM)
    pad_q = jnp.zeros((MLA_Q_RANK, MLA_HEADS, LANES - MLA_NOPE - ROPE_DIM), F32)
    wq_main = jnp.concatenate([wq, pad_q], axis=-1).reshape(MLA_Q_RANK, HEAD_PAD)
    wkv = p["mla_w_kv_up"].reshape(MLA_KV_RANK, MLA_HEADS, MLA_NOPE + MLA_V)
    pad_kv = jnp.zeros((MLA_KV_RANK, MLA_HEADS, LANES - MLA_NOPE), F32)
    wk_pad = jnp.concatenate([wkv[..., :MLA_NOPE], pad_kv], axis=-1).reshape(MLA_KV_RANK, HEAD_PAD)
    pad_v = jnp.zeros((MLA_KV_RANK, MLA_HEADS, V_ROWS - MLA_V), F32)
    wv_pad = jnp.concatenate([wkv[..., MLA_NOPE:], pad_v], axis=-1).reshape(MLA_KV_RANK, MLA_HEADS * V_ROWS)
    w_gate = jnp.concatenate([p["gla_w_gate_up"], jnp.zeros((LANES - GLA_GATE_RANK, QK_WIDTH), F32)], axis=0)
    row = lambda a: a.reshape(1, -1)
    w_out = p["w_out"].astype(BF16)
    return {
        "mix_pre_g": row(p["mix_pre_g"]), "mix_post_g": row(p["mix_post_g"]),
        "w_in": w_in_ext,
        "q_norm_g": row(p["mla_q_norm_g"]),
        "wq": wq_main.astype(BF16),
        "kv_norm_g": row(p["mla_kv_norm_g"]),
        "wk": wk_pad.astype(BF16), "wvt": wv_pad.T.astype(BF16),
        "w_gate": w_gate.astype(BF16), "b_gate": row(p["gla_b_gate"]),
        "gla_norm_g": row(jnp.tile(p["gla_norm_g"], GLA_HEADS)),
        "ret_norm_g": row(jnp.tile(p["ret_norm_g"], RET_HEADS)),
        "w_out_mla": w_out[:MLA_WIDTH], "w_out_lin": w_out[MLA_WIDTH:],
    }


def _ffn_params(p, prefix):
    gain = lambda g: g.reshape(g.shape[0], 1, g.shape[1])
    return (gain(p[prefix + "_pre_g"]), gain(p[prefix + "_post_g"]),
            p[prefix + "_w_gate"], p[prefix + "_w_up"], p[prefix + "_w_down"])


def _tile(n, pref):
    while n % pref:
        pref //= 2
    return pref


def kernel(x, positions, ffn1_pre_g, ffn1_post_g, ffn1_w_gate, ffn1_w_up, ffn1_w_down, mix_pre_g, mix_post_g, w_in, mla_q_norm_g, mla_w_q_up, mla_kv_norm_g, mla_w_kv_up, gla_w_gate_up, gla_b_gate, gla_norm_g, ret_norm_g, w_out, ffn2_pre_g, ffn2_post_g, ffn2_w_gate, ffn2_w_up, ffn2_w_down):
    p = dict(ffn1_pre_g=ffn1_pre_g, ffn1_post_g=ffn1_post_g, ffn1_w_gate=ffn1_w_gate, ffn1_w_up=ffn1_w_up,
             ffn1_w_down=ffn1_w_down, ffn2_pre_g=ffn2_pre_g, ffn2_post_g=ffn2_post_g, ffn2_w_gate=ffn2_w_gate,
             ffn2_w_up=ffn2_w_up, ffn2_w_down=ffn2_w_down)
    mixer = dict(mix_pre_g=mix_pre_g, mix_post_g=mix_post_g, w_in=w_in, mla_q_norm_g=mla_q_norm_g,
                 mla_w_q_up=mla_w_q_up, mla_kv_norm_g=mla_kv_norm_g, mla_w_kv_up=mla_w_kv_up,
                 gla_w_gate_up=gla_w_gate_up, gla_b_gate=gla_b_gate, gla_norm_g=gla_norm_g,
                 ret_norm_g=ret_norm_g, w_out=w_out)
    batch, seq, d = x.shape
    t = batch * seq
    depth = w_in.shape[0]
    tm = _tile(seq, 512)
    tq = _tile(seq, 1024)
    tc = _tile(seq, 2048)

    inv = ROPE_BASE ** (-jnp.arange(0, ROPE_DIM, 2, dtype=F32) / ROPE_DIM)
    inv_lane = jnp.tile(inv, LANES // HALF_ROPE).reshape(1, LANES)
    tb = _tile(tm, 256)
    idx = jnp.arange(tb)
    tri = ((idx[:, None] >= idx[None, :]) & (idx[:, None] // CHUNK == idx[None, :] // CHUNK)).astype(BF16)
    hd = jnp.arange(GLA_WIDTH) // GLA_DV
    ones_bd = (hd[:, None] == hd[None, :]).astype(BF16)
    log_gamma = jnp.log1p(-jnp.exp2(-5.0 - jnp.arange(RET_HEADS, dtype=F32)))
    lgam = jnp.repeat(log_gamma, RET_DK).reshape(1, QK_WIDTH)

    fw1 = _ffn_params(p, "ffn1")
    fw2 = _ffn_params(p, "ffn2")
    mw = jax.vmap(_prep_mixer_layer)(mixer)
    phase_lane = jnp.where(jnp.arange(LANES) < LANES // 2, 0.0, math.pi / 2).astype(F32).reshape(1, LANES)
    rope_tab = _rope_table(positions.reshape(t, 1), inv_lane, phase_lane, tm=tm)

    xf = x.reshape(t, d)
    for l in range(depth):
        xf = _ffn(xf, l, fw1, tm=tm, tf=256)
        (q, k, gq, gk, b, gv, sg, rq, rk, rv, sr, vt) = _proj(xf, rope_tab, l, mw, tri, tm=_tile(seq, 2 * tm))
        o_mla = _attention(q, k, vt, batch=batch, seq=seq, tq=tq, tk=tq, tn=_tile(tq, 256))
        o_lin = _linmix(gq, gk, b, gv, sg, rq, rk, rv, sr, l, mw["gla_norm_g"], mw["ret_norm_g"], lgam, ones_bd,
                        batch=batch, seq=seq, tc=tc)
        xf = _ffn(xf, l, fw2, tm=tm, tf=256,
                  mixer_out=(o_mla, o_lin, mw["w_out_mla"], mw["w_out_lin"], mw["mix_post_g"]))
    return xf.reshape(batch, seq, d)
```

```python
import functools
import math

import jax
import jax.numpy as jnp
from jax import lax
from jax.experimental import pallas as pl
from jax.experimental.pallas import tpu as pltpu

F32 = jnp.float32
BF16 = jnp.bfloat16

D_MODEL = 1024
D_FF = 2816
CHUNK = 64
EPS = 1e-6
ROPE_BASE = 10000.0
ROPE_DIM = 32
HALF_ROPE = ROPE_DIM // 2

MLA_HEADS = 8
MLA_Q_RANK = 256
MLA_KV_RANK = 128
MLA_NOPE = 64
MLA_V = 64
GLA_HEADS = 4
GLA_DK = 32
GLA_DV = 64
GLA_GATE_RANK = 16
GLA_TAU = 16.0
RET_HEADS = 4
RET_DK = 32
RET_DV = 64

LANES = 128
MLA_WIDTH = MLA_HEADS * MLA_V
GLA_WIDTH = GLA_HEADS * GLA_DV
RET_WIDTH = RET_HEADS * RET_DV
QK_WIDTH = GLA_HEADS * GLA_DK
HEAD_PAD = MLA_HEADS * LANES

C_MQ = 0
C_MKV = C_MQ + MLA_Q_RANK
C_KPE = C_MKV + MLA_KV_RANK
C_GQ = C_KPE + LANES
C_GK = C_GQ + QK_WIDTH
C_GV = C_GK + QK_WIDTH
C_GR = C_GV + GLA_WIDTH
C_RQ = C_GR + GLA_WIDTH
C_RK = C_RQ + QK_WIDTH
C_RV = C_RK + QK_WIDTH
C_RG = C_RV + RET_WIDTH
D_IN_EXT = C_RG + RET_WIDTH

LOG2_E = math.log2(math.e)
LOOKAHEAD = 4
KV_UNROLL = 1
V_ROWS = 80
VMEM_LIMIT = 56 * 1024 * 1024


def _rms(x, g):
    return x * lax.rsqrt(jnp.mean(x * x, axis=-1, keepdims=True) + EPS) * g


def _pack_rows(x):
    return pltpu.bitcast(x, jnp.uint32)


def _unpack_rows(x):
    return pltpu.bitcast(x, BF16)


def _silu(x):
    return x * jax.nn.sigmoid(x)


def _cparams(*sem):
    return pltpu.CompilerParams(dimension_semantics=sem, vmem_limit_bytes=VMEM_LIMIT)


def _const_spec(shape):
    nd = len(shape)
    return pl.BlockSpec(shape, lambda *_: (0,) * nd, pipeline_mode=pl.Buffered(1))


def _layer_spec(stacked, layer):
    nd = stacked.ndim
    return pl.BlockSpec((None,) + stacked.shape[1:], lambda *_: (layer,) + (0,) * (nd - 1),
                        pipeline_mode=pl.Buffered(1))


def _ffn_kernel(*refs, tf, with_mixer_out):
    if with_mixer_out:
        x_ref, om_ref, ol_ref, wa_ref, wb_ref, mpost_ref = refs[:6]
        refs = refs[6:]
        m = (lax.dot_general(om_ref[...], wa_ref[...], (((0,), (0,)), ((), ())), preferred_element_type=F32)
             + jnp.dot(ol_ref[...], wb_ref[...], preferred_element_type=F32))
        x = x_ref[...] + _rms(m, mpost_ref[...])
    else:
        x_ref = refs[0]
        refs = refs[1:]
        x = x_ref[...]
    pre_ref, post_ref, wg_ref, wu_ref, wd_ref, o_ref, h_ref = refs
    xn = _rms(x, pre_ref[...]).astype(BF16)
    for c in range(D_FF // tf):
        sl = slice(c * tf, (c + 1) * tf)
        g = jnp.dot(xn, wg_ref[:, sl].astype(BF16), preferred_element_type=F32)
        u = jnp.dot(xn, wu_ref[:, sl].astype(BF16), preferred_element_type=F32)
        h_ref[:, sl] = (_silu(g) * u).astype(BF16)
    f = jnp.dot(h_ref[...], wd_ref[...].astype(BF16), preferred_element_type=F32)
    o_ref[...] = x + 0.5 * _rms(f, post_ref[...])


def _ffn(x, layer, fw, *, tm, tf, mixer_out=None):
    t = x.shape[0]
    row = lambda w: pl.BlockSpec((tm, w), lambda i: (i, 0))
    args, specs = [x], [row(D_MODEL)]
    if mixer_out is not None:
        o_mla, o_lin, wa, wb, mpost = mixer_out
        args += [o_mla, o_lin, wa, wb, mpost]
        specs += [pl.BlockSpec((MLA_WIDTH, tm), lambda i: (0, i)), row(GLA_WIDTH + RET_WIDTH)]
        specs += [_layer_spec(a, layer) for a in (wa, wb, mpost)]
    args += list(fw)
    specs += [_layer_spec(a, layer) for a in fw]
    return pl.pallas_call(
        functools.partial(_ffn_kernel, tf=tf, with_mixer_out=mixer_out is not None),
        out_shape=jax.ShapeDtypeStruct(x.shape, F32),
        grid=(t // tm,),
        in_specs=specs,
        out_specs=row(D_MODEL),
        scratch_shapes=[pltpu.VMEM((tm, D_FF), BF16)],
        compiler_params=_cparams("parallel"),
        name="ffn_mixer_out" if mixer_out is not None else "ffn",
    )(*args)


def _rope_kernel(pos_ref, pos_dense_ref, inv_ref, phase_ref, tab_ref, cos_step_ref, sin_step_ref):
    tm = pos_ref.shape[0]
    inv = inv_ref[...]
    phase = phase_ref[...]

    @pl.when(pl.program_id(0) == 0)
    def _():
        ang = lax.broadcasted_iota(jnp.int32, (tm, 1), 0).astype(F32) * inv
        cos_step_ref[...] = jnp.cos(ang)
        sin_step_ref[...] = jnp.sin(ang)

    dense = pos_dense_ref[...]
    first = dense[0:1, 0:1]
    dense_step = (lax.broadcasted_iota(jnp.int32, dense.shape, 0) * LANES
                  + lax.broadcasted_iota(jnp.int32, dense.shape, 1))
    consecutive = jnp.min(jnp.where(dense == first + dense_step, 1, 0))

    @pl.when(consecutive == 1)
    def _():
        base = first.astype(F32) * inv - phase
        tab_ref[...] = jnp.cos(base) * cos_step_ref[...] - jnp.sin(base) * sin_step_ref[...]

    @pl.when(consecutive != 1)
    def _():
        tab_ref[...] = jnp.cos(pos_ref[...].astype(F32) * inv - phase)


def _rope_table(pos_col, inv_lane, phase_lane, *, tm):
    t = pos_col.shape[0]
    assert tm % (8 * LANES) == 0
    return pl.pallas_call(
        _rope_kernel,
        out_shape=jax.ShapeDtypeStruct((t, LANES), F32),
        grid=(t // tm,),
        in_specs=[pl.BlockSpec((tm, 1), lambda i: (i, 0)), pl.BlockSpec((tm // LANES, LANES), lambda i: (i, 0)),
                  _const_spec((1, LANES)), _const_spec((1, LANES))],
        out_specs=pl.BlockSpec((tm, LANES), lambda i: (i, 0)),
        scratch_shapes=[pltpu.VMEM((tm, LANES), F32), pltpu.VMEM((tm, LANES), F32)],
        compiler_params=_cparams("arbitrary"),
        name="rope_table",
    )(pos_col, pos_col.reshape(t // LANES, LANES), inv_lane, phase_lane)


def _proj_kernel(x_ref, rope_ref, pre_ref, win_ref, qg_ref, wq_ref, kvg_ref, wk_ref, wvt_ref,
                 wgate_ref, bgate_ref, tri_ref,
                 q_ref, k_ref, gq_ref, gk_ref, b_ref, gv_ref, sg_ref,
                 rq_ref, rk_ref, rv_ref, sr_ref, vt_ref):
    tm = x_ref.shape[0]
    xn = _rms(x_ref[...], pre_ref[...]).astype(BF16)

    chunk_starts = (0, C_GQ, C_RQ, D_IN_EXT)
    chunks = [jnp.dot(xn, win_ref[:, a:b], preferred_element_type=F32)
              for a, b in zip(chunk_starts[:-1], chunk_starts[1:])]

    def proj(c0, width):
        for a, b, chunk in zip(chunk_starts[:-1], chunk_starts[1:], chunks):
            if a <= c0 and c0 + width <= b:
                return chunk[:, c0 - a:c0 - a + width]
        raise ValueError("column group straddles two chunks")

    lane = lax.broadcasted_iota(jnp.int32, (tm, LANES), 1)
    tab = rope_ref[...]
    swapped = pltpu.roll(tab, LANES // 2, axis=1)
    cosb = jnp.where(lane < LANES // 2, tab, swapped)
    sinb = jnp.where(lane < LANES // 2, swapped, tab)
    first_half = (lane % ROPE_DIM) < HALF_ROPE
    sins = jnp.where(first_half, -sinb, sinb)

    def rot_half(v):
        return jnp.where(first_half, pltpu.roll(v, LANES - HALF_ROPE, axis=1), pltpu.roll(v, HALF_ROPE, axis=1))

    hq = _rms(proj(C_MQ, MLA_Q_RANK), qg_ref[...]).astype(BF16)
    in_rope = (lane >= MLA_NOPE) & (lane < MLA_NOPE + ROPE_DIM)
    qcos = jnp.where(in_rope, cosb, 1.0)
    qsin = jnp.where(in_rope, sins, 0.0)
    q_scale = (MLA_NOPE + ROPE_DIM) ** -0.5 * LOG2_E
    qa = jnp.dot(hq, wq_ref[...], preferred_element_type=F32)
    for h in range(MLA_HEADS):
        sl = slice(h * LANES, (h + 1) * LANES)
        q_ref[:, sl] = _pack_rows(((qa[:, sl] * qcos + rot_half(qa[:, sl]) * qsin) * q_scale).astype(BF16))

    hkv = _rms(proj(C_MKV, MLA_KV_RANK), kvg_ref[...]).astype(BF16)
    g1 = proj(C_KPE, LANES)
    kpe = jnp.where(in_rope, g1 * cosb + rot_half(g1) * sins, 0.0)
    ones_row = jnp.where(lax.broadcasted_iota(jnp.int32, (V_ROWS, tm), 0) == MLA_V, 1.0, 0.0)
    kn = jnp.dot(hkv, wk_ref[...], preferred_element_type=F32)
    vt = lax.dot_general(wvt_ref[...], hkv, (((1,), (1,)), ((), ())), preferred_element_type=F32)
    for h in range(MLA_HEADS):
        sl = slice(h * LANES, (h + 1) * LANES)
        vs = slice(h * V_ROWS, (h + 1) * V_ROWS)
        k_ref[:, sl] = _pack_rows((kn[:, sl] + kpe).astype(BF16))
        vt_ref[h * V_ROWS // 2:(h + 1) * V_ROWS // 2, :] = _pack_rows((vt[vs, :] + ones_row).astype(BF16))

    z = jnp.dot(g1.astype(BF16), wgate_ref[...], preferred_element_type=F32) + bgate_ref[...]
    log_a = (jnp.minimum(z, 0.0) - jnp.log1p(jnp.exp(-jnp.abs(z)))) / GLA_TAU
    la_hi = log_a.astype(BF16)
    la_lo = (log_a - la_hi.astype(F32)).astype(BF16)
    la_split = jnp.concatenate([la_hi, la_lo], axis=1)
    tri = tri_ref[...]
    tb = tri.shape[0]
    for r in range(tm // tb):
        rs = slice(r * tb, (r + 1) * tb)
        cum = jnp.dot(tri, la_split[rs], preferred_element_type=F32)
        b_ref[rs, :] = cum[:, :QK_WIDTH] + cum[:, QK_WIDTH:]
    gq_ref[...] = (proj(C_GQ, QK_WIDTH) * (GLA_DK ** -0.5)).astype(BF16)
    gk_ref[...] = proj(C_GK, QK_WIDTH).astype(BF16)
    gv_ref[...] = proj(C_GV, GLA_WIDTH).astype(BF16)
    sg_ref[...] = _silu(proj(C_GR, GLA_WIDTH)).astype(BF16)

    rq = proj(C_RQ, QK_WIDTH)
    rk = proj(C_RK, QK_WIDTH)
    rq_ref[...] = (rq * cosb + rot_half(rq) * sins).astype(BF16)
    rk_ref[...] = ((rk * cosb + rot_half(rk) * sins) * (RET_DK ** -0.5)).astype(BF16)
    rv_ref[...] = proj(C_RV, RET_WIDTH).astype(BF16)
    sr_ref[...] = _silu(proj(C_RG, RET_WIDTH)).astype(BF16)


def _proj(x, rope_tab, layer, mw, tri, *, tm):
    t = x.shape[0]
    row = lambda w: pl.BlockSpec((tm, w), lambda i: (i, 0))
    out_widths = [(QK_WIDTH, BF16), (QK_WIDTH, BF16), (QK_WIDTH, F32), (GLA_WIDTH, BF16), (GLA_WIDTH, BF16),
                  (QK_WIDTH, BF16), (QK_WIDTH, BF16), (RET_WIDTH, BF16), (RET_WIDTH, BF16)]
    stacked = [mw["mix_pre_g"], mw["w_in"], mw["q_norm_g"], mw["wq"], mw["kv_norm_g"], mw["wk"], mw["wvt"],
               mw["w_gate"], mw["b_gate"]]
    return pl.pallas_call(
        _proj_kernel,
        out_shape=(jax.ShapeDtypeStruct((t // 2, HEAD_PAD), jnp.uint32),) * 2
        + tuple(jax.ShapeDtypeStruct((t, w), dt) for w, dt in out_widths)
        + (jax.ShapeDtypeStruct((MLA_HEADS * V_ROWS // 2, t), jnp.uint32),),
        grid=(t // tm,),
        in_specs=[row(D_MODEL), row(LANES)] + [_layer_spec(a, layer) for a in stacked]
        + [_const_spec(tri.shape)],
        out_specs=(pl.BlockSpec((tm // 2, HEAD_PAD), lambda i: (i, 0)),) * 2 + tuple(row(w) for w, _ in out_widths)
        + (pl.BlockSpec((MLA_HEADS * V_ROWS // 2, tm), lambda i: (0, i)),),
        compiler_params=_cparams("parallel"),
        name="mixer_proj",
    )(x, rope_tab, *stacked, tri)


def _attn_kernel(q_ref, k_ref, vt_ref, o_ref, m_ref, acc_ref, st_ref, *, tk, tn):
    tq = 2 * q_ref.shape[0]
    qi = pl.program_id(1)
    nt = (((1,), (1,)), ((), ()))
    items = [(h, c) for h in range(MLA_HEADS) for c in range(tq // tn)]

    def key_rows(j, n):
        return pl.ds(pl.multiple_of(j * tk, tk), n)

    def packed_key_rows(j, n):
        return pl.ds(pl.multiple_of(j * (tk // 2), tk // 2), n // 2)

    def n_keys(item, diag):
        return (item[1] + 1) * tn if diag else tk

    def scores(j, item, diag):
        h, c = item
        hs = slice(h * LANES, (h + 1) * LANES)
        keys = _unpack_rows(k_ref[packed_key_rows(j, n_keys(item, diag)), hs])
        queries = _unpack_rows(q_ref[c * tn // 2:(c + 1) * tn // 2, hs])
        return lax.dot_general(keys, queries, nt, preferred_element_type=F32).astype(BF16)

    key_chunk = lax.broadcasted_iota(jnp.int32, (tn, tn), 0) // CHUNK
    qry_chunk = lax.broadcasted_iota(jnp.int32, (tn, tn), 1) // CHUNK
    diag_mask = jnp.where(key_chunk <= qry_chunk, 0.0, -jnp.inf).astype(BF16)

    def consume(j, item, st, diag):
        h, c = item
        vs = slice(h * V_ROWS // 2, (h + 1) * V_ROWS // 2)
        cs = slice(c * tn, (c + 1) * tn)
        nk = n_keys(item, diag)
        st = st[:nk]
        if diag:
            own = st[nk - tn:] + diag_mask
            st = own if nk == tn else jnp.concatenate([st[:nk - tn], own], axis=0)
        m_old = m_ref[h, :, cs]
        m_new = jnp.maximum(m_old, jnp.max(st, axis=0, keepdims=True).astype(F32))
        p = jnp.exp2(st - m_new.astype(BF16))
        acc_ref[h, :, cs] = (acc_ref[h, :, cs] * jnp.exp2(m_old - m_new)
                             + jnp.dot(_unpack_rows(vt_ref[vs, key_rows(j, nk)]), p, preferred_element_type=F32))
        m_ref[h, :, cs] = m_new

    for h in range(MLA_HEADS):
        m_ref[h] = jnp.full((1, tq), -jnp.inf, F32)
        acc_ref[h] = jnp.zeros((V_ROWS, tq), F32)

    for a in range(LOOKAHEAD):
        st_ref[a] = scores(0, items[a], False)

    def sweep(j, diag):
        pending = [st_ref[a] for a in range(LOOKAHEAD)]
        for i, item in enumerate(items):
            ahead = i + LOOKAHEAD
            if ahead < len(items):
                pending.append(scores(j, items[ahead], diag))
            elif not diag:
                pending.append(scores(j + 1, items[ahead - len(items)], False))
            consume(j, item, pending.pop(0), diag)
        for a, st in enumerate(pending):
            st_ref[a] = st

    def body(jj, carry):
        for u in range(KV_UNROLL):
            sweep(jj * KV_UNROLL + u, False)
        return carry

    n_full = qi // KV_UNROLL
    lax.fori_loop(0, n_full, body, 0)
    lax.fori_loop(n_full * KV_UNROLL, qi, lambda j, c: (sweep(j, False), c)[1], 0)
    sweep(qi, True)

    for h in range(MLA_HEADS):
        acc = acc_ref[h]
        o_ref[h * MLA_V:(h + 1) * MLA_V, :] = (acc[:MLA_V] / acc[MLA_V:MLA_V + 1, :]).astype(BF16)


def _attention(q, k, vt, *, batch, seq, tq, tk, tn):
    nq = seq // tq
    assert tq == tk and tq % tn == 0
    return pl.pallas_call(
        functools.partial(_attn_kernel, tk=tk, tn=tn),
        out_shape=jax.ShapeDtypeStruct((MLA_WIDTH, batch * seq), BF16),
        grid=(batch, nq),
        in_specs=[
            pl.BlockSpec((tq // 2, HEAD_PAD), lambda b, i: (b * nq + i, 0)),
            pl.BlockSpec((seq // 2, HEAD_PAD), lambda b, i: (b, 0)),
            pl.BlockSpec((MLA_HEADS * V_ROWS // 2, seq), lambda b, i: (0, b)),
        ],
        out_specs=pl.BlockSpec((MLA_WIDTH, tq), lambda b, i: (0, b * nq + i)),
        scratch_shapes=[pltpu.VMEM((MLA_HEADS, 1, tq), F32), pltpu.VMEM((MLA_HEADS, V_ROWS, tq), F32),
                        pltpu.VMEM((LOOKAHEAD, tk, tn), BF16)],
        compiler_params=_cparams("parallel", "arbitrary"),
        name="mla_attention",
    )(q, k, vt)


def _head_rows(x, n_heads, head_lanes):
    lane_head = lax.broadcasted_iota(jnp.int32, x.shape, 1) // head_lanes
    return jnp.concatenate([jnp.where(lane_head == h, x, 0.0).astype(BF16) for h in range(n_heads)], axis=0)


def _linmix_kernel(gq_ref, gk_ref, b_ref, gv_ref, sg_ref, rq_ref, rk_ref, rv_ref, sr_ref,
                   gng_ref, rng_ref, lgam_ref, ones_ref, o_ref, sg_state, sr_state, og_ref, or_ref):
    tc = gq_ref.shape[0]
    L = CHUNK

    @pl.when(pl.program_id(1) == 0)
    def _():
        sg_state[...] = jnp.zeros(sg_state.shape, F32)
        sr_state[...] = jnp.zeros(sr_state.shape, F32)

    ri = lax.broadcasted_iota(jnp.int32, (L, GLA_HEADS * L), 0)
    cj = lax.broadcasted_iota(jnp.int32, (L, GLA_HEADS * L), 1) % L
    lower = cj <= ri
    lgam = lgam_ref[...]
    lgam_cols = jnp.concatenate(
        [jnp.broadcast_to(lgam[:, h * RET_DK:h * RET_DK + 1], (1, L)) for h in range(RET_HEADS)], axis=1)
    ret_decay = jnp.exp(lgam_cols * jnp.abs(ri - cj).astype(F32))
    pos = lax.broadcasted_iota(jnp.int32, (L, QK_WIDTH), 0).astype(F32)
    q_dec = jnp.exp(lgam * (pos + 1.0))
    k_dec = jnp.exp(lgam * (L - 1.0 - pos))
    chunk_dec = jnp.exp(lgam * float(L))
    state_head = lax.broadcasted_iota(jnp.int32, (GLA_WIDTH, QK_WIDTH), 0) // GLA_DV
    state_lane_head = lax.broadcasted_iota(jnp.int32, (GLA_WIDTH, QK_WIDTH), 1) // GLA_DK
    state_diag = state_head == state_lane_head
    nt = (((1,), (1,)), ((), ()))

    for c in range(tc // L):
        rs = slice(c * L, (c + 1) * L)
        b = b_ref[rs, :]
        b_last = b[L - 1:L, :]
        eb, enb = jnp.exp(b), jnp.exp(-b)
        q = gq_ref[rs, :].astype(F32)
        k = gk_ref[rs, :].astype(F32)
        v = gv_ref[rs, :].astype(F32)
        q_fwd = (q * eb).astype(BF16)
        a_low = lax.dot_general(q_fwd, _head_rows(k * enb, GLA_HEADS, GLA_DK), nt, preferred_element_type=F32)
        a_up = lax.dot_general((q * enb).astype(BF16), _head_rows(k * eb, GLA_HEADS, GLA_DK), nt,
                               preferred_element_type=F32)
        attn = jnp.where(lower, a_low, a_up).astype(BF16)
        o_intra = jnp.dot(attn, _head_rows(v, GLA_HEADS, GLA_DV), preferred_element_type=F32)
        s_t = sg_state[...]
        o_inter = lax.dot_general(q_fwd, s_t.astype(BF16), nt, preferred_element_type=F32)
        og_ref[rs, :] = o_intra + o_inter
        k_state = (k * jnp.exp(b_last - b)).astype(BF16)
        d_state = jnp.dot(v.T.astype(BF16), k_state, preferred_element_type=F32)
        sg_state[...] = s_t * jnp.exp(b_last) + jnp.where(state_diag, d_state, 0.0)

        q = rq_ref[rs, :].astype(F32)
        k = rk_ref[rs, :].astype(F32)
        v = rv_ref[rs, :].astype(F32)
        scores = lax.dot_general(q.astype(BF16), _head_rows(k, RET_HEADS, RET_DK), nt,
                                 preferred_element_type=F32) * ret_decay
        o_intra = jnp.dot(scores.astype(BF16), _head_rows(v, RET_HEADS, RET_DV), preferred_element_type=F32)
        s_t = sr_state[...]
        o_inter = lax.dot_general((q * q_dec).astype(BF16), s_t.astype(BF16), nt, preferred_element_type=F32)
        or_ref[rs, :] = o_intra + o_inter
        d_state = jnp.dot(v.T.astype(BF16), (k * k_dec).astype(BF16), preferred_element_type=F32)
        sr_state[...] = s_t * chunk_dec + jnp.where(state_diag, d_state, 0.0)

    def head_norm(o, g, gate):
        o2 = o * o
        hi = o2.astype(BF16)
        lo = (o2 - hi.astype(F32)).astype(BF16)
        ss = (jnp.dot(hi, ones_ref[...], preferred_element_type=F32)
              + jnp.dot(lo, ones_ref[...], preferred_element_type=F32))
        return o * lax.rsqrt(ss * (1.0 / GLA_DV) + EPS) * g * gate

    o_ref[:, :GLA_WIDTH] = head_norm(og_ref[...], gng_ref[...], sg_ref[...].astype(F32)).astype(BF16)
    o_ref[:, GLA_WIDTH:] = head_norm(or_ref[...], rng_ref[...], sr_ref[...].astype(F32)).astype(BF16)


def _linmix(gq, gk, b, gv, sg, rq, rk, rv, sr, layer, gng, rng, lgam, ones_bd, *, batch, seq, tc):
    nt = seq // tc
    row = lambda w: pl.BlockSpec((tc, w), lambda bb, i: (bb * nt + i, 0))
    return pl.pallas_call(
        _linmix_kernel,
        out_shape=jax.ShapeDtypeStruct((batch * seq, GLA_WIDTH + RET_WIDTH), BF16),
        grid=(batch, nt),
        in_specs=[row(QK_WIDTH), row(QK_WIDTH), row(QK_WIDTH), row(GLA_WIDTH), row(GLA_WIDTH),
                  row(QK_WIDTH), row(QK_WIDTH), row(RET_WIDTH), row(RET_WIDTH),
                  _layer_spec(gng, layer), _layer_spec(rng, layer), _const_spec(lgam.shape),
                  _const_spec(ones_bd.shape)],
        out_specs=row(GLA_WIDTH + RET_WIDTH),
        scratch_shapes=[pltpu.VMEM((GLA_WIDTH, QK_WIDTH), F32), pltpu.VMEM((RET_WIDTH, QK_WIDTH), F32),
                        pltpu.VMEM((tc, GLA_WIDTH), F32), pltpu.VMEM((tc, RET_WIDTH), F32)],
        compiler_params=_cparams("parallel", "arbitrary"),
        name="gla_retention",
    )(gq, gk, b, gv, sg, rq, rk, rv, sr, gng, rng, lgam, ones_bd)


def _prep_mixer_layer(p):
    w_in = p["w_in"]
    splits = (MLA_Q_RANK, MLA_KV_RANK, ROPE_DIM, QK_WIDTH, QK_WIDTH, GLA_WIDTH, GLA_GATE_RANK, GLA_WIDTH,
              QK_WIDTH, QK_WIDTH, RET_WIDTH, RET_WIDTH)
    offs = [0]
    for s in splits:
        offs.append(offs[-1] + s)
    mq, mkv, mkr, gq, gk, gv, gg, gr, rq, rk, rv, rg = (w_in[:, offs[i]:offs[i + 1]] for i in range(12))
    d = w_in.shape[0]
    zeros = lambda n: jnp.zeros((d, n), F32)
    kpe_grp = jnp.concatenate([gg, zeros(MLA_NOPE - GLA_GATE_RANK), mkr, zeros(LANES - MLA_NOPE - ROPE_DIM)], 1)
    w_in_ext = jnp.concatenate([mq, mkv, kpe_grp, gq, gk, gv, gr, rq, rk, rv, rg], axis=1).astype(BF16)

    wq = p["mla_w_q_up"].reshape(MLA_Q_RANK, MLA_HEADS, MLA_NOPE + ROPE_DIM)
    pad_q = jnp.zeros((MLA_Q_RANK, MLA_HEADS, LANES - MLA_NOPE - ROPE_DIM), F32)
    wq_main = jnp.concatenate([wq, pad_q], axis=-1).reshape(MLA_Q_RANK, HEAD_PAD)
    wkv = p["mla_w_kv_up"].reshape(MLA_KV_RANK, MLA_HEADS, MLA_NOPE + MLA_V)
    pad_kv = jnp.zeros((MLA_KV_RANK, MLA_HEADS, LANES - MLA_NOPE), F32)
    wk_pad = jnp.concatenate([wkv[..., :MLA_NOPE], pad_kv], axis=-1).reshape(MLA_KV_RANK, HEAD_PAD)
    pad_v = jnp.zeros((MLA_KV_RANK, MLA_HEADS, V_ROWS - MLA_V), F32)
    wv_pad = jnp.concatenate([wkv[..., MLA_NOPE:], pad_v], axis=-1).reshape(MLA_KV_RANK, MLA_HEADS * V_ROWS)
    w_gate = jnp.concatenate([p["gla_w_gate_up"], jnp.zeros((LANES - GLA_GATE_RANK, QK_WIDTH), F32)], axis=0)
    row = lambda a: a.reshape(1, -1)
    w_out = p["w_out"].astype(BF16)
    return {
        "mix_pre_g": row(p["mix_pre_g"]), "mix_post_g": row(p["mix_post_g"]),
        "w_in": w_in_ext,
        "q_norm_g": row(p["mla_q_norm_g"]),
        "wq": wq_main.astype(BF16),
        "kv_norm_g": row(p["mla_kv_norm_g"]),
        "wk": wk_pad.astype(BF16), "wvt": wv_pad.T.astype(BF16),
        "w_gate": w_gate.astype(BF16), "b_gate": row(p["gla_b_gate"]),
        "gla_norm_g": row(jnp.tile(p["gla_norm_g"], GLA_HEADS)),
        "ret_norm_g": row(jnp.tile(p["ret_norm_g"], RET_HEADS)),
        "w_out_mla": w_out[:MLA_WIDTH], "w_out_lin": w_out[MLA_WIDTH:],
    }


def _ffn_params(p, prefix):
    gain = lambda g: g.reshape(g.shape[0], 1, g.shape[1])
    return (gain(p[prefix + "_pre_g"]), gain(p[prefix + "_post_g"]),
            p[prefix + "_w_gate"], p[prefix + "_w_up"], p[prefix + "_w_down"])


def _tile(n, pref):
    while n % pref:
        pref //= 2
    return pref


def kernel(x, positions, ffn1_pre_g, ffn1_post_g, ffn1_w_gate, ffn1_w_up, ffn1_w_down, mix_pre_g, mix_post_g, w_in, mla_q_norm_g, mla_w_q_up, mla_kv_norm_g, mla_w_kv_up, gla_w_gate_up, gla_b_gate, gla_norm_g, ret_norm_g, w_out, ffn2_pre_g, ffn2_post_g, ffn2_w_gate, ffn2_w_up, ffn2_w_down):
    p = dict(ffn1_pre_g=ffn1_pre_g, ffn1_post_g=ffn1_post_g, ffn1_w_gate=ffn1_w_gate, ffn1_w_up=ffn1_w_up,
             ffn1_w_down=ffn1_w_down, ffn2_pre_g=ffn2_pre_g, ffn2_post_g=ffn2_post_g, ffn2_w_gate=ffn2_w_gate,
             ffn2_w_up=ffn2_w_up, ffn2_w_down=ffn2_w_down)
    mixer = dict(mix_pre_g=mix_pre_g, mix_post_g=mix_post_g, w_in=w_in, mla_q_norm_g=mla_q_norm_g,
                 mla_w_q_up=mla_w_q_up, mla_kv_norm_g=mla_kv_norm_g, mla_w_kv_up=mla_w_kv_up,
                 gla_w_gate_up=gla_w_gate_up, gla_b_gate=gla_b_gate, gla_norm_g=gla_norm_g,
                 ret_norm_g=ret_norm_g, w_out=w_out)
    batch, seq, d = x.shape
    t = batch * seq
    depth = w_in.shape[0]
    tm = _tile(seq, 512)
    tq = _tile(seq, 1024)
    tc = _tile(seq, 2048)

    inv = ROPE_BASE ** (-jnp.arange(0, ROPE_DIM, 2, dtype=F32) / ROPE_DIM)
    inv_lane = jnp.tile(inv, LANES // HALF_ROPE).reshape(1, LANES)
    tb = _tile(tm, 256)
    idx = jnp.arange(tb)
    tri = ((idx[:, None] >= idx[None, :]) & (idx[:, None] // CHUNK == idx[None, :] // CHUNK)).astype(BF16)
    hd = jnp.arange(GLA_WIDTH) // GLA_DV
    ones_bd = (hd[:, None] == hd[None, :]).astype(BF16)
    log_gamma = jnp.log1p(-jnp.exp2(-5.0 - jnp.arange(RET_HEADS, dtype=F32)))
    lgam = jnp.repeat(log_gamma, RET_DK).reshape(1, QK_WIDTH)

    fw1 = _ffn_params(p, "ffn1")
    fw2 = _ffn_params(p, "ffn2")
    mw = jax.vmap(_prep_mixer_layer)(mixer)
    phase_lane = jnp.where(jnp.arange(LANES) < LANES // 2, 0.0, math.pi / 2).astype(F32).reshape(1, LANES)
    rope_tab = _rope_table(positions.reshape(t, 1), inv_lane, phase_lane, tm=_tile(seq, 8 * LANES))

    xf = x.reshape(t, d)
    for l in range(depth):
        xf = _ffn(xf, l, fw1, tm=tm, tf=256)
        (q, k, gq, gk, b, gv, sg, rq, rk, rv, sr, vt) = _proj(xf, rope_tab, l, mw, tri, tm=_tile(seq, 2 * tm))
        o_mla = _attention(q, k, vt, batch=batch, seq=seq, tq=tq, tk=tq, tn=_tile(tq, 256))
        o_lin = _linmix(gq, gk, b, gv, sg, rq, rk, rv, sr, l, mw["gla_norm_g"], mw["ret_norm_g"], lgam, ones_bd,
                        batch=batch, seq=seq, tc=tc)
        xf = _ffn(xf, l, fw2, tm=tm, tf=256,
                  mixer_out=(o_mla, o_lin, mw["w_out_mla"], mw["w_out_lin"], mw["mix_post_g"]))
    return xf.reshape(batch, seq, d)
```
